```python
import jax, jax.numpy as jnp
from jax import lax
import numpy as np

D_MODEL = 1024
BATCH = 8
SEQ = 2048
DEPTH = 4
DEC_BATCH = 128
DEC_SEQ = 8
PAST_LEN = 16384
PAGE_SIZE = 128

N_EVEN = (DEPTH + 1) // 2
N_ODD = DEPTH // 2
CHUNK = 128
A_HEADS = 4
A_WIDTH = D_MODEL // 2
A_HEAD_DIM = A_WIDTH // A_HEADS
B_GROUPS = 4
B_WIDTH = D_MODEL // 2
B_CONV = 3
C_WIDTH = D_MODEL // 2
C_CONV = 31
D_WIDTH = D_MODEL // 2
POOL_WINDOWS = (2, 4, 8, 16)
D_GROUPS = len(POOL_WINDOWS)
D_GROUP_DIM = D_WIDTH // D_GROUPS
POOL_BUF = max(POOL_WINDOWS) - 1
D_FF = ((8 * D_MODEL // 3 + 127) // 128) * 128
FFN_CONV = 3
EVEN_IN = 2 * A_WIDTH + 3 * B_WIDTH
ODD_IN = 2 * C_WIDTH + D_WIDTH
EVEN_OUT = A_WIDTH + B_WIDTH
ODD_OUT = C_WIDTH + D_WIDTH
EPS = 1e-6

kernel_name = "hybrid_chunkmlp_shortconv_conformer_pool_decoder_step"


def rms_norm(x, g):
    xf = x.astype(jnp.float32)
    y = xf * lax.rsqrt(jnp.mean(xf * xf, axis=-1, keepdims=True) + EPS)
    return (y * g.astype(jnp.float32)).astype(x.dtype)


def layer_norm(x, g, b=None):
    xf = x.astype(jnp.float32)
    mu = jnp.mean(xf, axis=-1, keepdims=True)
    xc = xf - mu
    y = xc * lax.rsqrt(jnp.mean(xc * xc, axis=-1, keepdims=True) + EPS) * g.astype(jnp.float32)
    if b is not None:
        y = y + b.astype(jnp.float32)
    return y.astype(x.dtype)


def causal_dwconv(x, buf, w, b=None):
    k = w.shape[0]
    xp = jnp.concatenate([buf.astype(x.dtype), x], axis=1)
    y = lax.conv_general_dilated(
        xp, w[:, None, :].astype(x.dtype), window_strides=(1,), padding='VALID',
        dimension_numbers=('NWC', 'WIO', 'NWC'), feature_group_count=x.shape[-1])
    if b is not None:
        y = y + b.astype(y.dtype)
    return y, xp[:, xp.shape[1] - (k - 1):]


def chunk_spatial_gate(u, v, ws, bs):
    n, l, h, p = v.shape
    lc = min(l, CHUNK)
    nc = l // lc
    w = jnp.tril(ws[:, :lc, :lc]).astype(v.dtype)
    vc = v.reshape(n, nc, lc, h, p)
    mixed = jnp.einsum('hts,ncshp->ncthp', w, vc)
    mixed = mixed + bs[:, :lc].T.astype(v.dtype)[None, None, :, :, None]
    return u * mixed.reshape(n, l, h, p)


def multi_scale_pool(p, buf, start_pos, proj, scale):
    n, l, _ = p.shape
    xp = jnp.concatenate([buf.astype(p.dtype), p], axis=1)
    cs = jnp.cumsum(xp.astype(jnp.float32), axis=1)
    cs = jnp.concatenate([jnp.zeros_like(cs[:, :1]), cs], axis=1)
    pos = start_pos + jnp.arange(l, dtype=jnp.int32)
    hi = cs[:, POOL_BUF + 1:]
    outs = []
    for gi, win in enumerate(POOL_WINDOWS):
        sl = slice(gi * D_GROUP_DIM, (gi + 1) * D_GROUP_DIM)
        lo = cs[:, POOL_BUF + 1 - win:POOL_BUF + 1 - win + l, sl]
        cnt = jnp.minimum(pos + 1, win).astype(jnp.float32)[None, :, None]
        outs.append((hi[:, :, sl] - lo) / cnt)
    pooled = jnp.stack(outs, axis=2).astype(p.dtype)
    diff = pooled - p.reshape(n, l, D_GROUPS, D_GROUP_DIM)
    y = jnp.einsum('nlgc,gcd->nlgd', diff, proj).reshape(n, l, D_WIDTH) * scale
    return y, xp[:, xp.shape[1] - POOL_BUF:]


def even_mixer(xn, buf_b, w_in, w_out, a_ln_g, a_ws, a_bs, b_conv_w):
    n, l, _ = xn.shape
    z = xn @ w_in
    u, v, bg, cg, h = jnp.split(
        z, [A_WIDTH, 2 * A_WIDTH, 2 * A_WIDTH + B_WIDTH, 2 * A_WIDTH + 2 * B_WIDTH], axis=-1)
    u = jax.nn.gelu(u)
    v = layer_norm(jax.nn.gelu(v), a_ln_g)
    ya = chunk_spatial_gate(u.reshape(n, l, A_HEADS, A_HEAD_DIM),
                            v.reshape(n, l, A_HEADS, A_HEAD_DIM), a_ws, a_bs).reshape(n, l, A_WIDTH)
    conv_out, new_b = causal_dwconv(cg * h, buf_b, b_conv_w)
    yb = bg * conv_out
    y = jnp.concatenate([ya, yb], axis=-1) @ w_out
    return y, v, new_b


def odd_mixer(xn, buf_c, buf_d, start_pos, w_in, w_out, c_conv_w, c_conv_b, c_ln_g, c_ln_b,
              d_proj, d_scale):
    z = xn @ w_in
    ca, cb, p = jnp.split(z, [C_WIDTH, 2 * C_WIDTH], axis=-1)
    g = ca * jax.nn.sigmoid(cb)
    cc, new_c = causal_dwconv(g, buf_c, c_conv_w, c_conv_b)
    yc = jax.nn.silu(layer_norm(cc, c_ln_g, c_ln_b))
    yd, new_d = multi_scale_pool(p, buf_d, start_pos, d_proj, d_scale)
    y = jnp.concatenate([yc, yd], axis=-1) @ w_out
    return y, new_c, new_d


def conv_ffn(xn, buf, w_in, conv_w, w_out):
    z = xn @ w_in
    gate, up = jnp.split(z, [D_FF], axis=-1)
    gc, new_buf = causal_dwconv(gate, buf, conv_w)
    return (jax.nn.gelu(gc) * up) @ w_out, new_buf


def trunk(x, buf_b, buf_c, buf_d, buf_f, start_pos, prm):
    new_a, new_b, new_c, new_d, new_f = [], [], [], [], []
    for layer in range(DEPTH):
        i = layer // 2
        xn = rms_norm(x, prm['norm_mix_g'][layer])
        if layer % 2 == 0:
            y, v, nb = even_mixer(xn, buf_b[i], prm['w_in_even'][i], prm['w_out_even'][i],
                                  prm['a_ln_g'][i], prm['a_ws'][i], prm['a_bs'][i], prm['b_conv_w'][i])
            new_a.append(v)
            new_b.append(nb)
        else:
            y, nc, nd = odd_mixer(xn, buf_c[i], buf_d[i], start_pos, prm['w_in_odd'][i],
                                  prm['w_out_odd'][i], prm['c_conv_w'][i], prm['c_conv_b'][i],
                                  prm['c_ln_g'][i], prm['c_ln_b'][i], prm['d_proj'][i], prm['d_scale'][i])
            new_c.append(nc)
            new_d.append(nd)
        x = x + y
        xn = rms_norm(x, prm['norm_ffn_g'][layer])
        y, nf = conv_ffn(xn, buf_f[layer], prm['w_ffn_in'][layer], prm['ffn_conv_w'][layer],
                         prm['w_ffn_out'][layer])
        new_f.append(nf)
        x = x + y
    out = rms_norm(x, prm['norm_final_g'])
    return out, jnp.stack(new_a), jnp.stack(new_b), jnp.stack(new_c), jnp.stack(new_d), jnp.stack(new_f)


def setup_inputs(seed: int = 0) -> dict:
    key = jax.random.key(seed)
    ks = iter(jax.random.split(key, 32))

    def nrm(shape, scale):
        return jax.random.normal(next(ks), shape, jnp.float32) * scale

    inp = {}
    inp['x_prompt'] = nrm((BATCH, SEQ, D_MODEL), 1.0)
    inp['x_sample'] = nrm((DEC_BATCH, DEC_SEQ, D_MODEL), 1.0)
    inp['state_b_conv'] = nrm((N_EVEN, DEC_BATCH, B_CONV - 1, B_WIDTH), 1.0)
    inp['state_c_conv'] = nrm((N_ODD, DEC_BATCH, C_CONV - 1, C_WIDTH), 0.5)
    inp['state_d_pool'] = nrm((N_ODD, DEC_BATCH, POOL_BUF, D_WIDTH), 1.0)
    inp['state_ffn_conv'] = nrm((DEPTH, DEC_BATCH, FFN_CONV - 1, D_FF), 1.0)
    inp['norm_mix_g'] = 1.0 + nrm((DEPTH, D_MODEL), 0.05)
    inp['norm_ffn_g'] = 1.0 + nrm((DEPTH, D_MODEL), 0.05)
    inp['norm_final_g'] = 1.0 + nrm((D_MODEL,), 0.05)
    inp['w_in_even'] = nrm((N_EVEN, D_MODEL, EVEN_IN), D_MODEL ** -0.5)
    inp['w_out_even'] = nrm((N_EVEN, EVEN_OUT, D_MODEL), EVEN_OUT ** -0.5)
    inp['a_ln_g'] = 1.0 + nrm((N_EVEN, A_WIDTH), 0.05)
    inp['a_ws'] = nrm((N_EVEN, A_HEADS, CHUNK, CHUNK), CHUNK ** -0.5)
    inp['a_bs'] = 1.0 + nrm((N_EVEN, A_HEADS, CHUNK), 0.1)
    inp['b_conv_w'] = nrm((N_EVEN, B_CONV, B_WIDTH), B_CONV ** -0.5)
    inp['w_in_odd'] = nrm((N_ODD, D_MODEL, ODD_IN), D_MODEL ** -0.5)
    inp['w_out_odd'] = nrm((N_ODD, ODD_OUT, D_MODEL), ODD_OUT ** -0.5)
    inp['c_conv_w'] = nrm((N_ODD, C_CONV, C_WIDTH), C_CONV ** -0.5)
    inp['c_conv_b'] = nrm((N_ODD, C_WIDTH), 0.02)
    inp['c_ln_g'] = 1.0 + nrm((N_ODD, C_WIDTH), 0.05)
    inp['c_ln_b'] = nrm((N_ODD, C_WIDTH), 0.02)
    inp['d_proj'] = nrm((N_ODD, D_GROUPS, D_GROUP_DIM, D_GROUP_DIM), D_GROUP_DIM ** -0.5)
    inp['d_scale'] = 1.0 + nrm((N_ODD, D_WIDTH), 0.1)
    inp['w_ffn_in'] = nrm((DEPTH, D_MODEL, 2 * D_FF), D_MODEL ** -0.5)
    inp['ffn_conv_w'] = nrm((DEPTH, FFN_CONV, D_FF), FFN_CONV ** -0.5)
    inp['w_ffn_out'] = nrm((DEPTH, D_FF, D_MODEL), D_FF ** -0.5)
    return inp


def reference(x_prompt, x_sample, state_b_conv, state_c_conv, state_d_pool, state_ffn_conv,
              norm_mix_g, norm_ffn_g, norm_final_g, w_in_even, w_out_even, a_ln_g, a_ws, a_bs,
              b_conv_w, w_in_odd, w_out_odd, c_conv_w, c_conv_b, c_ln_g, c_ln_b, d_proj, d_scale,
              w_ffn_in, ffn_conv_w, w_ffn_out):
    prm = dict(norm_mix_g=norm_mix_g, norm_ffn_g=norm_ffn_g, norm_final_g=norm_final_g,
               w_in_even=w_in_even, w_out_even=w_out_even, a_ln_g=a_ln_g, a_ws=a_ws, a_bs=a_bs,
               b_conv_w=b_conv_w, w_in_odd=w_in_odd, w_out_odd=w_out_odd, c_conv_w=c_conv_w,
               c_conv_b=c_conv_b, c_ln_g=c_ln_g, c_ln_b=c_ln_b, d_proj=d_proj, d_scale=d_scale,
               w_ffn_in=w_ffn_in, ffn_conv_w=ffn_conv_w, w_ffn_out=w_ffn_out)
    dt = x_prompt.dtype
    zb = jnp.zeros((N_EVEN, BATCH, B_CONV - 1, B_WIDTH), dt)
    zc = jnp.zeros((N_ODD, BATCH, C_CONV - 1, C_WIDTH), dt)
    zd = jnp.zeros((N_ODD, BATCH, POOL_BUF, D_WIDTH), dt)
    zf = jnp.zeros((DEPTH, BATCH, FFN_CONV - 1, D_FF), dt)
    y_prompt, _, b_p, c_p, d_p, f_p = trunk(x_prompt, zb, zc, zd, zf, 0, prm)
    y_sample, a_s, b_s, c_s, d_s, f_s = trunk(x_sample, state_b_conv, state_c_conv, state_d_pool,
                                              state_ffn_conv, PAST_LEN, prm)
    return (y_prompt, y_sample, a_s, b_p, b_s, c_p, c_s, d_p, d_s, f_p, f_s)
```

```python
import functools

import jax
import jax.numpy as jnp
from jax import lax
from jax.experimental import pallas as pl
from jax.experimental.pallas import tpu as pltpu

D_MODEL = 1024
CHUNK = 128
A_HEADS = 4
A_WIDTH = 512
B_WIDTH = 512
B_CONV = 3
C_WIDTH = 512
C_CONV = 31
D_WIDTH = 512
POOL_WINDOWS = (2, 4, 8, 16)
D_GROUP_DIM = 128
POOL_BUF = 15
D_FF = 2816
FFN_CONV = 3
PAST_LEN = 16384
EPS = 1e-6

SUBLANES = 8
LANES = 128
TILE_M = 512
ROW_BLOCK = 64
NORM_BLOCK = 32
FF_CHUNK = 256
SAMPLE_SEQS = TILE_M // 8
VMEM_LIMIT = 56 * 1024 * 1024

_BF16 = jnp.bfloat16
_F32 = jnp.float32


def _round_up(n, m):
    return (n + m - 1) // m * m


def _dot(a, b):
    return jnp.dot(a, b, preferred_element_type=_F32)


def _rms_rows(x, g):
    y = x * lax.rsqrt(jnp.mean(x * x, axis=-1, keepdims=True) + EPS)
    return y * g


def _ln_rows(x, g, b=None):
    mu = jnp.mean(x, axis=-1, keepdims=True)
    xc = x - mu
    y = xc * lax.rsqrt(jnp.mean(xc * xc, axis=-1, keepdims=True) + EPS) * g
    if b is not None:
        y = y + b
    return y


def _norm_to_bf16(x_ref, g_ref, xn_ref, tm):
    g = g_ref[...]
    for r0 in range(0, tm, NORM_BLOCK):
        rows = slice(r0, r0 + NORM_BLOCK)
        xn_ref[rows, :] = _rms_rows(x_ref[rows, :], g).astype(_BF16)


def _init_history(xp_ref, state_ref, hist, first_tile):
    h_rows = _round_up(hist, SUBLANES)
    if state_ref is None:
        @pl.when(first_tile)
        def _():
            xp_ref[0:h_rows, :] = jnp.zeros((h_rows, xp_ref.shape[1]), _F32)
    else:
        xp_ref[h_rows - hist:h_rows, :] = state_ref[0]


def _emit_history(xp_ref, new_ref, hist, tm, carry):
    h_rows = _round_up(hist, SUBLANES)
    tail = xp_ref[h_rows + tm - hist:h_rows + tm, :]
    new_ref[0] = tail
    if carry:
        xp_ref[h_rows - hist:h_rows, :] = tail


def _conv_taps(xp_ref, w_ref, r0, cols, ktaps, stride, rb):
    hist = (ktaps - 1) * stride
    base = _round_up(hist, SUBLANES) - hist + r0
    acc = None
    for k in range(ktaps):
        term = w_ref[k:k + 1, cols] * xp_ref[base + k * stride:base + k * stride + rb, cols]
        acc = term if acc is None else acc + term
    return acc


def _conv_long_stride1(xp_ref, w_ref, r0, cols, ktaps, rb):
    hist = ktaps - 1
    base = _round_up(hist, SUBLANES) - hist
    n = rb + SUBLANES
    p = None
    for r in range(SUBLANES - 1, -1, -1):
        z = None
        for k in range(ktaps):
            if (base + k) % SUBLANES != r:
                continue
            off = r0 + (base + k) - r
            term = w_ref[k:k + 1, cols] * xp_ref[off:off + n, cols]
            z = term if z is None else z + term
        if p is not None:
            p = pltpu.roll(p, n - 1, 0)
            p = p if z is None else p + z
        else:
            p = z
    return p[0:rb]


def _even_kernel(*refs, tm, stride, has_state):
    if has_state:
        (x_ref, g_ref, win_ref, wout_ref, lng_ref, cw_ref, gw_ref, gb_ref, state_ref,
         out_ref, newb_ref, v_ref, xn_ref, xp_ref, ycat_ref, vb_ref) = refs
    else:
        (x_ref, g_ref, win_ref, wout_ref, lng_ref, cw_ref, gw_ref, gb_ref,
         out_ref, newb_ref, xn_ref, xp_ref, ycat_ref, vb_ref) = refs
        state_ref = None
        v_ref = None
    rb = ROW_BLOCK
    hist = (B_CONV - 1) * stride
    h_rows = _round_up(hist, SUBLANES)
    _init_history(xp_ref, state_ref, hist, pl.program_id(1) == 0)
    _norm_to_bf16(x_ref, g_ref, xn_ref, tm)
    xn = xn_ref[...]

    zbg = _dot(xn, win_ref[:, 2 * A_WIDTH:2 * A_WIDTH + B_WIDTH])
    zcg = _dot(xn, win_ref[:, 2 * A_WIDTH + B_WIDTH:2 * A_WIDTH + 2 * B_WIDTH])
    zh = _dot(xn, win_ref[:, 2 * A_WIDTH + 2 * B_WIDTH:2 * A_WIDTH + 3 * B_WIDTH])
    allc = slice(0, B_WIDTH)
    for r0 in range(0, tm, rb):
        rows = slice(r0, r0 + rb)
        xp_ref[h_rows + r0:h_rows + r0 + rb, :] = zcg[rows] * zh[rows]
        conv = _conv_taps(xp_ref, cw_ref, r0, allc, B_CONV, stride, rb)
        ycat_ref[rows, A_WIDTH:A_WIDTH + B_WIDTH] = (zbg[rows] * conv).astype(_BF16)
    _emit_history(xp_ref, newb_ref, hist, tm, carry=not has_state)

    zu = _dot(xn, win_ref[:, 0:A_WIDTH])
    zv = _dot(xn, win_ref[:, A_WIDTH:2 * A_WIDTH])
    lng = lng_ref[...]
    for r0 in range(0, tm, NORM_BLOCK):
        rows = slice(r0, r0 + NORM_BLOCK)
        v = _ln_rows(jax.nn.gelu(zv[rows]), lng)
        if v_ref is not None:
            v_ref[rows, :] = v
        else:
            vb_ref[rows, :] = v.astype(_BF16)
    if stride == 1:
        row_i = lax.broadcasted_iota(jnp.int32, (CHUNK, CHUNK), 0)
        col_i = lax.broadcasted_iota(jnp.int32, (CHUNK, CHUNK), 1)
        wts = [jnp.where(col_i <= row_i, gw_ref[h], 0.0).astype(_BF16) for h in range(A_HEADS)]
        for c0 in range(0, tm, CHUNK):
            rows = slice(c0, c0 + CHUNK)
            for h in range(A_HEADS):
                cols = slice(h * CHUNK, (h + 1) * CHUNK)
                mixed = _dot(wts[h], vb_ref[rows, cols]) + gb_ref[:, h:h + 1]
                ycat_ref[rows, cols] = (jax.nn.gelu(zu[rows, cols]) * mixed).astype(_BF16)
    else:
        steps = tm // stride
        for t in range(steps):
            for n0 in range(0, stride, rb):
                rows = slice(t * stride + n0, t * stride + n0 + rb)
                for h in range(A_HEADS):
                    cols = slice(h * CHUNK, (h + 1) * CHUNK)
                    mixed = None
                    for s in range(t + 1):
                        w = gw_ref[(h * steps + t) * steps + s]
                        term = w * v_ref[s * stride + n0:s * stride + n0 + rb, cols]
                        mixed = term if mixed is None else mixed + term
                    mixed = mixed + gb_ref[h * steps + t]
                    ycat_ref[rows, cols] = (jax.nn.gelu(zu[rows, cols]) * mixed).astype(_BF16)

    y = _dot(ycat_ref[...], wout_ref[...])
    out_ref[...] = x_ref[...] + y


def _pool_block(xp_ref, r0, gi, win, stride, rb, pos0, row_pos_static):
    cols = slice(gi * D_GROUP_DIM, (gi + 1) * D_GROUP_DIM)
    hist = POOL_BUF * stride
    h_rows = _round_up(hist, SUBLANES)
    if stride % SUBLANES == 0:
        s = None
        for j in range(win):
            off = h_rows + r0 - j * stride
            term = xp_ref[off:off + rb, cols]
            s = term if s is None else s + term
    else:
        halo = 2 * SUBLANES
        ext = xp_ref[h_rows + r0 - halo:h_rows + r0 + rb, cols]
        shift = 1
        while shift < win:
            ext = ext + pltpu.roll(ext, shift, 0)
            shift *= 2
        s = ext[halo:]
    if row_pos_static is not None:
        return s / float(min(row_pos_static + 1, win))
    pos = pos0 + r0 + lax.broadcasted_iota(jnp.int32, (rb, D_GROUP_DIM), 0)
    cnt = jnp.minimum(pos + 1, win).astype(_F32)
    return s / cnt


def _odd_kernel(*refs, tm, stride, has_state, start_pos):
    if has_state:
        (x_ref, g_ref, win_ref, wout_ref, ccw_ref, ccb_ref, clg_ref, clb_ref, dproj_ref, dscale_ref,
         statec_ref, stated_ref, out_ref, newc_ref, newd_ref, xn_ref, xpc_ref, xpd_ref, ycat_ref) = refs
    else:
        (x_ref, g_ref, win_ref, wout_ref, ccw_ref, ccb_ref, clg_ref, clb_ref, dproj_ref, dscale_ref,
         out_ref, newc_ref, newd_ref, xn_ref, xpc_ref, xpd_ref, ycat_ref) = refs
        statec_ref = None
        stated_ref = None
    rb = ROW_BLOCK
    hist_c = (C_CONV - 1) * stride
    hc_rows = _round_up(hist_c, SUBLANES)
    hist_d = POOL_BUF * stride
    hd_rows = _round_up(hist_d, SUBLANES)
    first = pl.program_id(1) == 0
    _init_history(xpc_ref, statec_ref, hist_c, first)
    _init_history(xpd_ref, stated_ref, hist_d, first)
    xpc_ref[hc_rows + tm:hc_rows + tm + SUBLANES, :] = jnp.zeros((SUBLANES, C_WIDTH), _F32)
    _norm_to_bf16(x_ref, g_ref, xn_ref, tm)
    xn = xn_ref[...]

    zca = _dot(xn, win_ref[:, 0:C_WIDTH])
    zcb = _dot(xn, win_ref[:, C_WIDTH:2 * C_WIDTH])
    zp = _dot(xn, win_ref[:, 2 * C_WIDTH:2 * C_WIDTH + D_WIDTH])
    for r0 in range(0, tm, rb):
        rows = slice(r0, r0 + rb)
        xpc_ref[hc_rows + r0:hc_rows + r0 + rb, :] = zca[rows] * jax.nn.sigmoid(zcb[rows])
        xpd_ref[hd_rows + r0:hd_rows + r0 + rb, :] = zp[rows]

    ccb = ccb_ref[...]
    clg = clg_ref[...]
    clb = clb_ref[...]
    for r0 in range(0, tm, rb):
        rows = slice(r0, r0 + rb)
        parts = []
        for c0 in range(0, C_WIDTH, LANES):
            cols = slice(c0, c0 + LANES)
            if stride % SUBLANES == 0:
                parts.append(_conv_taps(xpc_ref, ccw_ref, r0, cols, C_CONV, stride, rb))
            else:
                parts.append(_conv_long_stride1(xpc_ref, ccw_ref, r0, cols, C_CONV, rb))
        cc = jnp.concatenate(parts, axis=-1) + ccb
        ycat_ref[rows, 0:C_WIDTH] = jax.nn.silu(_ln_rows(cc, clg, clb)).astype(_BF16)
    _emit_history(xpc_ref, newc_ref, hist_c, tm, carry=not has_state)

    pos0 = start_pos + pl.program_id(1) * tm
    for r0 in range(0, tm, rb):
        rows = slice(r0, r0 + rb)
        for gi, win in enumerate(POOL_WINDOWS):
            cols = slice(gi * D_GROUP_DIM, (gi + 1) * D_GROUP_DIM)
            if stride == 1:
                static_pos = None if r0 < POOL_BUF else POOL_BUF
            else:
                static_pos = start_pos + r0 // stride
            pooled = _pool_block(xpd_ref, r0, gi, win, stride, rb, pos0, static_pos)
            diff = (pooled - xpd_ref[hd_rows + r0:hd_rows + r0 + rb, cols]).astype(_BF16)
            yd = _dot(diff, dproj_ref[gi]) * dscale_ref[:, cols]
            ycat_ref[rows, C_WIDTH + gi * D_GROUP_DIM:C_WIDTH + (gi + 1) * D_GROUP_DIM] = yd.astype(_BF16)
    _emit_history(xpd_ref, newd_ref, hist_d, tm, carry=not has_state)

    y = _dot(ycat_ref[...], wout_ref[...])
    out_ref[...] = x_ref[...] + y


def _ffn_kernel(*refs, tm, stride, has_state, final_norm):
    refs = list(refs)
    x_ref, g_ref, win_ref, wout_ref, cw_ref = refs[:5]
    del refs[:5]
    state_ref = refs.pop(0) if has_state else None
    gfin_ref = refs.pop(0) if final_norm else None
    out_ref, newf_ref, xn_ref, gp_ref, h_ref = refs
    rb = ROW_BLOCK
    hist = (FFN_CONV - 1) * stride
    h_rows = _round_up(hist, SUBLANES)
    _init_history(gp_ref, state_ref, hist, pl.program_id(1) == 0)
    _norm_to_bf16(x_ref, g_ref, xn_ref, tm)
    xn = xn_ref[...]

    for c0 in range(0, D_FF, FF_CHUNK):
        cols = slice(c0, c0 + FF_CHUNK)
        zg = _dot(xn, win_ref[:, c0:c0 + FF_CHUNK])
        zu = _dot(xn, win_ref[:, D_FF + c0:D_FF + c0 + FF_CHUNK])
        for r0 in range(0, tm, rb):
            rows = slice(r0, r0 + rb)
            gp_ref[h_rows + r0:h_rows + r0 + rb, cols] = zg[rows]
            conv = _conv_taps(gp_ref, cw_ref, r0, cols, FFN_CONV, stride, rb)
            h_ref[rows, cols] = (jax.nn.gelu(conv) * zu[rows]).astype(_BF16)
    _emit_history(gp_ref, newf_ref, hist, tm, carry=not has_state)

    y = x_ref[...] + _dot(h_ref[...], wout_ref[...])
    if final_norm:
        gfin = gfin_ref[...]
        out_ref[...] = y
        for r0 in range(0, tm, NORM_BLOCK):
            rows = slice(r0, r0 + NORM_BLOCK)
            out_ref[rows, :] = _rms_rows(out_ref[rows, :], gfin)
    else:
        out_ref[...] = y


def _const_spec(shape):
    nd = len(shape)
    return pl.BlockSpec(shape, lambda b, t: (0,) * nd, pipeline_mode=pl.Buffered(1))


def _smem_spec():
    return pl.BlockSpec(memory_space=pltpu.SMEM)


def _tile_spec(nt, width):
    return pl.BlockSpec((TILE_M, width), lambda b, t: (b * nt + t, 0))


def _state_spec(rows, width):
    return pl.BlockSpec((1, rows, width), lambda b, t: (b, 0, 0))


def _params():
    return pltpu.CompilerParams(dimension_semantics=("arbitrary", "arbitrary"), vmem_limit_bytes=VMEM_LIMIT)


def _row(v):
    return v.reshape(1, -1)


def _even_call(x2, nb, nt, stride, state, g, win, wout, lng, cw, gate_w, gate_b, name):
    has_state = state is not None
    hist = (B_CONV - 1) * stride
    h_rows = _round_up(hist, SUBLANES)
    in_specs = [_tile_spec(nt, D_MODEL), _const_spec((1, D_MODEL)), _const_spec(win.shape), _const_spec(wout.shape),
                _const_spec((1, A_WIDTH)), _const_spec(cw.shape)]
    args = [x2, _row(g), win, wout, _row(lng), cw, gate_w, gate_b]
    if has_state:
        in_specs += [_smem_spec(), _smem_spec(), _state_spec(hist, B_WIDTH)]
        args.append(state)
    else:
        in_specs += [_const_spec(gate_w.shape), _const_spec(gate_b.shape)]
    out_shape = [jax.ShapeDtypeStruct(x2.shape, _F32), jax.ShapeDtypeStruct((nb, hist, B_WIDTH), _F32)]
    out_specs = [_tile_spec(nt, D_MODEL), _state_spec(hist, B_WIDTH)]
    if has_state:
        out_shape.append(jax.ShapeDtypeStruct((x2.shape[0], A_WIDTH), _F32))
        out_specs.append(_tile_spec(nt, A_WIDTH))
    scratch = [pltpu.VMEM((TILE_M, D_MODEL), _BF16), pltpu.VMEM((h_rows + TILE_M, B_WIDTH), _F32),
               pltpu.VMEM((TILE_M, A_WIDTH + B_WIDTH), _BF16), pltpu.VMEM((TILE_M, A_WIDTH), _BF16)]
    return pl.pallas_call(
        functools.partial(_even_kernel, tm=TILE_M, stride=stride, has_state=has_state),
        grid=(nb, nt), in_specs=in_specs, out_specs=out_specs, out_shape=out_shape,
        scratch_shapes=scratch, compiler_params=_params(), name=name)(*args)


def _odd_call(x2, nb, nt, stride, state_c, state_d, start_pos, g, win, wout, ccw, ccb, clg, clb, dproj, dscale, name):
    has_state = state_c is not None
    hist_c = (C_CONV - 1) * stride
    hist_d = POOL_BUF * stride
    hc_rows = _round_up(hist_c, SUBLANES)
    hd_rows = _round_up(hist_d, SUBLANES)
    in_specs = [_tile_spec(nt, D_MODEL), _const_spec((1, D_MODEL)), _const_spec(win.shape), _const_spec(wout.shape),
                _const_spec(ccw.shape), _const_spec((1, C_WIDTH)), _const_spec((1, C_WIDTH)), _const_spec((1, C_WIDTH)),
                _const_spec(dproj.shape), _const_spec((1, D_WIDTH))]
    args = [x2, _row(g), win, wout, ccw, _row(ccb), _row(clg), _row(clb), dproj, _row(dscale)]
    if has_state:
        in_specs += [_state_spec(hist_c, C_WIDTH), _state_spec(hist_d, D_WIDTH)]
        args += [state_c, state_d]
    out_shape = [jax.ShapeDtypeStruct(x2.shape, _F32), jax.ShapeDtypeStruct((nb, hist_c, C_WIDTH), _F32),
                 jax.ShapeDtypeStruct((nb, hist_d, D_WIDTH), _F32)]
    out_specs = [_tile_spec(nt, D_MODEL), _state_spec(hist_c, C_WIDTH), _state_spec(hist_d, D_WIDTH)]
    scratch = [pltpu.VMEM((TILE_M, D_MODEL), _BF16), pltpu.VMEM((hc_rows + TILE_M + SUBLANES, C_WIDTH), _F32),
               pltpu.VMEM((hd_rows + TILE_M, D_WIDTH), _F32), pltpu.VMEM((TILE_M, C_WIDTH + D_WIDTH), _BF16)]
    return pl.pallas_call(
        functools.partial(_odd_kernel, tm=TILE_M, stride=stride, has_state=has_state, start_pos=start_pos),
        grid=(nb, nt), in_specs=in_specs, out_specs=out_specs, out_shape=out_shape,
        scratch_shapes=scratch, compiler_params=_params(), name=name)(*args)


def _ffn_call(x2, nb, nt, stride, state, g, win, wout, cw, gfinal, name):
    has_state = state is not None
    final_norm = gfinal is not None
    hist = (FFN_CONV - 1) * stride
    h_rows = _round_up(hist, SUBLANES)
    in_specs = [_tile_spec(nt, D_MODEL), _const_spec((1, D_MODEL)), _const_spec(win.shape), _const_spec(wout.shape),
                _const_spec(cw.shape)]
    args = [x2, _row(g), win, wout, cw]
    if has_state:
        in_specs.append(_state_spec(hist, D_FF))
        args.append(state)
    if final_norm:
        in_specs.append(_const_spec((1, D_MODEL)))
        args.append(_row(gfinal))
    out_shape = [jax.ShapeDtypeStruct(x2.shape, _F32), jax.ShapeDtypeStruct((nb, hist, D_FF), _F32)]
    out_specs = [_tile_spec(nt, D_MODEL), _state_spec(hist, D_FF)]
    scratch = [pltpu.VMEM((TILE_M, D_MODEL), _BF16), pltpu.VMEM((h_rows + TILE_M, D_FF), _F32),
               pltpu.VMEM((TILE_M, D_FF), _BF16)]
    return pl.pallas_call(
        functools.partial(_ffn_kernel, tm=TILE_M, stride=stride, has_state=has_state, final_norm=final_norm),
        grid=(nb, nt), in_specs=in_specs, out_specs=out_specs, out_shape=out_shape,
        scratch_shapes=scratch, compiler_params=_params(), name=name)(*args)


def _to_tiles(a, ns):
    n, l, c = a.shape
    return a.reshape(n // ns, ns, l, c).transpose(0, 2, 1, 3).reshape(n // ns, l * ns, c)


def _from_tiles(a, ns, l):
    nb, _, c = a.shape
    return a.reshape(nb, l, ns, c).transpose(0, 2, 1, 3).reshape(nb * ns, l, c)


def _trunk(x2, nb, nt, stride, states, start_pos, prm, tag):
    new_a, new_b, new_c, new_d, new_f = [], [], [], [], []
    depth = prm['w_ffn_in'].shape[0]
    for layer in range(depth):
        i = layer // 2
        if layer % 2 == 0:
            if states is None:
                gate_w, gate_b, st = prm['a_ws'][i], prm['a_bs'][i].T, None
            else:
                steps = TILE_M // stride
                gate_w = prm['a_ws'][i][:, :steps, :steps].reshape(-1)
                gate_b = prm['a_bs'][i][:, :steps].reshape(-1)
                st = states['b'][i]
            res = _even_call(x2, nb, nt, stride, st, prm['norm_mix_g'][layer], prm['w_in_even'][i],
                             prm['w_out_even'][i], prm['a_ln_g'][i], prm['b_conv_w'][i], gate_w, gate_b,
                             name=f"{tag}_mix{layer}")
            x2 = res[0]
            new_b.append(res[1])
            if states is not None:
                new_a.append(res[2])
        else:
            st_c = None if states is None else states['c'][i]
            st_d = None if states is None else states['d'][i]
            ccw = jnp.pad(prm['c_conv_w'][i], ((0, _round_up(C_CONV, SUBLANES) - C_CONV), (0, 0)))
            x2, nc, nd = _odd_call(x2, nb, nt, stride, st_c, st_d, start_pos, prm['norm_mix_g'][layer],
                                   prm['w_in_odd'][i], prm['w_out_odd'][i], ccw, prm['c_conv_b'][i],
                                   prm['c_ln_g'][i], prm['c_ln_b'][i], prm['d_proj'][i], prm['d_scale'][i],
                                   name=f"{tag}_mix{layer}")
            new_c.append(nc)
            new_d.append(nd)
        st_f = None if states is None else states['f'][layer]
        gfinal = prm['norm_final_g'] if layer == depth - 1 else None
        x2, nf = _ffn_call(x2, nb, nt, stride, st_f, prm['norm_ffn_g'][layer], prm['w_ffn_in'][layer],
                           prm['w_ffn_out'][layer], prm['ffn_conv_w'][layer], gfinal, name=f"{tag}_ffn{layer}")
        new_f.append(nf)
    return x2, new_a, new_b, new_c, new_d, new_f


def kernel(x_prompt, x_sample, state_b_conv, state_c_conv, state_d_pool, state_ffn_conv, norm_mix_g, norm_ffn_g, norm_final_g, w_in_even, w_out_even, a_ln_g, a_ws, a_bs, b_conv_w, w_in_odd, w_out_odd, c_conv_w, c_conv_b, c_ln_g, c_ln_b, d_proj, d_scale, w_ffn_in, ffn_conv_w, w_ffn_out):
    prm = dict(norm_mix_g=norm_mix_g, norm_ffn_g=norm_ffn_g, norm_final_g=norm_final_g,
               w_in_even=w_in_even.astype(_BF16), w_out_even=w_out_even.astype(_BF16), a_ln_g=a_ln_g,
               a_ws=a_ws, a_bs=a_bs, b_conv_w=b_conv_w, w_in_odd=w_in_odd.astype(_BF16),
               w_out_odd=w_out_odd.astype(_BF16), c_conv_w=c_conv_w, c_conv_b=c_conv_b, c_ln_g=c_ln_g,
               c_ln_b=c_ln_b, d_proj=d_proj.astype(_BF16), d_scale=d_scale,
               w_ffn_in=w_ffn_in.astype(_BF16), ffn_conv_w=ffn_conv_w, w_ffn_out=w_ffn_out.astype(_BF16))
    batch, seq, d = x_prompt.shape
    dec_batch, dec_seq, _ = x_sample.shape
    ns = SAMPLE_SEQS

    yp, _, b_p, c_p, d_p, f_p = _trunk(x_prompt.reshape(batch * seq, d), batch, seq // TILE_M, 1, None, 0, prm, "p")

    def tiled(state):
        return [_to_tiles(s, ns) for s in state]

    states = dict(b=tiled(state_b_conv), c=tiled(state_c_conv), d=tiled(state_d_pool), f=tiled(state_ffn_conv))
    xs = _to_tiles(x_sample, ns).reshape(dec_batch * dec_seq, d)
    ys, a_s, b_s, c_s, d_s, f_s = _trunk(xs, dec_batch // ns, 1, ns, states, PAST_LEN, prm, "s")

    def untile(parts, l):
        return jnp.stack([_from_tiles(p, ns, l) for p in parts])

    y_prompt = yp.reshape(batch, seq, d)
    y_sample = _from_tiles(ys.reshape(dec_batch // ns, dec_seq * ns, d), ns, dec_seq)
    new_a = untile([a.reshape(dec_batch // ns, dec_seq * ns, A_WIDTH) for a in a_s], dec_seq)
    return (y_prompt, y_sample, new_a, jnp.stack(b_p), untile(b_s, B_CONV - 1), jnp.stack(c_p),
            untile(c_s, C_CONV - 1), jnp.stack(d_p), untile(d_s, POOL_BUF), jnp.stack(f_p),
            untile(f_s, FFN_CONV - 1))
```

```python
import functools

import jax
import jax.numpy as jnp
from jax import lax
from jax.experimental import pallas as pl
from jax.experimental.pallas import tpu as pltpu

D_MODEL = 1024
CHUNK = 128
A_HEADS = 4
A_WIDTH = 512
B_WIDTH = 512
B_CONV = 3
C_WIDTH = 512
C_CONV = 31
D_WIDTH = 512
POOL_WINDOWS = (2, 4, 8, 16)
D_GROUP_DIM = 128
POOL_BUF = 15
D_FF = 2816
FFN_CONV = 3
PAST_LEN = 16384
EPS = 1e-6

SUBLANES = 8
LANES = 128
TILE_M = 512
ROW_BLOCK = 64
NORM_BLOCK = 32
FF_CHUNK = 256
SAMPLE_SEQS = TILE_M // 8
VMEM_LIMIT = 56 * 1024 * 1024

_BF16 = jnp.bfloat16
_F32 = jnp.float32


def _round_up(n, m):
    return (n + m - 1) // m * m


def _dot(a, b):
    return jnp.dot(a, b, preferred_element_type=_F32)


def _rms_rows(x, g):
    y = x * lax.rsqrt(jnp.mean(x * x, axis=-1, keepdims=True) + EPS)
    return y * g


def _ln_rows(x, g, b=None):
    mu = jnp.mean(x, axis=-1, keepdims=True)
    xc = x - mu
    y = xc * lax.rsqrt(jnp.mean(xc * xc, axis=-1, keepdims=True) + EPS) * g
    if b is not None:
        y = y + b
    return y


def _norm_to_bf16(x_ref, g_ref, xn_ref, tm):
    g = g_ref[...]
    for r0 in range(0, tm, NORM_BLOCK):
        rows = slice(r0, r0 + NORM_BLOCK)
        xn_ref[rows, :] = _rms_rows(x_ref[rows, :], g).astype(_BF16)


def _init_history(xp_ref, state_ref, hist, first_tile):
    h_rows = _round_up(hist, SUBLANES)
    if state_ref is None:
        @pl.when(first_tile)
        def _():
            xp_ref[0:h_rows, :] = jnp.zeros((h_rows, xp_ref.shape[1]), _F32)
    else:
        xp_ref[h_rows - hist:h_rows, :] = state_ref[0]


def _emit_history(xp_ref, new_ref, hist, tm, carry):
    h_rows = _round_up(hist, SUBLANES)
    tail = xp_ref[h_rows + tm - hist:h_rows + tm, :]
    new_ref[0] = tail
    if carry:
        xp_ref[h_rows - hist:h_rows, :] = tail


def _conv_taps(xp_ref, w_ref, r0, cols, ktaps, stride, rb):
    hist = (ktaps - 1) * stride
    base = _round_up(hist, SUBLANES) - hist + r0
    acc = None
    for k in range(ktaps):
        term = w_ref[k:k + 1, cols] * xp_ref[base + k * stride:base + k * stride + rb, cols]
        acc = term if acc is None else acc + term
    return acc


def _conv_long_stride1(xp_ref, w_ref, r0, cols, ktaps, rb):
    hist = ktaps - 1
    base = _round_up(hist, SUBLANES) - hist
    n = rb + SUBLANES
    p = None
    for r in range(SUBLANES - 1, -1, -1):
        z = None
        for k in range(ktaps):
            if (base + k) % SUBLANES != r:
                continue
            off = r0 + (base + k) - r
            term = w_ref[k:k + 1, cols] * xp_ref[off:off + n, cols]
            z = term if z is None else z + term
        if p is not None:
            p = pltpu.roll(p, n - 1, 0)
            p = p if z is None else p + z
        else:
            p = z
    return p[0:rb]


def _even_kernel(*refs, tm, stride, has_state, gate_layer):
    if has_state:
        (x_ref, g_ref, win_ref, wout_ref, lng_ref, cw_ref, gw_ref, gb_ref, state_ref,
         out_ref, newb_ref, v_ref, xn_ref, xp_ref, ycat_ref, vb_ref) = refs
    else:
        (x_ref, g_ref, win_ref, wout_ref, lng_ref, cw_ref, gw_ref, gb_ref,
         out_ref, newb_ref, xn_ref, xp_ref, ycat_ref, vb_ref) = refs
        state_ref = None
        v_ref = None
    rb = ROW_BLOCK
    hist = (B_CONV - 1) * stride
    h_rows = _round_up(hist, SUBLANES)
    _init_history(xp_ref, state_ref, hist, pl.program_id(1) == 0)
    _norm_to_bf16(x_ref, g_ref, xn_ref, tm)
    xn = xn_ref[...]

    zbg = _dot(xn, win_ref[:, 2 * A_WIDTH:2 * A_WIDTH + B_WIDTH])
    zcg = _dot(xn, win_ref[:, 2 * A_WIDTH + B_WIDTH:2 * A_WIDTH + 2 * B_WIDTH])
    zh = _dot(xn, win_ref[:, 2 * A_WIDTH + 2 * B_WIDTH:2 * A_WIDTH + 3 * B_WIDTH])
    allc = slice(0, B_WIDTH)
    for r0 in range(0, tm, rb):
        rows = slice(r0, r0 + rb)
        xp_ref[h_rows + r0:h_rows + r0 + rb, :] = zcg[rows] * zh[rows]
        conv = _conv_taps(xp_ref, cw_ref, r0, allc, B_CONV, stride, rb)
        ycat_ref[rows, A_WIDTH:A_WIDTH + B_WIDTH] = (zbg[rows] * conv).astype(_BF16)
    _emit_history(xp_ref, newb_ref, hist, tm, carry=not has_state)

    zu = _dot(xn, win_ref[:, 0:A_WIDTH])
    zv = _dot(xn, win_ref[:, A_WIDTH:2 * A_WIDTH])
    lng = lng_ref[...]
    for r0 in range(0, tm, NORM_BLOCK):
        rows = slice(r0, r0 + NORM_BLOCK)
        v = _ln_rows(jax.nn.gelu(zv[rows]), lng)
        if v_ref is not None:
            v_ref[rows, :] = v
        else:
            vb_ref[rows, :] = v.astype(_BF16)
    if stride == 1:
        row_i = lax.broadcasted_iota(jnp.int32, (CHUNK, CHUNK), 0)
        col_i = lax.broadcasted_iota(jnp.int32, (CHUNK, CHUNK), 1)
        wts = [jnp.where(col_i <= row_i, gw_ref[h], 0.0).astype(_BF16) for h in range(A_HEADS)]
        for c0 in range(0, tm, CHUNK):
            rows = slice(c0, c0 + CHUNK)
            for h in range(A_HEADS):
                cols = slice(h * CHUNK, (h + 1) * CHUNK)
                mixed = _dot(wts[h], vb_ref[rows, cols]) + gb_ref[:, h:h + 1]
                ycat_ref[rows, cols] = (jax.nn.gelu(zu[rows, cols]) * mixed).astype(_BF16)
    else:
        steps = tm // stride
        gw_off = gate_layer * A_HEADS * steps * steps
        gb_off = gate_layer * A_HEADS * steps
        for t in range(steps):
            for n0 in range(0, stride, rb):
                rows = slice(t * stride + n0, t * stride + n0 + rb)
                for h in range(A_HEADS):
                    cols = slice(h * CHUNK, (h + 1) * CHUNK)
                    mixed = None
                    for s in range(t + 1):
                        w = gw_ref[gw_off + (h * steps + t) * steps + s]
                        term = w * v_ref[s * stride + n0:s * stride + n0 + rb, cols]
                        mixed = term if mixed is None else mixed + term
                    mixed = mixed + gb_ref[gb_off + h * steps + t]
                    ycat_ref[rows, cols] = (jax.nn.gelu(zu[rows, cols]) * mixed).astype(_BF16)

    y = _dot(ycat_ref[...], wout_ref[...])
    out_ref[...] = x_ref[...] + y


def _pool_block(xp_ref, r0, gi, win, stride, rb, pos0, row_pos_static):
    cols = slice(gi * D_GROUP_DIM, (gi + 1) * D_GROUP_DIM)
    hist = POOL_BUF * stride
    h_rows = _round_up(hist, SUBLANES)
    if stride % SUBLANES == 0:
        s = None
        for j in range(win):
            off = h_rows + r0 - j * stride
            term = xp_ref[off:off + rb, cols]
            s = term if s is None else s + term
    else:
        halo = 2 * SUBLANES
        ext = xp_ref[h_rows + r0 - halo:h_rows + r0 + rb, cols]
        shift = 1
        while shift < win:
            ext = ext + pltpu.roll(ext, shift, 0)
            shift *= 2
        s = ext[halo:]
    if row_pos_static is not None:
        return s / float(min(row_pos_static + 1, win))
    pos = pos0 + r0 + lax.broadcasted_iota(jnp.int32, (rb, D_GROUP_DIM), 0)
    cnt = jnp.minimum(pos + 1, win).astype(_F32)
    return s / cnt


def _odd_kernel(*refs, tm, stride, has_state, start_pos):
    if has_state:
        (x_ref, g_ref, win_ref, wout_ref, ccw_ref, ccb_ref, clg_ref, clb_ref, dproj_ref, dscale_ref,
         statec_ref, stated_ref, out_ref, newc_ref, newd_ref, xn_ref, xpc_ref, xpd_ref, ycat_ref) = refs
    else:
        (x_ref, g_ref, win_ref, wout_ref, ccw_ref, ccb_ref, clg_ref, clb_ref, dproj_ref, dscale_ref,
         out_ref, newc_ref, newd_ref, xn_ref, xpc_ref, xpd_ref, ycat_ref) = refs
        statec_ref = None
        stated_ref = None
    rb = ROW_BLOCK
    hist_c = (C_CONV - 1) * stride
    hc_rows = _round_up(hist_c, SUBLANES)
    hist_d = POOL_BUF * stride
    hd_rows = _round_up(hist_d, SUBLANES)
    first = pl.program_id(1) == 0
    _init_history(xpc_ref, statec_ref, hist_c, first)
    _init_history(xpd_ref, stated_ref, hist_d, first)
    xpc_ref[hc_rows + tm:hc_rows + tm + SUBLANES, :] = jnp.zeros((SUBLANES, C_WIDTH), _F32)
    _norm_to_bf16(x_ref, g_ref, xn_ref, tm)
    xn = xn_ref[...]

    zca = _dot(xn, win_ref[:, 0:C_WIDTH])
    zcb = _dot(xn, win_ref[:, C_WIDTH:2 * C_WIDTH])
    zp = _dot(xn, win_ref[:, 2 * C_WIDTH:2 * C_WIDTH + D_WIDTH])
    for r0 in range(0, tm, rb):
        rows = slice(r0, r0 + rb)
        xpc_ref[hc_rows + r0:hc_rows + r0 + rb, :] = zca[rows] * jax.nn.sigmoid(zcb[rows])
        xpd_ref[hd_rows + r0:hd_rows + r0 + rb, :] = zp[rows]

    ccb = ccb_ref[...]
    clg = clg_ref[...]
    clb = clb_ref[...]
    for r0 in range(0, tm, rb):
        rows = slice(r0, r0 + rb)
        parts = []
        for c0 in range(0, C_WIDTH, LANES):
            cols = slice(c0, c0 + LANES)
            if stride % SUBLANES == 0:
                parts.append(_conv_taps(xpc_ref, ccw_ref, r0, cols, C_CONV, stride, rb))
            else:
                parts.append(_conv_long_stride1(xpc_ref, ccw_ref, r0, cols, C_CONV, rb))
        cc = jnp.concatenate(parts, axis=-1) + ccb
        ycat_ref[rows, 0:C_WIDTH] = jax.nn.silu(_ln_rows(cc, clg, clb)).astype(_BF16)
    _emit_history(xpc_ref, newc_ref, hist_c, tm, carry=not has_state)

    pos0 = start_pos + pl.program_id(1) * tm
    for r0 in range(0, tm, rb):
        rows = slice(r0, r0 + rb)
        for gi, win in enumerate(POOL_WINDOWS):
            cols = slice(gi * D_GROUP_DIM, (gi + 1) * D_GROUP_DIM)
            if stride == 1:
                static_pos = None if r0 < POOL_BUF else POOL_BUF
            else:
                static_pos = start_pos + r0 // stride
            pooled = _pool_block(xpd_ref, r0, gi, win, stride, rb, pos0, static_pos)
            diff = (pooled - xpd_ref[hd_rows + r0:hd_rows + r0 + rb, cols]).astype(_BF16)
            yd = _dot(diff, dproj_ref[gi]) * dscale_ref[:, cols]
            ycat_ref[rows, C_WIDTH + gi * D_GROUP_DIM:C_WIDTH + (gi + 1) * D_GROUP_DIM] = yd.astype(_BF16)
    _emit_history(xpd_ref, newd_ref, hist_d, tm, carry=not has_state)

    y = _dot(ycat_ref[...], wout_ref[...])
    out_ref[...] = x_ref[...] + y


def _ffn_kernel(*refs, tm, stride, has_state, final_norm):
    refs = list(refs)
    x_ref, g_ref, win_ref, wout_ref, cw_ref = refs[:5]
    del refs[:5]
    state_ref = refs.pop(0) if has_state else None
    gfin_ref = refs.pop(0) if final_norm else None
    out_ref, newf_ref, xn_ref, gp_ref, h_ref = refs
    rb = ROW_BLOCK
    hist = (FFN_CONV - 1) * stride
    h_rows = _round_up(hist, SUBLANES)
    _init_history(gp_ref, state_ref, hist, pl.program_id(1) == 0)
    _norm_to_bf16(x_ref, g_ref, xn_ref, tm)
    xn = xn_ref[...]

    for c0 in range(0, D_FF, FF_CHUNK):
        cols = slice(c0, c0 + FF_CHUNK)
        zg = _dot(xn, win_ref[:, c0:c0 + FF_CHUNK])
        zu = _dot(xn, win_ref[:, D_FF + c0:D_FF + c0 + FF_CHUNK])
        for r0 in range(0, tm, rb):
            rows = slice(r0, r0 + rb)
            gp_ref[h_rows + r0:h_rows + r0 + rb, cols] = zg[rows]
            conv = _conv_taps(gp_ref, cw_ref, r0, cols, FFN_CONV, stride, rb)
            h_ref[rows, cols] = (jax.nn.gelu(conv) * zu[rows]).astype(_BF16)
    _emit_history(gp_ref, newf_ref, hist, tm, carry=not has_state)

    y = x_ref[...] + _dot(h_ref[...], wout_ref[...])
    if final_norm:
        gfin = gfin_ref[...]
        out_ref[...] = y
        for r0 in range(0, tm, NORM_BLOCK):
            rows = slice(r0, r0 + NORM_BLOCK)
            out_ref[rows, :] = _rms_rows(out_ref[rows, :], gfin)
    else:
        out_ref[...] = y


def _const_spec(shape):
    nd = len(shape)
    return pl.BlockSpec(shape, lambda b, t: (0,) * nd, pipeline_mode=pl.Buffered(1))


def _layer_spec(arr, layer):
    nd = arr.ndim - 1
    return pl.BlockSpec((None,) + arr.shape[1:], lambda b, t: (layer,) + (0,) * nd, pipeline_mode=pl.Buffered(1))


def _smem_spec():
    return pl.BlockSpec(memory_space=pltpu.SMEM)


def _tile_spec(nt, width):
    return pl.BlockSpec((TILE_M, width), lambda b, t: (b * nt + t, 0))


def _state_spec(layer, rows, width):
    return pl.BlockSpec((None, 1, rows, width), lambda b, t: (layer, b, 0, 0))


def _new_state_spec(rows, width):
    return pl.BlockSpec((1, rows, width), lambda b, t: (b, 0, 0))


def _params():
    return pltpu.CompilerParams(dimension_semantics=("arbitrary", "arbitrary"), vmem_limit_bytes=VMEM_LIMIT)


def _even_call(x2, nb, nt, stride, state, layer, prm, name):
    has_state = state is not None
    i = layer // 2
    hist = (B_CONV - 1) * stride
    h_rows = _round_up(hist, SUBLANES)
    in_specs = [_tile_spec(nt, D_MODEL), _layer_spec(prm['norm_mix_g'], layer), _layer_spec(prm['w_in_even'], i),
                _layer_spec(prm['w_out_even'], i), _layer_spec(prm['a_ln_g'], i), _layer_spec(prm['b_conv_w'], i)]
    args = [x2, prm['norm_mix_g'], prm['w_in_even'], prm['w_out_even'], prm['a_ln_g'], prm['b_conv_w']]
    if has_state:
        in_specs += [_smem_spec(), _smem_spec(), _state_spec(i, hist, B_WIDTH)]
        args += [prm['a_ws_steps'], prm['a_bs_steps'], state]
    else:
        in_specs += [_layer_spec(prm['a_ws'], i), _layer_spec(prm['a_bs_t'], i)]
        args += [prm['a_ws'], prm['a_bs_t']]
    out_shape = [jax.ShapeDtypeStruct(x2.shape, _F32), jax.ShapeDtypeStruct((nb, hist, B_WIDTH), _F32)]
    out_specs = [_tile_spec(nt, D_MODEL), _new_state_spec(hist, B_WIDTH)]
    if has_state:
        out_shape.append(jax.ShapeDtypeStruct((x2.shape[0], A_WIDTH), _F32))
        out_specs.append(_tile_spec(nt, A_WIDTH))
    scratch = [pltpu.VMEM((TILE_M, D_MODEL), _BF16), pltpu.VMEM((h_rows + TILE_M, B_WIDTH), _F32),
               pltpu.VMEM((TILE_M, A_WIDTH + B_WIDTH), _BF16), pltpu.VMEM((TILE_M, A_WIDTH), _BF16)]
    return pl.pallas_call(
        functools.partial(_even_kernel, tm=TILE_M, stride=stride, has_state=has_state, gate_layer=i),
        grid=(nb, nt), in_specs=in_specs, out_specs=out_specs, out_shape=out_shape,
        scratch_shapes=scratch, compiler_params=_params(), name=name)(*args)


def _odd_call(x2, nb, nt, stride, state_c, state_d, start_pos, layer, prm, name):
    has_state = state_c is not None
    i = layer // 2
    hist_c = (C_CONV - 1) * stride
    hist_d = POOL_BUF * stride
    hc_rows = _round_up(hist_c, SUBLANES)
    hd_rows = _round_up(hist_d, SUBLANES)
    names = ['w_in_odd', 'w_out_odd', 'c_conv_w', 'c_conv_b', 'c_ln_g', 'c_ln_b', 'd_proj', 'd_scale']
    in_specs = [_tile_spec(nt, D_MODEL), _layer_spec(prm['norm_mix_g'], layer)] + [_layer_spec(prm[k], i) for k in names]
    args = [x2, prm['norm_mix_g']] + [prm[k] for k in names]
    if has_state:
        in_specs += [_state_spec(i, hist_c, C_WIDTH), _state_spec(i, hist_d, D_WIDTH)]
        args += [state_c, state_d]
    out_shape = [jax.ShapeDtypeStruct(x2.shape, _F32), jax.ShapeDtypeStruct((nb, hist_c, C_WIDTH), _F32),
                 jax.ShapeDtypeStruct((nb, hist_d, D_WIDTH), _F32)]
    out_specs = [_tile_spec(nt, D_MODEL), _new_state_spec(hist_c, C_WIDTH), _new_state_spec(hist_d, D_WIDTH)]
    scratch = [pltpu.VMEM((TILE_M, D_MODEL), _BF16), pltpu.VMEM((hc_rows + TILE_M + SUBLANES, C_WIDTH), _F32),
               pltpu.VMEM((hd_rows + TILE_M, D_WIDTH), _F32), pltpu.VMEM((TILE_M, C_WIDTH + D_WIDTH), _BF16)]
    return pl.pallas_call(
        functools.partial(_odd_kernel, tm=TILE_M, stride=stride, has_state=has_state, start_pos=start_pos),
        grid=(nb, nt), in_specs=in_specs, out_specs=out_specs, out_shape=out_shape,
        scratch_shapes=scratch, compiler_params=_params(), name=name)(*args)


def _ffn_call(x2, nb, nt, stride, state, layer, final_norm, prm, name):
    has_state = state is not None
    hist = (FFN_CONV - 1) * stride
    h_rows = _round_up(hist, SUBLANES)
    names = ['norm_ffn_g', 'w_ffn_in', 'w_ffn_out', 'ffn_conv_w']
    in_specs = [_tile_spec(nt, D_MODEL)] + [_layer_spec(prm[k], layer) for k in names]
    args = [x2] + [prm[k] for k in names]
    if has_state:
        in_specs.append(_state_spec(layer, hist, D_FF))
        args.append(state)
    if final_norm:
        in_specs.append(_const_spec((1, D_MODEL)))
        args.append(prm['norm_final_g'])
    out_shape = [jax.ShapeDtypeStruct(x2.shape, _F32), jax.ShapeDtypeStruct((nb, hist, D_FF), _F32)]
    out_specs = [_tile_spec(nt, D_MODEL), _new_state_spec(hist, D_FF)]
    scratch = [pltpu.VMEM((TILE_M, D_MODEL), _BF16), pltpu.VMEM((h_rows + TILE_M, D_FF), _F32),
               pltpu.VMEM((TILE_M, D_FF), _BF16)]
    return pl.pallas_call(
        functools.partial(_ffn_kernel, tm=TILE_M, stride=stride, has_state=has_state, final_norm=final_norm),
        grid=(nb, nt), in_specs=in_specs, out_specs=out_specs, out_shape=out_shape,
        scratch_shapes=scratch, compiler_params=_params(), name=name)(*args)


def _to_tiles(a, ns):
    *lead, n, l, c = a.shape
    k = len(lead)
    a = a.reshape(*lead, n // ns, ns, l, c)
    a = a.transpose(*range(k), k, k + 2, k + 1, k + 3)
    return a.reshape(*lead, n // ns, l * ns, c)


def _from_tiles(a, ns, l):
    *lead, nb, _, c = a.shape
    k = len(lead)
    a = a.reshape(*lead, nb, l, ns, c)
    a = a.transpose(*range(k), k, k + 2, k + 1, k + 3)
    return a.reshape(*lead, nb * ns, l, c)


def _trunk(x2, nb, nt, stride, states, start_pos, prm, tag):
    new_a, new_b, new_c, new_d, new_f = [], [], [], [], []
    depth = prm['w_ffn_in'].shape[0]
    for layer in range(depth):
        if layer % 2 == 0:
            st = None if states is None else states['b']
            res = _even_call(x2, nb, nt, stride, st, layer, prm, name=f"{tag}_mix{layer}")
            x2 = res[0]
            new_b.append(res[1])
            if states is not None:
                new_a.append(res[2])
        else:
            st_c = None if states is None else states['c']
            st_d = None if states is None else states['d']
            x2, nc, nd = _odd_call(x2, nb, nt, stride, st_c, st_d, start_pos, layer, prm, name=f"{tag}_mix{layer}")
            new_c.append(nc)
            new_d.append(nd)
        st_f = None if states is None else states['f']
        x2, nf = _ffn_call(x2, nb, nt, stride, st_f, layer, layer == depth - 1, prm, name=f"{tag}_ffn{layer}")
        new_f.append(nf)
    return x2, new_a, new_b, new_c, new_d, new_f


def kernel(x_prompt, x_sample, state_b_conv, state_c_conv, state_d_pool, state_ffn_conv, norm_mix_g, norm_ffn_g, norm_final_g, w_in_even, w_out_even, a_ln_g, a_ws, a_bs, b_conv_w, w_in_odd, w_out_odd, c_conv_w, c_conv_b, c_ln_g, c_ln_b, d_proj, d_scale, w_ffn_in, ffn_conv_w, w_ffn_out):
    batch, seq, d = x_prompt.shape
    dec_batch, dec_seq, _ = x_sample.shape
    ns = SAMPLE_SEQS

    def rows(p):
        return p.reshape(p.shape[0], 1, p.shape[1])

    prm = dict(norm_mix_g=rows(norm_mix_g), norm_ffn_g=rows(norm_ffn_g), norm_final_g=norm_final_g.reshape(1, -1),
               w_in_even=w_in_even.astype(_BF16), w_out_even=w_out_even.astype(_BF16), a_ln_g=rows(a_ln_g),
               a_ws=a_ws, a_bs_t=jnp.swapaxes(a_bs, 1, 2),
               a_ws_steps=a_ws[:, :, :dec_seq, :dec_seq].reshape(-1), a_bs_steps=a_bs[:, :, :dec_seq].reshape(-1),
               b_conv_w=b_conv_w, w_in_odd=w_in_odd.astype(_BF16), w_out_odd=w_out_odd.astype(_BF16),
               c_conv_w=jnp.pad(c_conv_w, ((0, 0), (0, _round_up(C_CONV, SUBLANES) - C_CONV), (0, 0))),
               c_conv_b=rows(c_conv_b), c_ln_g=rows(c_ln_g), c_ln_b=rows(c_ln_b), d_proj=d_proj.astype(_BF16),
               d_scale=rows(d_scale), w_ffn_in=w_ffn_in.astype(_BF16), ffn_conv_w=ffn_conv_w,
               w_ffn_out=w_ffn_out.astype(_BF16))

    yp, _, b_p, c_p, d_p, f_p = _trunk(x_prompt.reshape(batch * seq, d), batch, seq // TILE_M, 1, None, 0, prm, "p")

    states = dict(b=_to_tiles(state_b_conv, ns), c=_to_tiles(state_c_conv, ns), d=_to_tiles(state_d_pool, ns),
                  f=_to_tiles(state_ffn_conv, ns))
    xs = _to_tiles(x_sample, ns).reshape(dec_batch * dec_seq, d)
    ys, a_s, b_s, c_s, d_s, f_s = _trunk(xs, dec_batch // ns, 1, ns, states, PAST_LEN, prm, "s")

    def untile(parts, l):
        return _from_tiles(jnp.stack(parts), ns, l)

    y_prompt = yp.reshape(batch, seq, d)
    y_sample = _from_tiles(ys.reshape(dec_batch // ns, dec_seq * ns, d), ns, dec_seq)
    new_a = untile([a.reshape(dec_batch // ns, dec_seq * ns, A_WIDTH) for a in a_s], dec_seq)
    return (y_prompt, y_sample, new_a, jnp.stack(b_p), untile(b_s, B_CONV - 1), jnp.stack(c_p),
            untile(c_s, C_CONV - 1), jnp.stack(d_p), untile(d_s, POOL_BUF), jnp.stack(f_p),
            untile(f_s, FFN_CONV - 1))
```

```python
import functools

import jax
import jax.numpy as jnp
from jax import lax
from jax.experimental import pallas as pl
from jax.experimental.pallas import tpu as pltpu

D_MODEL = 1024
CHUNK = 128
A_HEADS = 4
A_WIDTH = 512
B_WIDTH = 512
B_CONV = 3
C_WIDTH = 512
C_CONV = 31
D_WIDTH = 512
POOL_WINDOWS = (2, 4, 8, 16)
D_GROUP_DIM = 128
POOL_BUF = 15
D_FF = 2816
FFN_CONV = 3
PAST_LEN = 16384
EPS = 1e-6

SUBLANES = 8
LANES = 128
PROMPT_TILE = 1024
SAMPLE_TILE = 512
ROW_BLOCK = 64
NORM_BLOCK = 32
FF_CHUNK = 256
VMEM_LIMIT = 56 * 1024 * 1024

_BF16 = jnp.bfloat16
_F32 = jnp.float32


def _round_up(n, m):
    return (n + m - 1) // m * m


def _dot(a, b):
    return jnp.dot(a, b, preferred_element_type=_F32)


def _rms_rows(x, g):
    y = x * lax.rsqrt(jnp.mean(x * x, axis=-1, keepdims=True) + EPS)
    return y * g


def _ln_rows(x, g, b=None):
    mu = jnp.mean(x, axis=-1, keepdims=True)
    xc = x - mu
    y = xc * lax.rsqrt(jnp.mean(xc * xc, axis=-1, keepdims=True) + EPS) * g
    if b is not None:
        y = y + b
    return y


def _norm_to_bf16(x_ref, g_ref, xn_ref, tm):
    g = g_ref[...]
    for r0 in range(0, tm, NORM_BLOCK):
        rows = slice(r0, r0 + NORM_BLOCK)
        xn_ref[rows, :] = _rms_rows(x_ref[rows, :], g).astype(_BF16)


def _init_history(xp_ref, state_ref, hist, first_tile):
    h_rows = _round_up(hist, SUBLANES)
    if state_ref is None:
        @pl.when(first_tile)
        def _():
            xp_ref[0:h_rows, :] = jnp.zeros((h_rows, xp_ref.shape[1]), _F32)
    else:
        xp_ref[h_rows - hist:h_rows, :] = state_ref[0]


def _emit_history(xp_ref, new_ref, hist, tm, carry):
    h_rows = _round_up(hist, SUBLANES)
    tail = xp_ref[h_rows + tm - hist:h_rows + tm, :]
    new_ref[0] = tail
    if carry:
        xp_ref[h_rows - hist:h_rows, :] = tail


def _conv_taps(xp_ref, w_ref, r0, cols, ktaps, stride, rb, xcols=None):
    xcols = cols if xcols is None else xcols
    hist = (ktaps - 1) * stride
    base = _round_up(hist, SUBLANES) - hist + r0
    acc = None
    for k in range(ktaps):
        term = w_ref[k:k + 1, cols] * xp_ref[base + k * stride:base + k * stride + rb, xcols]
        acc = term if acc is None else acc + term
    return acc


def _conv_long_stride1(xp_ref, w_ref, r0, cols, ktaps, rb):
    hist = ktaps - 1
    base = _round_up(hist, SUBLANES) - hist
    n = rb + SUBLANES
    p = None
    for r in range(SUBLANES - 1, -1, -1):
        z = None
        for k in range(ktaps):
            if (base + k) % SUBLANES != r:
                continue
            off = r0 + (base + k) - r
            term = w_ref[k:k + 1, cols] * xp_ref[off:off + n, cols]
            z = term if z is None else z + term
        if p is not None:
            p = pltpu.roll(p, n - 1, 0)
            p = p if z is None else p + z
        else:
            p = z
    return p[0:rb]


def _even_kernel(*refs, tm, stride, has_state, gate_layer):
    if has_state:
        (x_ref, g_ref, win_ref, wout_ref, lng_ref, cw_ref, gw_ref, gb_ref, state_ref,
         out_ref, newb_ref, v_ref, xn_ref, xp_ref, ycat_ref, vb_ref) = refs
    else:
        (x_ref, g_ref, win_ref, wout_ref, lng_ref, cw_ref, gw_ref, gb_ref,
         out_ref, newb_ref, xn_ref, xp_ref, ycat_ref, vb_ref) = refs
        state_ref = None
        v_ref = None
    rb = ROW_BLOCK
    hist = (B_CONV - 1) * stride
    h_rows = _round_up(hist, SUBLANES)
    _init_history(xp_ref, state_ref, hist, pl.program_id(1) == 0)
    _norm_to_bf16(x_ref, g_ref, xn_ref, tm)
    xn = xn_ref[...]

    zbg = _dot(xn, win_ref[:, 2 * A_WIDTH:2 * A_WIDTH + B_WIDTH])
    zcg = _dot(xn, win_ref[:, 2 * A_WIDTH + B_WIDTH:2 * A_WIDTH + 2 * B_WIDTH])
    zh = _dot(xn, win_ref[:, 2 * A_WIDTH + 2 * B_WIDTH:2 * A_WIDTH + 3 * B_WIDTH])
    allc = slice(0, B_WIDTH)
    for r0 in range(0, tm, rb):
        rows = slice(r0, r0 + rb)
        xp_ref[h_rows + r0:h_rows + r0 + rb, :] = zcg[rows] * zh[rows]
        conv = _conv_taps(xp_ref, cw_ref, r0, allc, B_CONV, stride, rb)
        ycat_ref[rows, A_WIDTH:A_WIDTH + B_WIDTH] = (zbg[rows] * conv).astype(_BF16)
    _emit_history(xp_ref, newb_ref, hist, tm, carry=not has_state)

    zu = _dot(xn, win_ref[:, 0:A_WIDTH])
    zv = _dot(xn, win_ref[:, A_WIDTH:2 * A_WIDTH])
    lng = lng_ref[...]
    for r0 in range(0, tm, NORM_BLOCK):
        rows = slice(r0, r0 + NORM_BLOCK)
        v = _ln_rows(jax.nn.gelu(zv[rows]), lng)
        if v_ref is not None:
            v_ref[rows, :] = v
        else:
            vb_ref[rows, :] = v.astype(_BF16)
    if stride == 1:
        row_i = lax.broadcasted_iota(jnp.int32, (CHUNK, CHUNK), 0)
        col_i = lax.broadcasted_iota(jnp.int32, (CHUNK, CHUNK), 1)
        wts = [jnp.where(col_i <= row_i, gw_ref[h], 0.0).astype(_BF16) for h in range(A_HEADS)]
        for c0 in range(0, tm, CHUNK):
            rows = slice(c0, c0 + CHUNK)
            for h in range(A_HEADS):
                cols = slice(h * CHUNK, (h + 1) * CHUNK)
                mixed = _dot(wts[h], vb_ref[rows, cols]) + gb_ref[:, h:h + 1]
                ycat_ref[rows, cols] = (jax.nn.gelu(zu[rows, cols]) * mixed).astype(_BF16)
    else:
        steps = tm // stride
        gw_off = gate_layer * A_HEADS * steps * steps
        gb_off = gate_layer * A_HEADS * steps
        for t in range(steps):
            for n0 in range(0, stride, rb):
                rows = slice(t * stride + n0, t * stride + n0 + rb)
                for h in range(A_HEADS):
                    cols = slice(h * CHUNK, (h + 1) * CHUNK)
                    mixed = None
                    for s in range(t + 1):
                        w = gw_ref[gw_off + (h * steps + t) * steps + s]
                        term = w * v_ref[s * stride + n0:s * stride + n0 + rb, cols]
                        mixed = term if mixed is None else mixed + term
                    mixed = mixed + gb_ref[gb_off + h * steps + t]
                    ycat_ref[rows, cols] = (jax.nn.gelu(zu[rows, cols]) * mixed).astype(_BF16)

    y = _dot(ycat_ref[...], wout_ref[...])
    out_ref[...] = x_ref[...] + y


def _pool_block(xp_ref, r0, gi, win, stride, rb, pos0, row_pos_static):
    cols = slice(gi * D_GROUP_DIM, (gi + 1) * D_GROUP_DIM)
    hist = POOL_BUF * stride
    h_rows = _round_up(hist, SUBLANES)
    if stride % SUBLANES == 0:
        s = None
        for j in range(win):
            off = h_rows + r0 - j * stride
            term = xp_ref[off:off + rb, cols]
            s = term if s is None else s + term
    else:
        halo = 2 * SUBLANES
        ext = xp_ref[h_rows + r0 - halo:h_rows + r0 + rb, cols]
        shift = 1
        while shift < win:
            ext = ext + pltpu.roll(ext, shift, 0)
            shift *= 2
        s = ext[halo:]
    if row_pos_static is not None:
        return s / float(min(row_pos_static + 1, win))
    pos = pos0 + r0 + lax.broadcasted_iota(jnp.int32, (rb, D_GROUP_DIM), 0)
    cnt = jnp.minimum(pos + 1, win).astype(_F32)
    return s / cnt


def _odd_kernel(*refs, tm, stride, has_state, start_pos):
    if has_state:
        (x_ref, g_ref, win_ref, wout_ref, ccw_ref, ccb_ref, clg_ref, clb_ref, dproj_ref, dscale_ref,
         statec_ref, stated_ref, out_ref, newc_ref, newd_ref, xn_ref, xpc_ref, xpd_ref, ycat_ref) = refs
    else:
        (x_ref, g_ref, win_ref, wout_ref, ccw_ref, ccb_ref, clg_ref, clb_ref, dproj_ref, dscale_ref,
         out_ref, newc_ref, newd_ref, xn_ref, xpc_ref, xpd_ref, ycat_ref) = refs
        statec_ref = None
        stated_ref = None
    rb = ROW_BLOCK
    hist_c = (C_CONV - 1) * stride
    hc_rows = _round_up(hist_c, SUBLANES)
    hist_d = POOL_BUF * stride
    hd_rows = _round_up(hist_d, SUBLANES)
    first = pl.program_id(1) == 0
    _init_history(xpc_ref, statec_ref, hist_c, first)
    _init_history(xpd_ref, stated_ref, hist_d, first)
    xpc_ref[hc_rows + tm:hc_rows + tm + SUBLANES, :] = jnp.zeros((SUBLANES, C_WIDTH), _F32)
    _norm_to_bf16(x_ref, g_ref, xn_ref, tm)
    xn = xn_ref[...]

    zca = _dot(xn, win_ref[:, 0:C_WIDTH])
    zcb = _dot(xn, win_ref[:, C_WIDTH:2 * C_WIDTH])
    zp = _dot(xn, win_ref[:, 2 * C_WIDTH:2 * C_WIDTH + D_WIDTH])
    for r0 in range(0, tm, rb):
        rows = slice(r0, r0 + rb)
        xpc_ref[hc_rows + r0:hc_rows + r0 + rb, :] = zca[rows] * jax.nn.sigmoid(zcb[rows])
        xpd_ref[hd_rows + r0:hd_rows + r0 + rb, :] = zp[rows]

    ccb = ccb_ref[...]
    clg = clg_ref[...]
    clb = clb_ref[...]
    for r0 in range(0, tm, rb):
        rows = slice(r0, r0 + rb)
        parts = []
        for c0 in range(0, C_WIDTH, LANES):
            cols = slice(c0, c0 + LANES)
            if stride % SUBLANES == 0:
                parts.append(_conv_taps(xpc_ref, ccw_ref, r0, cols, C_CONV, stride, rb))
            else:
                parts.append(_conv_long_stride1(xpc_ref, ccw_ref, r0, cols, C_CONV, rb))
        cc = jnp.concatenate(parts, axis=-1) + ccb
        ycat_ref[rows, 0:C_WIDTH] = jax.nn.silu(_ln_rows(cc, clg, clb)).astype(_BF16)
    _emit_history(xpc_ref, newc_ref, hist_c, tm, carry=not has_state)

    pos0 = start_pos + pl.program_id(1) * tm
    for r0 in range(0, tm, rb):
        rows = slice(r0, r0 + rb)
        for gi, win in enumerate(POOL_WINDOWS):
            cols = slice(gi * D_GROUP_DIM, (gi + 1) * D_GROUP_DIM)
            if stride == 1:
                static_pos = None if r0 < POOL_BUF else POOL_BUF
            else:
                static_pos = start_pos + r0 // stride
            pooled = _pool_block(xpd_ref, r0, gi, win, stride, rb, pos0, static_pos)
            diff = (pooled - xpd_ref[hd_rows + r0:hd_rows + r0 + rb, cols]).astype(_BF16)
            yd = _dot(diff, dproj_ref[gi]) * dscale_ref[:, cols]
            ycat_ref[rows, C_WIDTH + gi * D_GROUP_DIM:C_WIDTH + (gi + 1) * D_GROUP_DIM] = yd.astype(_BF16)
    _emit_history(xpd_ref, newd_ref, hist_d, tm, carry=not has_state)

    y = _dot(ycat_ref[...], wout_ref[...])
    out_ref[...] = x_ref[...] + y


def _ffn_kernel(*refs, tm, stride, has_state, final_norm):
    refs = list(refs)
    x_ref, g_ref, win_ref, wout_ref, cw_ref = refs[:5]
    del refs[:5]
    state_ref = refs.pop(0) if has_state else None
    gfin_ref = refs.pop(0) if final_norm else None
    out_ref, newf_ref, xn_ref, gp_ref, hist_ref, h_ref = refs
    rb = ROW_BLOCK
    hist = (FFN_CONV - 1) * stride
    h_rows = _round_up(hist, SUBLANES)
    _init_history(hist_ref, state_ref, hist, pl.program_id(1) == 0)
    _norm_to_bf16(x_ref, g_ref, xn_ref, tm)
    xn = xn_ref[...]

    for j, c0 in enumerate(range(0, D_FF, FF_CHUNK)):
        cols = slice(c0, c0 + FF_CHUNK)
        slot = slice((j % 2) * FF_CHUNK, (j % 2 + 1) * FF_CHUNK)
        gp_ref[0:h_rows, slot] = hist_ref[:, cols]
        zg = _dot(xn, win_ref[:, c0:c0 + FF_CHUNK])
        zu = _dot(xn, win_ref[:, D_FF + c0:D_FF + c0 + FF_CHUNK])
        for r0 in range(0, tm, rb):
            rows = slice(r0, r0 + rb)
            gp_ref[h_rows + r0:h_rows + r0 + rb, slot] = zg[rows]
            conv = _conv_taps(gp_ref, cw_ref, r0, cols, FFN_CONV, stride, rb, xcols=slot)
            h_ref[rows, cols] = (jax.nn.gelu(conv) * zu[rows]).astype(_BF16)
        hist_ref[:, cols] = gp_ref[tm:tm + h_rows, slot]
    newf_ref[0] = hist_ref[h_rows - hist:h_rows, :]

    y = x_ref[...] + _dot(h_ref[...], wout_ref[...])
    if final_norm:
        gfin = gfin_ref[...]
        out_ref[...] = y
        for r0 in range(0, tm, NORM_BLOCK):
            rows = slice(r0, r0 + NORM_BLOCK)
            out_ref[rows, :] = _rms_rows(out_ref[rows, :], gfin)
    else:
        out_ref[...] = y


def _const_spec(shape):
    nd = len(shape)
    return pl.BlockSpec(shape, lambda b, t: (0,) * nd, pipeline_mode=pl.Buffered(1))


def _layer_spec(arr, layer):
    nd = arr.ndim - 1
    return pl.BlockSpec((None,) + arr.shape[1:], lambda b, t: (layer,) + (0,) * nd, pipeline_mode=pl.Buffered(1))


def _smem_spec():
    return pl.BlockSpec(memory_space=pltpu.SMEM)


def _tile_spec(tm, nt, width):
    return pl.BlockSpec((tm, width), lambda b, t: (b * nt + t, 0))


def _state_spec(layer, rows, width):
    return pl.BlockSpec((None, 1, rows, width), lambda b, t: (layer, b, 0, 0))


def _new_state_spec(rows, width):
    return pl.BlockSpec((1, rows, width), lambda b, t: (b, 0, 0))


def _params():
    return pltpu.CompilerParams(dimension_semantics=("arbitrary", "arbitrary"), vmem_limit_bytes=VMEM_LIMIT)


def _even_call(x2, nb, nt, stride, state, layer, prm, name):
    has_state = state is not None
    i = layer // 2
    tm = x2.shape[0] // (nb * nt)
    hist = (B_CONV - 1) * stride
    h_rows = _round_up(hist, SUBLANES)
    in_specs = [_tile_spec(tm, nt, D_MODEL), _layer_spec(prm['norm_mix_g'], layer), _layer_spec(prm['w_in_even'], i),
                _layer_spec(prm['w_out_even'], i), _layer_spec(prm['a_ln_g'], i), _layer_spec(prm['b_conv_w'], i)]
    args = [x2, prm['norm_mix_g'], prm['w_in_even'], prm['w_out_even'], prm['a_ln_g'], prm['b_conv_w']]
    if has_state:
        in_specs += [_smem_spec(), _smem_spec(), _state_spec(i, hist, B_WIDTH)]
        args += [prm['a_ws_steps'], prm['a_bs_steps'], state]
    else:
        in_specs += [_layer_spec(prm['a_ws'], i), _layer_spec(prm['a_bs_t'], i)]
        args += [prm['a_ws'], prm['a_bs_t']]
    out_shape = [jax.ShapeDtypeStruct(x2.shape, _F32), jax.ShapeDtypeStruct((nb, hist, B_WIDTH), _F32)]
    out_specs = [_tile_spec(tm, nt, D_MODEL), _new_state_spec(hist, B_WIDTH)]
    if has_state:
        out_shape.append(jax.ShapeDtypeStruct((x2.shape[0], A_WIDTH), _F32))
        out_specs.append(_tile_spec(tm, nt, A_WIDTH))
    scratch = [pltpu.VMEM((tm, D_MODEL), _BF16), pltpu.VMEM((h_rows + tm, B_WIDTH), _F32),
               pltpu.VMEM((tm, A_WIDTH + B_WIDTH), _BF16), pltpu.VMEM((tm, A_WIDTH), _BF16)]
    return pl.pallas_call(
        functools.partial(_even_kernel, tm=tm, stride=stride, has_state=has_state, gate_layer=i),
        grid=(nb, nt), in_specs=in_specs, out_specs=out_specs, out_shape=out_shape,
        scratch_shapes=scratch, compiler_params=_params(), name=name)(*args)


def _odd_call(x2, nb, nt, stride, state_c, state_d, start_pos, layer, prm, name):
    has_state = state_c is not None
    i = layer // 2
    hist_c = (C_CONV - 1) * stride
    hist_d = POOL_BUF * stride
    hc_rows = _round_up(hist_c, SUBLANES)
    hd_rows = _round_up(hist_d, SUBLANES)
    tm = x2.shape[0] // (nb * nt)
    names = ['w_in_odd', 'w_out_odd', 'c_conv_w', 'c_conv_b', 'c_ln_g', 'c_ln_b', 'd_proj', 'd_scale']
    in_specs = [_tile_spec(tm, nt, D_MODEL), _layer_spec(prm['norm_mix_g'], layer)] + [_layer_spec(prm[k], i) for k in names]
    args = [x2, prm['norm_mix_g']] + [prm[k] for k in names]
    if has_state:
        in_specs += [_state_spec(i, hist_c, C_WIDTH), _state_spec(i, hist_d, D_WIDTH)]
        args += [state_c, state_d]
    out_shape = [jax.ShapeDtypeStruct(x2.shape, _F32), jax.ShapeDtypeStruct((nb, hist_c, C_WIDTH), _F32),
                 jax.ShapeDtypeStruct((nb, hist_d, D_WIDTH), _F32)]
    out_specs = [_tile_spec(tm, nt, D_MODEL), _new_state_spec(hist_c, C_WIDTH), _new_state_spec(hist_d, D_WIDTH)]
    scratch = [pltpu.VMEM((tm, D_MODEL), _BF16), pltpu.VMEM((hc_rows + tm + SUBLANES, C_WIDTH), _F32),
               pltpu.VMEM((hd_rows + tm, D_WIDTH), _F32), pltpu.VMEM((tm, C_WIDTH + D_WIDTH), _BF16)]
    return pl.pallas_call(
        functools.partial(_odd_kernel, tm=tm, stride=stride, has_state=has_state, start_pos=start_pos),
        grid=(nb, nt), in_specs=in_specs, out_specs=out_specs, out_shape=out_shape,
        scratch_shapes=scratch, compiler_params=_params(), name=name)(*args)


def _ffn_call(x2, nb, nt, stride, state, layer, final_norm, prm, name):
    has_state = state is not None
    hist = (FFN_CONV - 1) * stride
    h_rows = _round_up(hist, SUBLANES)
    tm = x2.shape[0] // (nb * nt)
    names = ['norm_ffn_g', 'w_ffn_in', 'w_ffn_out', 'ffn_conv_w']
    in_specs = [_tile_spec(tm, nt, D_MODEL)] + [_layer_spec(prm[k], layer) for k in names]
    args = [x2] + [prm[k] for k in names]
    if has_state:
        in_specs.append(_state_spec(layer, hist, D_FF))
        args.append(state)
    if final_norm:
        in_specs.append(_const_spec((1, D_MODEL)))
        args.append(prm['norm_final_g'])
    out_shape = [jax.ShapeDtypeStruct(x2.shape, _F32), jax.ShapeDtypeStruct((nb, hist, D_FF), _F32)]
    out_specs = [_tile_spec(tm, nt, D_MODEL), _new_state_spec(hist, D_FF)]
    scratch = [pltpu.VMEM((tm, D_MODEL), _BF16), pltpu.VMEM((h_rows + tm, 2 * FF_CHUNK), _F32),
               pltpu.VMEM((h_rows, D_FF), _F32), pltpu.VMEM((tm, D_FF), _BF16)]
    return pl.pallas_call(
        functools.partial(_ffn_kernel, tm=tm, stride=stride, has_state=has_state, final_norm=final_norm),
        grid=(nb, nt), in_specs=in_specs, out_specs=out_specs, out_shape=out_shape,
        scratch_shapes=scratch, compiler_params=_params(), name=name)(*args)


def _to_tiles(a, ns):
    *lead, n, l, c = a.shape
    k = len(lead)
    a = a.reshape(*lead, n // ns, ns, l, c)
    a = a.transpose(*range(k), k, k + 2, k + 1, k + 3)
    return a.reshape(*lead, n // ns, l * ns, c)


def _from_tiles(a, ns, l):
    *lead, nb, _, c = a.shape
    k = len(lead)
    a = a.reshape(*lead, nb, l, ns, c)
    a = a.transpose(*range(k), k, k + 2, k + 1, k + 3)
    return a.reshape(*lead, nb * ns, l, c)


def _trunk(x2, nb, nt, stride, states, start_pos, prm, tag):
    new_a, new_b, new_c, new_d, new_f = [], [], [], [], []
    depth = prm['w_ffn_in'].shape[0]
    for layer in range(depth):
        if layer % 2 == 0:
            st = None if states is None else states['b']
            res = _even_call(x2, nb, nt, stride, st, layer, prm, name=f"{tag}_mix{layer}")
            x2 = res[0]
            new_b.append(res[1])
            if states is not None:
                new_a.append(res[2])
        else:
            st_c = None if states is None else states['c']
            st_d = None if states is None else states['d']
            x2, nc, nd = _odd_call(x2, nb, nt, stride, st_c, st_d, start_pos, layer, prm, name=f"{tag}_mix{layer}")
            new_c.append(nc)
            new_d.append(nd)
        st_f = None if states is None else states['f']
        x2, nf = _ffn_call(x2, nb, nt, stride, st_f, layer, layer == depth - 1, prm, name=f"{tag}_ffn{layer}")
        new_f.append(nf)
    return x2, new_a, new_b, new_c, new_d, new_f


def kernel(x_prompt, x_sample, state_b_conv, state_c_conv, state_d_pool, state_ffn_conv, norm_mix_g, norm_ffn_g, norm_final_g, w_in_even, w_out_even, a_ln_g, a_ws, a_bs, b_conv_w, w_in_odd, w_out_odd, c_conv_w, c_conv_b, c_ln_g, c_ln_b, d_proj, d_scale, w_ffn_in, ffn_conv_w, w_ffn_out):
    batch, seq, d = x_prompt.shape
    dec_batch, dec_seq, _ = x_sample.shape
    ns = SAMPLE_TILE // dec_seq

    def rows(p):
        return p.reshape(p.shape[0], 1, p.shape[1])

    prm = dict(norm_mix_g=rows(norm_mix_g), norm_ffn_g=rows(norm_ffn_g), norm_final_g=norm_final_g.reshape(1, -1),
               w_in_even=w_in_even.astype(_BF16), w_out_even=w_out_even.astype(_BF16), a_ln_g=rows(a_ln_g),
               a_ws=a_ws, a_bs_t=jnp.swapaxes(a_bs, 1, 2),
               a_ws_steps=a_ws[:, :, :dec_seq, :dec_seq].reshape(-1), a_bs_steps=a_bs[:, :, :dec_seq].reshape(-1),
               b_conv_w=b_conv_w, w_in_odd=w_in_odd.astype(_BF16), w_out_odd=w_out_odd.astype(_BF16),
               c_conv_w=jnp.pad(c_conv_w, ((0, 0), (0, _round_up(C_CONV, SUBLANES) - C_CONV), (0, 0))),
               c_conv_b=rows(c_conv_b), c_ln_g=rows(c_ln_g), c_ln_b=rows(c_ln_b), d_proj=d_proj.astype(_BF16),
               d_scale=rows(d_scale), w_ffn_in=w_ffn_in.astype(_BF16), ffn_conv_w=ffn_conv_w,
               w_ffn_out=w_ffn_out.astype(_BF16))

    yp, _, b_p, c_p, d_p, f_p = _trunk(x_prompt.reshape(batch * seq, d), batch, seq // PROMPT_TILE, 1, None, 0, prm, "p")

    states = dict(b=_to_tiles(state_b_conv, ns), c=_to_tiles(state_c_conv, ns), d=_to_tiles(state_d_pool, ns),
                  f=_to_tiles(state_ffn_conv, ns))
    xs = _to_tiles(x_sample, ns).reshape(dec_batch * dec_seq, d)
    ys, a_s, b_s, c_s, d_s, f_s = _trunk(xs, dec_batch // ns, 1, ns, states, PAST_LEN, prm, "s")

    def untile(parts, l):
        return _from_tiles(jnp.stack(parts), ns, l)

    y_prompt = yp.reshape(batch, seq, d)
    y_sample = _from_tiles(ys.reshape(dec_batch // ns, dec_seq * ns, d), ns, dec_seq)
    new_a = untile([a.reshape(dec_batch // ns, dec_seq * ns, A_WIDTH) for a in a_s], dec_seq)
    return (y_prompt, y_sample, new_a, jnp.stack(b_p), untile(b_s, B_CONV - 1), jnp.stack(c_p),
            untile(c_s, C_CONV - 1), jnp.stack(d_p), untile(d_s, POOL_BUF), jnp.stack(f_p),
            untile(f_s, FFN_CONV - 1))
```

```python
import functools

import jax
import jax.numpy as jnp
from jax import lax
from jax.experimental import pallas as pl
from jax.experimental.pallas import tpu as pltpu

D_MODEL = 1024
CHUNK = 128
A_HEADS = 4
A_WIDTH = 512
B_WIDTH = 512
B_CONV = 3
C_WIDTH = 512
C_CONV = 31
D_WIDTH = 512
POOL_WINDOWS = (2, 4, 8, 16)
D_GROUP_DIM = 128
POOL_BUF = 15
D_FF = 2816
FFN_CONV = 3
PAST_LEN = 16384
EPS = 1e-6

SUBLANES = 8
LANES = 128
PROMPT_TILE = 1024
SAMPLE_TILE = 512
SUB_TILE = 512
ROW_BLOCK = 64
NORM_BLOCK = 32
FF_CHUNK = 256
VMEM_LIMIT = 56 * 1024 * 1024

_BF16 = jnp.bfloat16
_F32 = jnp.float32


def _round_up(n, m):
    return (n + m - 1) // m * m


def _dot(a, b):
    return jnp.dot(a, b, preferred_element_type=_F32)


def _rms_rows(x, g):
    y = x * lax.rsqrt(jnp.mean(x * x, axis=-1, keepdims=True) + EPS)
    return y * g


def _ln_rows(x, g, b=None):
    mu = jnp.mean(x, axis=-1, keepdims=True)
    xc = x - mu
    y = xc * lax.rsqrt(jnp.mean(xc * xc, axis=-1, keepdims=True) + EPS) * g
    if b is not None:
        y = y + b
    return y


def _norm_to_bf16(x_ref, g_ref, xn_ref, s0, sub):
    g = g_ref[...]
    for r0 in range(s0, s0 + sub, NORM_BLOCK):
        rows = slice(r0, r0 + NORM_BLOCK)
        xn_ref[rows, :] = _rms_rows(x_ref[rows, :], g).astype(_BF16)


def _init_history(xp_ref, state_ref, hist, first_tile):
    h_rows = _round_up(hist, SUBLANES)
    if state_ref is None:
        @pl.when(first_tile)
        def _():
            xp_ref[0:h_rows, :] = jnp.zeros((h_rows, xp_ref.shape[1]), _F32)
    else:
        xp_ref[h_rows - hist:h_rows, :] = state_ref[0]


def _emit_history(xp_ref, new_ref, hist, tm, carry):
    h_rows = _round_up(hist, SUBLANES)
    tail = xp_ref[h_rows + tm - hist:h_rows + tm, :]
    new_ref[0] = tail
    if carry:
        xp_ref[h_rows - hist:h_rows, :] = tail


def _conv_taps(xp_ref, w_ref, r0, cols, ktaps, stride, rb, xcols=None):
    xcols = cols if xcols is None else xcols
    hist = (ktaps - 1) * stride
    base = _round_up(hist, SUBLANES) - hist + r0
    acc = None
    for k in range(ktaps):
        term = w_ref[k:k + 1, cols] * xp_ref[base + k * stride:base + k * stride + rb, xcols]
        acc = term if acc is None else acc + term
    return acc


def _conv_long_stride1(xp_ref, w_ref, r0, cols, ktaps, rb):
    hist = ktaps - 1
    base = _round_up(hist, SUBLANES) - hist
    n = rb + SUBLANES
    p = None
    for r in range(SUBLANES - 1, -1, -1):
        z = None
        for k in range(ktaps):
            if (base + k) % SUBLANES != r:
                continue
            off = r0 + (base + k) - r
            term = w_ref[k:k + 1, cols] * xp_ref[off:off + n, cols]
            z = term if z is None else z + term
        if p is not None:
            p = pltpu.roll(p, n - 1, 0)
            p = p if z is None else p + z
        else:
            p = z
    return p[0:rb]


def _even_kernel(*refs, tm, stride, has_state, gate_layer):
    if has_state:
        (x_ref, g_ref, win_ref, wout_ref, lng_ref, cw_ref, gw_ref, gb_ref, state_ref,
         out_ref, newb_ref, v_ref, xn_ref, xp_ref, ycat_ref, vb_ref) = refs
    else:
        (x_ref, g_ref, win_ref, wout_ref, lng_ref, cw_ref, gw_ref, gb_ref,
         out_ref, newb_ref, xn_ref, xp_ref, ycat_ref, vb_ref) = refs
        state_ref = None
        v_ref = None
    rb = ROW_BLOCK
    hist = (B_CONV - 1) * stride
    h_rows = _round_up(hist, SUBLANES)
    _init_history(xp_ref, state_ref, hist, pl.program_id(1) == 0)
    lng = lng_ref[...]
    allc = slice(0, B_WIDTH)
    if stride == 1:
        row_i = lax.broadcasted_iota(jnp.int32, (CHUNK, CHUNK), 0)
        col_i = lax.broadcasted_iota(jnp.int32, (CHUNK, CHUNK), 1)
        wts = [jnp.where(col_i <= row_i, gw_ref[h], 0.0).astype(_BF16) for h in range(A_HEADS)]

    sub = min(tm, SUB_TILE)
    for s0 in range(0, tm, sub):
        srows = slice(s0, s0 + sub)
        _norm_to_bf16(x_ref, g_ref, xn_ref, s0, sub)
        xn = xn_ref[srows, :]

        zbg = _dot(xn, win_ref[:, 2 * A_WIDTH:2 * A_WIDTH + B_WIDTH])
        zcg = _dot(xn, win_ref[:, 2 * A_WIDTH + B_WIDTH:2 * A_WIDTH + 2 * B_WIDTH])
        zh = _dot(xn, win_ref[:, 2 * A_WIDTH + 2 * B_WIDTH:2 * A_WIDTH + 3 * B_WIDTH])
        for r0 in range(s0, s0 + sub, rb):
            rows = slice(r0, r0 + rb)
            loc = slice(r0 - s0, r0 - s0 + rb)
            xp_ref[h_rows + r0:h_rows + r0 + rb, :] = zcg[loc] * zh[loc]
            conv = _conv_taps(xp_ref, cw_ref, r0, allc, B_CONV, stride, rb)
            ycat_ref[rows, A_WIDTH:A_WIDTH + B_WIDTH] = (zbg[loc] * conv).astype(_BF16)

        zu = _dot(xn, win_ref[:, 0:A_WIDTH])
        zv = _dot(xn, win_ref[:, A_WIDTH:2 * A_WIDTH])
        for r0 in range(s0, s0 + sub, NORM_BLOCK):
            rows = slice(r0, r0 + NORM_BLOCK)
            v = _ln_rows(jax.nn.gelu(zv[r0 - s0:r0 - s0 + NORM_BLOCK]), lng)
            if v_ref is not None:
                v_ref[rows, :] = v
            else:
                vb_ref[rows, :] = v.astype(_BF16)
        if stride == 1:
            for c0 in range(s0, s0 + sub, 2 * CHUNK):
                ra = slice(c0, c0 + CHUNK)
                rb2 = slice(c0 + CHUNK, c0 + 2 * CHUNK)
                for h in range(A_HEADS):
                    cols = slice(h * CHUNK, (h + 1) * CHUNK)
                    vpair = jnp.concatenate([vb_ref[ra, cols], vb_ref[rb2, cols]], axis=1)
                    mixed = _dot(wts[h], vpair)
                    bias = gb_ref[:, h:h + 1]
                    for half, rws in enumerate((ra, rb2)):
                        m = mixed[:, half * CHUNK:(half + 1) * CHUNK] + bias
                        u = zu[rws.start - s0:rws.stop - s0, cols]
                        ycat_ref[rws, cols] = (jax.nn.gelu(u) * m).astype(_BF16)
        else:
            steps = tm // stride
            gw_off = gate_layer * A_HEADS * steps * steps
            gb_off = gate_layer * A_HEADS * steps
            for t in range(s0 // stride, (s0 + sub) // stride):
                for n0 in range(0, stride, rb):
                    rows = slice(t * stride + n0, t * stride + n0 + rb)
                    for h in range(A_HEADS):
                        cols = slice(h * CHUNK, (h + 1) * CHUNK)
                        mixed = None
                        for s in range(t + 1):
                            w = gw_ref[gw_off + (h * steps + t) * steps + s]
                            term = w * v_ref[s * stride + n0:s * stride + n0 + rb, cols]
                            mixed = term if mixed is None else mixed + term
                        mixed = mixed + gb_ref[gb_off + h * steps + t]
                        u = zu[rows.start - s0:rows.stop - s0, cols]
                        ycat_ref[rows, cols] = (jax.nn.gelu(u) * mixed).astype(_BF16)

        y = _dot(ycat_ref[srows, :], wout_ref[...])
        out_ref[srows, :] = x_ref[srows, :] + y
    _emit_history(xp_ref, newb_ref, hist, tm, carry=not has_state)


def _pool_block(xp_ref, r0, gi, win, stride, rb, pos0, row_pos_static):
    cols = slice(gi * D_GROUP_DIM, (gi + 1) * D_GROUP_DIM)
    hist = POOL_BUF * stride
    h_rows = _round_up(hist, SUBLANES)
    if stride % SUBLANES == 0:
        s = None
        for j in range(win):
            off = h_rows + r0 - j * stride
            term = xp_ref[off:off + rb, cols]
            s = term if s is None else s + term
    else:
        halo = 2 * SUBLANES
        ext = xp_ref[h_rows + r0 - halo:h_rows + r0 + rb, cols]
        shift = 1
        while shift < win:
            ext = ext + pltpu.roll(ext, shift, 0)
            shift *= 2
        s = ext[halo:]
    if row_pos_static is not None:
        return s / float(min(row_pos_static + 1, win))
    pos = pos0 + r0 + lax.broadcasted_iota(jnp.int32, (rb, D_GROUP_DIM), 0)
    cnt = jnp.minimum(pos + 1, win).astype(_F32)
    return s / cnt


def _odd_kernel(*refs, tm, stride, has_state, start_pos):
    if has_state:
        (x_ref, g_ref, win_ref, wout_ref, ccw_ref, ccb_ref, clg_ref, clb_ref, dproj_ref, dscale_ref,
         statec_ref, stated_ref, out_ref, newc_ref, newd_ref, xn_ref, xpc_ref, xpd_ref, ycat_ref) = refs
    else:
        (x_ref, g_ref, win_ref, wout_ref, ccw_ref, ccb_ref, clg_ref, clb_ref, dproj_ref, dscale_ref,
         out_ref, newc_ref, newd_ref, xn_ref, xpc_ref, xpd_ref, ycat_ref) = refs
        statec_ref = None
        stated_ref = None
    rb = ROW_BLOCK
    hist_c = (C_CONV - 1) * stride
    hc_rows = _round_up(hist_c, SUBLANES)
    hist_d = POOL_BUF * stride
    hd_rows = _round_up(hist_d, SUBLANES)
    first = pl.program_id(1) == 0
    _init_history(xpc_ref, statec_ref, hist_c, first)
    _init_history(xpd_ref, stated_ref, hist_d, first)
    ccb = ccb_ref[...]
    clg = clg_ref[...]
    clb = clb_ref[...]
    pos0 = start_pos + pl.program_id(1) * tm

    sub = min(tm, SUB_TILE)
    for s0 in range(0, tm, sub):
        srows = slice(s0, s0 + sub)
        _norm_to_bf16(x_ref, g_ref, xn_ref, s0, sub)
        xn = xn_ref[srows, :]

        zca = _dot(xn, win_ref[:, 0:C_WIDTH])
        zcb = _dot(xn, win_ref[:, C_WIDTH:2 * C_WIDTH])
        zp = _dot(xn, win_ref[:, 2 * C_WIDTH:2 * C_WIDTH + D_WIDTH])
        for r0 in range(s0, s0 + sub, rb):
            loc = slice(r0 - s0, r0 - s0 + rb)
            xpc_ref[hc_rows + r0:hc_rows + r0 + rb, :] = zca[loc] * jax.nn.sigmoid(zcb[loc])
            xpd_ref[hd_rows + r0:hd_rows + r0 + rb, :] = zp[loc]
        xpc_ref[hc_rows + s0 + sub:hc_rows + s0 + sub + SUBLANES, :] = jnp.zeros((SUBLANES, C_WIDTH), _F32)

        for r0 in range(s0, s0 + sub, rb):
            rows = slice(r0, r0 + rb)
            parts = []
            for c0 in range(0, C_WIDTH, LANES):
                cols = slice(c0, c0 + LANES)
                if stride % SUBLANES == 0:
                    parts.append(_conv_taps(xpc_ref, ccw_ref, r0, cols, C_CONV, stride, rb))
                else:
                    parts.append(_conv_long_stride1(xpc_ref, ccw_ref, r0, cols, C_CONV, rb))
            cc = jnp.concatenate(parts, axis=-1) + ccb
            ycat_ref[rows, 0:C_WIDTH] = jax.nn.silu(_ln_rows(cc, clg, clb)).astype(_BF16)

        for r0 in range(s0, s0 + sub, rb):
            rows = slice(r0, r0 + rb)
            for gi, win in enumerate(POOL_WINDOWS):
                cols = slice(gi * D_GROUP_DIM, (gi + 1) * D_GROUP_DIM)
                if stride == 1:
                    static_pos = None if r0 < POOL_BUF else POOL_BUF
                else:
                    static_pos = start_pos + r0 // stride
                pooled = _pool_block(xpd_ref, r0, gi, win, stride, rb, pos0, static_pos)
                diff = (pooled - xpd_ref[hd_rows + r0:hd_rows + r0 + rb, cols]).astype(_BF16)
                yd = _dot(diff, dproj_ref[gi]) * dscale_ref[:, cols]
                ycat_ref[rows, C_WIDTH + gi * D_GROUP_DIM:C_WIDTH + (gi + 1) * D_GROUP_DIM] = yd.astype(_BF16)

        y = _dot(ycat_ref[srows, :], wout_ref[...])
        out_ref[srows, :] = x_ref[srows, :] + y
    _emit_history(xpc_ref, newc_ref, hist_c, tm, carry=not has_state)
    _emit_history(xpd_ref, newd_ref, hist_d, tm, carry=not has_state)


def _ffn_kernel(*refs, stride, has_state, final_norm):
    refs = list(refs)
    x_ref, g_ref, win_ref, wout_ref, cw_ref = refs[:5]
    del refs[:5]
    state_ref = refs.pop(0) if has_state else None
    gfin_ref = refs.pop(0) if final_norm else None
    out_ref, newf_ref, xn_ref, gp_ref, hist_ref, h_ref = refs
    rb = ROW_BLOCK
    hist = (FFN_CONV - 1) * stride
    h_rows = _round_up(hist, SUBLANES)
    _init_history(hist_ref, state_ref, hist, pl.program_id(1) == 0)
    n_sub, sub, _ = x_ref.shape

    def sub_tile(i, carry):
        xs = x_ref.at[i]
        os = out_ref.at[i]
        _norm_to_bf16(xs, g_ref, xn_ref, 0, sub)
        xn = xn_ref[...]
        for j, c0 in enumerate(range(0, D_FF, FF_CHUNK)):
            cols = slice(c0, c0 + FF_CHUNK)
            slot = slice((j % 2) * FF_CHUNK, (j % 2 + 1) * FF_CHUNK)
            gp_ref[0:h_rows, slot] = hist_ref[:, cols]
            zg = _dot(xn, win_ref[:, c0:c0 + FF_CHUNK])
            zu = _dot(xn, win_ref[:, D_FF + c0:D_FF + c0 + FF_CHUNK])
            for r0 in range(0, sub, rb):
                rows = slice(r0, r0 + rb)
                gp_ref[h_rows + r0:h_rows + r0 + rb, slot] = zg[rows]
                conv = _conv_taps(gp_ref, cw_ref, r0, cols, FFN_CONV, stride, rb, xcols=slot)
                h_ref[rows, cols] = (jax.nn.gelu(conv) * zu[rows]).astype(_BF16)
            hist_ref[:, cols] = gp_ref[sub:sub + h_rows, slot]

        y = xs[...] + _dot(h_ref[...], wout_ref[...])
        if final_norm:
            gfin = gfin_ref[...]
            os[...] = y
            for r0 in range(0, sub, NORM_BLOCK):
                rows = slice(r0, r0 + NORM_BLOCK)
                os[rows, :] = _rms_rows(os[rows, :], gfin)
        else:
            os[...] = y
        return carry

    lax.fori_loop(0, n_sub, sub_tile, 0)
    newf_ref[0] = hist_ref[h_rows - hist:h_rows, :]


def _const_spec(shape):
    nd = len(shape)
    return pl.BlockSpec(shape, lambda b, t: (0,) * nd, pipeline_mode=pl.Buffered(1))


def _layer_spec(arr, layer):
    nd = arr.ndim - 1
    return pl.BlockSpec((None,) + arr.shape[1:], lambda b, t: (layer,) + (0,) * nd, pipeline_mode=pl.Buffered(1))


def _smem_spec():
    return pl.BlockSpec(memory_space=pltpu.SMEM)


def _tile_spec(tm, nt, width):
    return pl.BlockSpec((tm, width), lambda b, t: (b * nt + t, 0))


def _state_spec(layer, rows, width):
    return pl.BlockSpec((None, 1, rows, width), lambda b, t: (layer, b, 0, 0))


def _new_state_spec(rows, width):
    return pl.BlockSpec((1, rows, width), lambda b, t: (b, 0, 0))


def _params():
    return pltpu.CompilerParams(dimension_semantics=("arbitrary", "arbitrary"), vmem_limit_bytes=VMEM_LIMIT)


def _even_call(x2, nb, nt, stride, state, layer, prm, name):
    has_state = state is not None
    i = layer // 2
    tm = x2.shape[0] // (nb * nt)
    hist = (B_CONV - 1) * stride
    h_rows = _round_up(hist, SUBLANES)
    in_specs = [_tile_spec(tm, nt, D_MODEL), _layer_spec(prm['norm_mix_g'], layer), _layer_spec(prm['w_in_even'], i),
                _layer_spec(prm['w_out_even'], i), _layer_spec(prm['a_ln_g'], i), _layer_spec(prm['b_conv_w'], i)]
    args = [x2, prm['norm_mix_g'], prm['w_in_even'], prm['w_out_even'], prm['a_ln_g'], prm['b_conv_w']]
    if has_state:
        in_specs += [_smem_spec(), _smem_spec(), _state_spec(i, hist, B_WIDTH)]
        args += [prm['a_ws_steps'], prm['a_bs_steps'], state]
    else:
        in_specs += [_layer_spec(prm['a_ws'], i), _layer_spec(prm['a_bs_t'], i)]
        args += [prm['a_ws'], prm['a_bs_t']]
    out_shape = [jax.ShapeDtypeStruct(x2.shape, _F32), jax.ShapeDtypeStruct((nb, hist, B_WIDTH), _F32)]
    out_specs = [_tile_spec(tm, nt, D_MODEL), _new_state_spec(hist, B_WIDTH)]
    if has_state:
        out_shape.append(jax.ShapeDtypeStruct((x2.shape[0], A_WIDTH), _F32))
        out_specs.append(_tile_spec(tm, nt, A_WIDTH))
    scratch = [pltpu.VMEM((tm, D_MODEL), _BF16), pltpu.VMEM((h_rows + tm, B_WIDTH), _F32),
               pltpu.VMEM((tm, A_WIDTH + B_WIDTH), _BF16), pltpu.VMEM((tm, A_WIDTH), _BF16)]
    return pl.pallas_call(
        functools.partial(_even_kernel, tm=tm, stride=stride, has_state=has_state, gate_layer=i),
        grid=(nb, nt), in_specs=in_specs, out_specs=out_specs, out_shape=out_shape,
        scratch_shapes=scratch, compiler_params=_params(), name=name)(*args)


def _odd_call(x2, nb, nt, stride, state_c, state_d, start_pos, layer, prm, name):
    has_state = state_c is not None
    i = layer // 2
    hist_c = (C_CONV - 1) * stride
    hist_d = POOL_BUF * stride
    hc_rows = _round_up(hist_c, SUBLANES)
    hd_rows = _round_up(hist_d, SUBLANES)
    tm = x2.shape[0] // (nb * nt)
    names = ['w_in_odd', 'w_out_odd', 'c_conv_w', 'c_conv_b', 'c_ln_g', 'c_ln_b', 'd_proj', 'd_scale']
    in_specs = [_tile_spec(tm, nt, D_MODEL), _layer_spec(prm['norm_mix_g'], layer)] + [_layer_spec(prm[k], i) for k in names]
    args = [x2, prm['norm_mix_g']] + [prm[k] for k in names]
    if has_state:
        in_specs += [_state_spec(i, hist_c, C_WIDTH), _state_spec(i, hist_d, D_WIDTH)]
        args += [state_c, state_d]
    out_shape = [jax.ShapeDtypeStruct(x2.shape, _F32), jax.ShapeDtypeStruct((nb, hist_c, C_WIDTH), _F32),
                 jax.ShapeDtypeStruct((nb, hist_d, D_WIDTH), _F32)]
    out_specs = [_tile_spec(tm, nt, D_MODEL), _new_state_spec(hist_c, C_WIDTH), _new_state_spec(hist_d, D_WIDTH)]
    scratch = [pltpu.VMEM((tm, D_MODEL), _BF16), pltpu.VMEM((hc_rows + tm + SUBLANES, C_WIDTH), _F32),
               pltpu.VMEM((hd_rows + tm, D_WIDTH), _F32), pltpu.VMEM((tm, C_WIDTH + D_WIDTH), _BF16)]
    return pl.pallas_call(
        functools.partial(_odd_kernel, tm=tm, stride=stride, has_state=has_state, start_pos=start_pos),
        grid=(nb, nt), in_specs=in_specs, out_specs=out_specs, out_shape=out_shape,
        scratch_shapes=scratch, compiler_params=_params(), name=name)(*args)


def _ffn_call(x2, nb, nt, stride, state, layer, final_norm, prm, name):
    has_state = state is not None
    hist = (FFN_CONV - 1) * stride
    h_rows = _round_up(hist, SUBLANES)
    tm = x2.shape[0] // (nb * nt)
    sub = min(tm, SUB_TILE)
    n_sub = tm // sub
    x3 = x2.reshape(-1, sub, D_MODEL)
    x_spec = pl.BlockSpec((n_sub, sub, D_MODEL), lambda b, t: (b * nt + t, 0, 0))
    names = ['norm_ffn_g', 'w_ffn_in', 'w_ffn_out', 'ffn_conv_w']
    in_specs = [x_spec] + [_layer_spec(prm[k], layer) for k in names]
    args = [x3] + [prm[k] for k in names]
    if has_state:
        in_specs.append(_state_spec(layer, hist, D_FF))
        args.append(state)
    if final_norm:
        in_specs.append(_const_spec((1, D_MODEL)))
        args.append(prm['norm_final_g'])
    out_shape = [jax.ShapeDtypeStruct(x3.shape, _F32), jax.ShapeDtypeStruct((nb, hist, D_FF), _F32)]
    out_specs = [x_spec, _new_state_spec(hist, D_FF)]
    scratch = [pltpu.VMEM((sub, D_MODEL), _BF16), pltpu.VMEM((h_rows + sub, 2 * FF_CHUNK), _F32),
               pltpu.VMEM((h_rows, D_FF), _F32), pltpu.VMEM((sub, D_FF), _BF16)]
    y3, newf = pl.pallas_call(
        functools.partial(_ffn_kernel, stride=stride, has_state=has_state, final_norm=final_norm),
        grid=(nb, nt), in_specs=in_specs, out_specs=out_specs, out_shape=out_shape,
        scratch_shapes=scratch, compiler_params=_params(), name=name)(*args)
    return y3.reshape(x2.shape), newf


def _to_tiles(a, ns):
    *lead, n, l, c = a.shape
    k = len(lead)
    a = a.reshape(*lead, n // ns, ns, l, c)
    a = a.transpose(*range(k), k, k + 2, k + 1, k + 3)
    return a.reshape(*lead, n // ns, l * ns, c)


def _from_tiles(a, ns, l):
    *lead, nb, _, c = a.shape
    k = len(lead)
    a = a.reshape(*lead, nb, l, ns, c)
    a = a.transpose(*range(k), k, k + 2, k + 1, k + 3)
    return a.reshape(*lead, nb * ns, l, c)


def _trunk(x2, nb, nt, stride, states, start_pos, prm, tag):
    new_a, new_b, new_c, new_d, new_f = [], [], [], [], []
    depth = prm['w_ffn_in'].shape[0]
    for layer in range(depth):
        if layer % 2 == 0:
            st = None if states is None else states['b']
            res = _even_call(x2, nb, nt, stride, st, layer, prm, name=f"{tag}_mix{layer}")
            x2 = res[0]
            new_b.append(res[1])
            if states is not None:
                new_a.append(res[2])
        else:
            st_c = None if states is None else states['c']
            st_d = None if states is None else states['d']
            x2, nc, nd = _odd_call(x2, nb, nt, stride, st_c, st_d, start_pos, layer, prm, name=f"{tag}_mix{layer}")
            new_c.append(nc)
            new_d.append(nd)
        st_f = None if states is None else states['f']
        x2, nf = _ffn_call(x2, nb, nt, stride, st_f, layer, layer == depth - 1, prm, name=f"{tag}_ffn{layer}")
        new_f.append(nf)
    return x2, new_a, new_b, new_c, new_d, new_f


def kernel(x_prompt, x_sample, state_b_conv, state_c_conv, state_d_pool, state_ffn_conv, norm_mix_g, norm_ffn_g, norm_final_g, w_in_even, w_out_even, a_ln_g, a_ws, a_bs, b_conv_w, w_in_odd, w_out_odd, c_conv_w, c_conv_b, c_ln_g, c_ln_b, d_proj, d_scale, w_ffn_in, ffn_conv_w, w_ffn_out):
    batch, seq, d = x_prompt.shape
    dec_batch, dec_seq, _ = x_sample.shape
    ns = SAMPLE_TILE // dec_seq

    def rows(p):
        return p.reshape(p.shape[0], 1, p.shape[1])

    prm = dict(norm_mix_g=rows(norm_mix_g), norm_ffn_g=rows(norm_ffn_g), norm_final_g=norm_final_g.reshape(1, -1),
               w_in_even=w_in_even.astype(_BF16), w_out_even=w_out_even.astype(_BF16), a_ln_g=rows(a_ln_g),
               a_ws=a_ws, a_bs_t=jnp.swapaxes(a_bs, 1, 2),
               a_ws_steps=a_ws[:, :, :dec_seq, :dec_seq].reshape(-1), a_bs_steps=a_bs[:, :, :dec_seq].reshape(-1),
               b_conv_w=b_conv_w, w_in_odd=w_in_odd.astype(_BF16), w_out_odd=w_out_odd.astype(_BF16),
               c_conv_w=jnp.pad(c_conv_w, ((0, 0), (0, _round_up(C_CONV, SUBLANES) - C_CONV), (0, 0))),
               c_conv_b=rows(c_conv_b), c_ln_g=rows(c_ln_g), c_ln_b=rows(c_ln_b), d_proj=d_proj.astype(_BF16),
               d_scale=rows(d_scale), w_ffn_in=w_ffn_in.astype(_BF16), ffn_conv_w=ffn_conv_w,
               w_ffn_out=w_ffn_out.astype(_BF16))

    yp, _, b_p, c_p, d_p, f_p = _trunk(x_prompt.reshape(batch * seq, d), batch, seq // PROMPT_TILE, 1, None, 0, prm, "p")

    states = dict(b=_to_tiles(state_b_conv, ns), c=_to_tiles(state_c_conv, ns), d=_to_tiles(state_d_pool, ns),
                  f=_to_tiles(state_ffn_conv, ns))
    xs = _to_tiles(x_sample, ns).reshape(dec_batch * dec_seq, d)
    ys, a_s, b_s, c_s, d_s, f_s = _trunk(xs, dec_batch // ns, 1, ns, states, PAST_LEN, prm, "s")

    def untile(parts, l):
        return _from_tiles(jnp.stack(parts), ns, l)

    y_prompt = yp.reshape(batch, seq, d)
    y_sample = _from_tiles(ys.reshape(dec_batch // ns, dec_seq * ns, d), ns, dec_seq)
    new_a = untile([a.reshape(dec_batch // ns, dec_seq * ns, A_WIDTH) for a in a_s], dec_seq)
    return (y_prompt, y_sample, new_a, jnp.stack(b_p), untile(b_s, B_CONV - 1), jnp.stack(c_p),
            untile(c_s, C_CONV - 1), jnp.stack(d_p), untile(d_s, POOL_BUF), jnp.stack(f_p),
            untile(f_s, FFN_CONV - 1))
```

```python
import functools

import jax
import jax.numpy as jnp
from jax import lax
from jax.experimental import pallas as pl
from jax.experimental.pallas import tpu as pltpu

D_MODEL = 1024
CHUNK = 128
A_HEADS = 4
A_WIDTH = 512
B_WIDTH = 512
B_CONV = 3
C_WIDTH = 512
C_CONV = 31
D_WIDTH = 512
POOL_WINDOWS = (2, 4, 8, 16)
D_GROUP_DIM = 128
POOL_BUF = 15
D_FF = 2816
FFN_CONV = 3
PAST_LEN = 16384
EPS = 1e-6

SUBLANES = 8
LANES = 128
TILE = 512
ROW_BLOCK = 64
NORM_BLOCK = 32
FF_CHUNK = 256
VMEM_LIMIT = 56 * 1024 * 1024

_BF16 = jnp.bfloat16
_F32 = jnp.float32


def _round_up(n, m):
    return (n + m - 1) // m * m


def _dot(a, b):
    return jnp.dot(a, b, preferred_element_type=_F32)


def _rms_rows(x, g):
    y = x * lax.rsqrt(jnp.mean(x * x, axis=-1, keepdims=True) + EPS)
    return y * g


def _ln_rows(x, g, b=None):
    mu = jnp.mean(x, axis=-1, keepdims=True)
    xc = x - mu
    y = xc * lax.rsqrt(jnp.mean(xc * xc, axis=-1, keepdims=True) + EPS) * g
    if b is not None:
        y = y + b
    return y


def _norm_to_bf16(x_ref, g_ref, xn_ref, tm):
    g = g_ref[...]
    for r0 in range(0, tm, NORM_BLOCK):
        rows = slice(r0, r0 + NORM_BLOCK)
        xn_ref[rows, :] = _rms_rows(x_ref[rows, :], g).astype(_BF16)


def _tile_pipeline(x_hbm, out_hbm, xbuf, obuf, in_sem, out_sem, n_tiles, tile_fn):
    def in_copy(i, slot):
        return pltpu.make_async_copy(x_hbm.at[i], xbuf.at[slot], in_sem.at[slot])

    def out_copy(i, slot):
        return pltpu.make_async_copy(obuf.at[slot], out_hbm.at[i], out_sem.at[slot])

    in_copy(0, 0).start()

    def body(i, carry):
        slot = lax.rem(i, 2)
        in_copy(i, slot).wait()

        @pl.when(i + 1 < n_tiles)
        def _():
            in_copy(i + 1, 1 - slot).start()

        @pl.when(i >= 2)
        def _():
            out_copy(i - 2, slot).wait()

        tile_fn(i, xbuf.at[slot], obuf.at[slot])
        out_copy(i, slot).start()
        return carry

    lax.fori_loop(0, n_tiles, body, 0)
    for j in range(max(n_tiles - 2, 0), n_tiles):
        out_copy(j, j % 2).wait()


def _load_history(xp_ref, state_ref, hist, i, tiles_per_seq):
    h_rows = _round_up(hist, SUBLANES)
    if state_ref is None:
        @pl.when(lax.rem(i, tiles_per_seq) == 0)
        def _():
            xp_ref[0:h_rows, :] = jnp.zeros((h_rows, xp_ref.shape[1]), _F32)
    else:
        xp_ref[h_rows - hist:h_rows, :] = state_ref[i]


def _store_history(xp_ref, new_ref, hist, tm, i, tiles_per_seq, carry):
    h_rows = _round_up(hist, SUBLANES)
    tail = xp_ref[h_rows + tm - hist:h_rows + tm, :]
    new_ref[lax.div(i, tiles_per_seq)] = tail
    if carry:
        xp_ref[h_rows - hist:h_rows, :] = tail


def _conv_taps(xp_ref, w_ref, r0, cols, ktaps, stride, rb, xcols=None):
    xcols = cols if xcols is None else xcols
    hist = (ktaps - 1) * stride
    base = _round_up(hist, SUBLANES) - hist + r0
    acc = None
    for k in range(ktaps):
        term = w_ref[k:k + 1, cols] * xp_ref[base + k * stride:base + k * stride + rb, xcols]
        acc = term if acc is None else acc + term
    return acc


def _conv_long_stride1(xp_ref, w_ref, r0, cols, ktaps, rb):
    hist = ktaps - 1
    base = _round_up(hist, SUBLANES) - hist
    n = rb + SUBLANES
    p = None
    for r in range(SUBLANES - 1, -1, -1):
        z = None
        for k in range(ktaps):
            if (base + k) % SUBLANES != r:
                continue
            off = r0 + (base + k) - r
            term = w_ref[k:k + 1, cols] * xp_ref[off:off + n, cols]
            z = term if z is None else z + term
        if p is not None:
            p = pltpu.roll(p, n - 1, 0)
            p = p if z is None else p + z
        else:
            p = z
    return p[0:rb]


def _even_kernel(*refs, n_tiles, tiles_per_seq, stride, has_state, gate_layer):
    if has_state:
        (x_hbm, g_ref, win_ref, wout_ref, lng_ref, cw_ref, gw_ref, gb_ref, state_ref,
         out_hbm, newb_ref, v_ref, xbuf, obuf, in_sem, out_sem, xn_ref, xp_ref, ycat_ref, vb_ref) = refs
    else:
        (x_hbm, g_ref, win_ref, wout_ref, lng_ref, cw_ref, gw_ref, gb_ref,
         out_hbm, newb_ref, xbuf, obuf, in_sem, out_sem, xn_ref, xp_ref, ycat_ref, vb_ref) = refs
        state_ref = None
        v_ref = None
    tm = xbuf.shape[1]
    rb = ROW_BLOCK
    hist = (B_CONV - 1) * stride
    h_rows = _round_up(hist, SUBLANES)
    allc = slice(0, B_WIDTH)

    def tile(i, x_ref, out_ref):
        _load_history(xp_ref, state_ref, hist, i, tiles_per_seq)
        _norm_to_bf16(x_ref, g_ref, xn_ref, tm)
        xn = xn_ref[...]
        lng = lng_ref[...]

        zbg = _dot(xn, win_ref[:, 2 * A_WIDTH:2 * A_WIDTH + B_WIDTH])
        zcg = _dot(xn, win_ref[:, 2 * A_WIDTH + B_WIDTH:2 * A_WIDTH + 2 * B_WIDTH])
        zh = _dot(xn, win_ref[:, 2 * A_WIDTH + 2 * B_WIDTH:2 * A_WIDTH + 3 * B_WIDTH])
        for r0 in range(0, tm, rb):
            rows = slice(r0, r0 + rb)
            xp_ref[h_rows + r0:h_rows + r0 + rb, :] = zcg[rows] * zh[rows]
            conv = _conv_taps(xp_ref, cw_ref, r0, allc, B_CONV, stride, rb)
            ycat_ref[rows, A_WIDTH:A_WIDTH + B_WIDTH] = (zbg[rows] * conv).astype(_BF16)
        _store_history(xp_ref, newb_ref, hist, tm, i, tiles_per_seq, carry=not has_state)

        zu = _dot(xn, win_ref[:, 0:A_WIDTH])
        zv = _dot(xn, win_ref[:, A_WIDTH:2 * A_WIDTH])
        vt_ref = None if v_ref is None else v_ref.at[i]
        for r0 in range(0, tm, NORM_BLOCK):
            rows = slice(r0, r0 + NORM_BLOCK)
            v = _ln_rows(jax.nn.gelu(zv[rows]), lng)
            if vt_ref is not None:
                vt_ref[rows, :] = v
            else:
                vb_ref[rows, :] = v.astype(_BF16)
        if stride == 1:
            row_i = lax.broadcasted_iota(jnp.int32, (CHUNK, CHUNK), 0)
            col_i = lax.broadcasted_iota(jnp.int32, (CHUNK, CHUNK), 1)
            for h in range(A_HEADS):
                cols = slice(h * CHUNK, (h + 1) * CHUNK)
                wt = jnp.where(col_i <= row_i, gw_ref[h], 0.0).astype(_BF16)
                bias = gb_ref[:, h:h + 1]
                for c0 in range(0, tm, 2 * CHUNK):
                    pair = (slice(c0, c0 + CHUNK), slice(c0 + CHUNK, c0 + 2 * CHUNK))
                    vpair = jnp.concatenate([vb_ref[pair[0], cols], vb_ref[pair[1], cols]], axis=1)
                    mixed = _dot(wt, vpair)
                    for half, rows in enumerate(pair):
                        m = mixed[:, half * CHUNK:(half + 1) * CHUNK] + bias
                        ycat_ref[rows, cols] = (jax.nn.gelu(zu[rows, cols]) * m).astype(_BF16)
        else:
            steps = tm // stride
            gw_off = gate_layer * A_HEADS * steps * steps
            gb_off = gate_layer * A_HEADS * steps
            for t in range(steps):
                for n0 in range(0, stride, rb):
                    rows = slice(t * stride + n0, t * stride + n0 + rb)
                    for h in range(A_HEADS):
                        cols = slice(h * CHUNK, (h + 1) * CHUNK)
                        mixed = None
                        for s in range(t + 1):
                            w = gw_ref[gw_off + (h * steps + t) * steps + s]
                            term = w * vt_ref[s * stride + n0:s * stride + n0 + rb, cols]
                            mixed = term if mixed is None else mixed + term
                        mixed = mixed + gb_ref[gb_off + h * steps + t]
                        ycat_ref[rows, cols] = (jax.nn.gelu(zu[rows, cols]) * mixed).astype(_BF16)

        y = _dot(ycat_ref[...], wout_ref[...])
        out_ref[...] = x_ref[...] + y

    _tile_pipeline(x_hbm, out_hbm, xbuf, obuf, in_sem, out_sem, n_tiles, tile)


def _pool_block(xp_ref, r0, gi, win, stride, rb, pos0, row_pos_static):
    cols = slice(gi * D_GROUP_DIM, (gi + 1) * D_GROUP_DIM)
    hist = POOL_BUF * stride
    h_rows = _round_up(hist, SUBLANES)
    if stride % SUBLANES == 0:
        s = None
        for j in range(win):
            off = h_rows + r0 - j * stride
            term = xp_ref[off:off + rb, cols]
            s = term if s is None else s + term
    else:
        halo = 2 * SUBLANES
        ext = xp_ref[h_rows + r0 - halo:h_rows + r0 + rb, cols]
        shift = 1
        while shift < win:
            ext = ext + pltpu.roll(ext, shift, 0)
            shift *= 2
        s = ext[halo:]
    if row_pos_static is not None:
        return s / float(min(row_pos_static + 1, win))
    pos = pos0 + r0 + lax.broadcasted_iota(jnp.int32, (rb, D_GROUP_DIM), 0)
    cnt = jnp.minimum(pos + 1, win).astype(_F32)
    return s / cnt


def _odd_kernel(*refs, n_tiles, tiles_per_seq, stride, has_state, start_pos):
    if has_state:
        (x_hbm, g_ref, win_ref, wout_ref, ccw_ref, ccb_ref, clg_ref, clb_ref, dproj_ref, dscale_ref,
         statec_ref, stated_ref, out_hbm, newc_ref, newd_ref,
         xbuf, obuf, in_sem, out_sem, xn_ref, xpc_ref, xpd_ref, ycat_ref) = refs
    else:
        (x_hbm, g_ref, win_ref, wout_ref, ccw_ref, ccb_ref, clg_ref, clb_ref, dproj_ref, dscale_ref,
         out_hbm, newc_ref, newd_ref,
         xbuf, obuf, in_sem, out_sem, xn_ref, xpc_ref, xpd_ref, ycat_ref) = refs
        statec_ref = None
        stated_ref = None
    tm = xbuf.shape[1]
    rb = ROW_BLOCK
    hist_c = (C_CONV - 1) * stride
    hc_rows = _round_up(hist_c, SUBLANES)
    hist_d = POOL_BUF * stride
    hd_rows = _round_up(hist_d, SUBLANES)

    def tile(i, x_ref, out_ref):
        _load_history(xpc_ref, statec_ref, hist_c, i, tiles_per_seq)
        _load_history(xpd_ref, stated_ref, hist_d, i, tiles_per_seq)
        xpc_ref[hc_rows + tm:hc_rows + tm + SUBLANES, :] = jnp.zeros((SUBLANES, C_WIDTH), _F32)
        _norm_to_bf16(x_ref, g_ref, xn_ref, tm)
        xn = xn_ref[...]

        zca = _dot(xn, win_ref[:, 0:C_WIDTH])
        zcb = _dot(xn, win_ref[:, C_WIDTH:2 * C_WIDTH])
        zp = _dot(xn, win_ref[:, 2 * C_WIDTH:2 * C_WIDTH + D_WIDTH])
        for r0 in range(0, tm, rb):
            rows = slice(r0, r0 + rb)
            xpc_ref[hc_rows + r0:hc_rows + r0 + rb, :] = zca[rows] * jax.nn.sigmoid(zcb[rows])
            xpd_ref[hd_rows + r0:hd_rows + r0 + rb, :] = zp[rows]

        ccb = ccb_ref[...]
        clg = clg_ref[...]
        clb = clb_ref[...]
        for r0 in range(0, tm, rb):
            rows = slice(r0, r0 + rb)
            parts = []
            for c0 in range(0, C_WIDTH, LANES):
                cols = slice(c0, c0 + LANES)
                if stride % SUBLANES == 0:
                    parts.append(_conv_taps(xpc_ref, ccw_ref, r0, cols, C_CONV, stride, rb))
                else:
                    parts.append(_conv_long_stride1(xpc_ref, ccw_ref, r0, cols, C_CONV, rb))
            cc = jnp.concatenate(parts, axis=-1) + ccb
            ycat_ref[rows, 0:C_WIDTH] = jax.nn.silu(_ln_rows(cc, clg, clb)).astype(_BF16)
        _store_history(xpc_ref, newc_ref, hist_c, tm, i, tiles_per_seq, carry=not has_state)

        pos0 = start_pos + lax.rem(i, tiles_per_seq) * tm
        for r0 in range(0, tm, rb):
            rows = slice(r0, r0 + rb)
            for gi, win in enumerate(POOL_WINDOWS):
                cols = slice(gi * D_GROUP_DIM, (gi + 1) * D_GROUP_DIM)
                if stride == 1:
                    static_pos = None if r0 < POOL_BUF else POOL_BUF
                else:
                    static_pos = start_pos + r0 // stride
                pooled = _pool_block(xpd_ref, r0, gi, win, stride, rb, pos0, static_pos)
                diff = (pooled - xpd_ref[hd_rows + r0:hd_rows + r0 + rb, cols]).astype(_BF16)
                yd = _dot(diff, dproj_ref[gi]) * dscale_ref[:, cols]
                ycat_ref[rows, C_WIDTH + gi * D_GROUP_DIM:C_WIDTH + (gi + 1) * D_GROUP_DIM] = yd.astype(_BF16)
        _store_history(xpd_ref, newd_ref, hist_d, tm, i, tiles_per_seq, carry=not has_state)

        y = _dot(ycat_ref[...], wout_ref[...])
        out_ref[...] = x_ref[...] + y

    _tile_pipeline(x_hbm, out_hbm, xbuf, obuf, in_sem, out_sem, n_tiles, tile)


def _ffn_kernel(*refs, n_tiles, tiles_per_seq, stride, has_state, final_norm):
    refs = list(refs)
    x_hbm, g_ref, win_ref, wout_ref, cw_ref = refs[:5]
    del refs[:5]
    state_ref = refs.pop(0) if has_state else None
    gfin_ref = refs.pop(0) if final_norm else None
    out_hbm, newf_ref, xbuf, obuf, in_sem, out_sem, xn_ref, gp_ref, hist_ref, h_ref = refs
    tm = xbuf.shape[1]
    rb = ROW_BLOCK
    hist = (FFN_CONV - 1) * stride
    h_rows = _round_up(hist, SUBLANES)

    def tile(i, x_ref, out_ref):
        _load_history(hist_ref, state_ref, hist, i, tiles_per_seq)
        _norm_to_bf16(x_ref, g_ref, xn_ref, tm)
        xn = xn_ref[...]
        for j, c0 in enumerate(range(0, D_FF, FF_CHUNK)):
            cols = slice(c0, c0 + FF_CHUNK)
            slot = slice((j % 2) * FF_CHUNK, (j % 2 + 1) * FF_CHUNK)
            gp_ref[0:h_rows, slot] = hist_ref[:, cols]
            zg = _dot(xn, win_ref[:, c0:c0 + FF_CHUNK])
            zu = _dot(xn, win_ref[:, D_FF + c0:D_FF + c0 + FF_CHUNK])
            for r0 in range(0, tm, rb):
                rows = slice(r0, r0 + rb)
                gp_ref[h_rows + r0:h_rows + r0 + rb, slot] = zg[rows]
                conv = _conv_taps(gp_ref, cw_ref, r0, cols, FFN_CONV, stride, rb, xcols=slot)
                h_ref[rows, cols] = (jax.nn.gelu(conv) * zu[rows]).astype(_BF16)
            hist_ref[:, cols] = gp_ref[tm:tm + h_rows, slot]
        newf_ref[lax.div(i, tiles_per_seq)] = hist_ref[h_rows - hist:h_rows, :]

        y = x_ref[...] + _dot(h_ref[...], wout_ref[...])
        if final_norm:
            gfin = gfin_ref[...]
            out_ref[...] = y
            for r0 in range(0, tm, NORM_BLOCK):
                rows = slice(r0, r0 + NORM_BLOCK)
                out_ref[rows, :] = _rms_rows(out_ref[rows, :], gfin)
        else:
            out_ref[...] = y

    _tile_pipeline(x_hbm, out_hbm, xbuf, obuf, in_sem, out_sem, n_tiles, tile)


def _whole_spec(shape):
    nd = len(shape)
    return pl.BlockSpec(shape, lambda g: (0,) * nd, pipeline_mode=pl.Buffered(1))


def _layer_spec(arr, layer):
    nd = arr.ndim - 1
    return pl.BlockSpec((None,) + arr.shape[1:], lambda g: (layer,) + (0,) * nd, pipeline_mode=pl.Buffered(1))


def _smem_spec():
    return pl.BlockSpec(memory_space=pltpu.SMEM)


def _hbm_spec():
    return pl.BlockSpec(memory_space=pl.ANY)


def _out_vmem_spec(shape):
    nd = len(shape)
    return pl.BlockSpec(shape, lambda g: (0,) * nd)


def _params():
    return pltpu.CompilerParams(dimension_semantics=("arbitrary",), vmem_limit_bytes=VMEM_LIMIT)


def _pipeline_scratch(tm):
    return [pltpu.VMEM((2, tm, D_MODEL), _F32), pltpu.VMEM((2, tm, D_MODEL), _F32),
            pltpu.SemaphoreType.DMA((2,)), pltpu.SemaphoreType.DMA((2,))]


def _even_call(x3, tiles_per_seq, stride, state, layer, prm, name):
    has_state = state is not None
    i = layer // 2
    n_tiles, tm, _ = x3.shape
    nb = n_tiles // tiles_per_seq
    hist = (B_CONV - 1) * stride
    h_rows = _round_up(hist, SUBLANES)
    in_specs = [_hbm_spec(), _layer_spec(prm['norm_mix_g'], layer), _layer_spec(prm['w_in_even'], i),
                _layer_spec(prm['w_out_even'], i), _layer_spec(prm['a_ln_g'], i), _layer_spec(prm['b_conv_w'], i)]
    args = [x3, prm['norm_mix_g'], prm['w_in_even'], prm['w_out_even'], prm['a_ln_g'], prm['b_conv_w']]
    if has_state:
        in_specs += [_smem_spec(), _smem_spec(), _layer_spec(state, i)]
        args += [prm['a_ws_steps'], prm['a_bs_steps'], state]
    else:
        in_specs += [_layer_spec(prm['a_ws'], i), _layer_spec(prm['a_bs_t'], i)]
        args += [prm['a_ws'], prm['a_bs_t']]
    out_shape = [jax.ShapeDtypeStruct(x3.shape, _F32), jax.ShapeDtypeStruct((nb, hist, B_WIDTH), _F32)]
    out_specs = [_hbm_spec(), _out_vmem_spec((nb, hist, B_WIDTH))]
    if has_state:
        out_shape.append(jax.ShapeDtypeStruct((n_tiles, tm, A_WIDTH), _F32))
        out_specs.append(_out_vmem_spec((n_tiles, tm, A_WIDTH)))
    scratch = _pipeline_scratch(tm) + [
        pltpu.VMEM((tm, D_MODEL), _BF16), pltpu.VMEM((h_rows + tm, B_WIDTH), _F32),
        pltpu.VMEM((tm, A_WIDTH + B_WIDTH), _BF16), pltpu.VMEM((tm, A_WIDTH), _BF16)]
    return pl.pallas_call(
        functools.partial(_even_kernel, n_tiles=n_tiles, tiles_per_seq=tiles_per_seq, stride=stride,
                          has_state=has_state, gate_layer=i),
        grid=(1,), in_specs=in_specs, out_specs=out_specs, out_shape=out_shape,
        scratch_shapes=scratch, compiler_params=_params(), name=name)(*args)


def _odd_call(x3, tiles_per_seq, stride, state_c, state_d, start_pos, layer, prm, name):
    has_state = state_c is not None
    i = layer // 2
    n_tiles, tm, _ = x3.shape
    nb = n_tiles // tiles_per_seq
    hist_c = (C_CONV - 1) * stride
    hist_d = POOL_BUF * stride
    hc_rows = _round_up(hist_c, SUBLANES)
    hd_rows = _round_up(hist_d, SUBLANES)
    names = ['w_in_odd', 'w_out_odd', 'c_conv_w', 'c_conv_b', 'c_ln_g', 'c_ln_b', 'd_proj', 'd_scale']
    in_specs = [_hbm_spec(), _layer_spec(prm['norm_mix_g'], layer)] + [_layer_spec(prm[k], i) for k in names]
    args = [x3, prm['norm_mix_g']] + [prm[k] for k in names]
    if has_state:
        in_specs += [_layer_spec(state_c, i), _layer_spec(state_d, i)]
        args += [state_c, state_d]
    out_shape = [jax.ShapeDtypeStruct(x3.shape, _F32), jax.ShapeDtypeStruct((nb, hist_c, C_WIDTH), _F32),
                 jax.ShapeDtypeStruct((nb, hist_d, D_WIDTH), _F32)]
    out_specs = [_hbm_spec(), _out_vmem_spec((nb, hist_c, C_WIDTH)), _out_vmem_spec((nb, hist_d, D_WIDTH))]
    scratch = _pipeline_scratch(tm) + [
        pltpu.VMEM((tm, D_MODEL), _BF16), pltpu.VMEM((hc_rows + tm + SUBLANES, C_WIDTH), _F32),
        pltpu.VMEM((hd_rows + tm, D_WIDTH), _F32), pltpu.VMEM((tm, C_WIDTH + D_WIDTH), _BF16)]
    return pl.pallas_call(
        functools.partial(_odd_kernel, n_tiles=n_tiles, tiles_per_seq=tiles_per_seq, stride=stride,
                          has_state=has_state, start_pos=start_pos),
        grid=(1,), in_specs=in_specs, out_specs=out_specs, out_shape=out_shape,
        scratch_shapes=scratch, compiler_params=_params(), name=name)(*args)


def _ffn_call(x3, tiles_per_seq, stride, state, layer, final_norm, prm, name):
    has_state = state is not None
    n_tiles, tm, _ = x3.shape
    nb = n_tiles // tiles_per_seq
    hist = (FFN_CONV - 1) * stride
    h_rows = _round_up(hist, SUBLANES)
    names = ['norm_ffn_g', 'w_ffn_in', 'w_ffn_out', 'ffn_conv_w']
    in_specs = [_hbm_spec()] + [_layer_spec(prm[k], layer) for k in names]
    args = [x3] + [prm[k] for k in names]
    if has_state:
        in_specs.append(_layer_spec(state, layer))
        args.append(state)
    if final_norm:
        in_specs.append(_whole_spec((1, D_MODEL)))
        args.append(prm['norm_final_g'])
    out_shape = [jax.ShapeDtypeStruct(x3.shape, _F32), jax.ShapeDtypeStruct((nb, hist, D_FF), _F32)]
    out_specs = [_hbm_spec(), _out_vmem_spec((nb, hist, D_FF))]
    scratch = _pipeline_scratch(tm) + [
        pltpu.VMEM((tm, D_MODEL), _BF16), pltpu.VMEM((h_rows + tm, 2 * FF_CHUNK), _F32),
        pltpu.VMEM((h_rows, D_FF), _F32), pltpu.VMEM((tm, D_FF), _BF16)]
    return pl.pallas_call(
        functools.partial(_ffn_kernel, n_tiles=n_tiles, tiles_per_seq=tiles_per_seq, stride=stride,
                          has_state=has_state, final_norm=final_norm),
        grid=(1,), in_specs=in_specs, out_specs=out_specs, out_shape=out_shape,
        scratch_shapes=scratch, compiler_params=_params(), name=name)(*args)


def _to_tiles(a, ns):
    *lead, n, l, c = a.shape
    k = len(lead)
    a = a.reshape(*lead, n // ns, ns, l, c)
    a = a.transpose(*range(k), k, k + 2, k + 1, k + 3)
    return a.reshape(*lead, n // ns, l * ns, c)


def _from_tiles(a, ns, l):
    *lead, nb, _, c = a.shape
    k = len(lead)
    a = a.reshape(*lead, nb, l, ns, c)
    a = a.transpose(*range(k), k, k + 2, k + 1, k + 3)
    return a.reshape(*lead, nb * ns, l, c)


def _trunk(x3, tiles_per_seq, stride, states, start_pos, prm, tag):
    new_a, new_b, new_c, new_d, new_f = [], [], [], [], []
    depth = prm['w_ffn_in'].shape[0]
    for layer in range(depth):
        if layer % 2 == 0:
            st = None if states is None else states['b']
            res = _even_call(x3, tiles_per_seq, stride, st, layer, prm, name=f"{tag}_mix{layer}")
            x3 = res[0]
            new_b.append(res[1])
            if states is not None:
                new_a.append(res[2])
        else:
            st_c = None if states is None else states['c']
            st_d = None if states is None else states['d']
            x3, nc, nd = _odd_call(x3, tiles_per_seq, stride, st_c, st_d, start_pos, layer, prm,
                                   name=f"{tag}_mix{layer}")
            new_c.append(nc)
            new_d.append(nd)
        st_f = None if states is None else states['f']
        x3, nf = _ffn_call(x3, tiles_per_seq, stride, st_f, layer, layer == depth - 1, prm, name=f"{tag}_ffn{layer}")
        new_f.append(nf)
    return x3, new_a, new_b, new_c, new_d, new_f


def kernel(x_prompt, x_sample, state_b_conv, state_c_conv, state_d_pool, state_ffn_conv, norm_mix_g, norm_ffn_g, norm_final_g, w_in_even, w_out_even, a_ln_g, a_ws, a_bs, b_conv_w, w_in_odd, w_out_odd, c_conv_w, c_conv_b, c_ln_g, c_ln_b, d_proj, d_scale, w_ffn_in, ffn_conv_w, w_ffn_out):
    batch, seq, d = x_prompt.shape
    dec_batch, dec_seq, _ = x_sample.shape
    ns = TILE // dec_seq

    def rows(p):
        return p.reshape(p.shape[0], 1, p.shape[1])

    prm = dict(norm_mix_g=rows(norm_mix_g), norm_ffn_g=rows(norm_ffn_g), norm_final_g=norm_final_g.reshape(1, -1),
               w_in_even=w_in_even.astype(_BF16), w_out_even=w_out_even.astype(_BF16), a_ln_g=rows(a_ln_g),
               a_ws=a_ws, a_bs_t=jnp.swapaxes(a_bs, 1, 2),
               a_ws_steps=a_ws[:, :, :dec_seq, :dec_seq].reshape(-1), a_bs_steps=a_bs[:, :, :dec_seq].reshape(-1),
               b_conv_w=b_conv_w, w_in_odd=w_in_odd.astype(_BF16), w_out_odd=w_out_odd.astype(_BF16),
               c_conv_w=jnp.pad(c_conv_w, ((0, 0), (0, _round_up(C_CONV, SUBLANES) - C_CONV), (0, 0))),
               c_conv_b=rows(c_conv_b), c_ln_g=rows(c_ln_g), c_ln_b=rows(c_ln_b), d_proj=d_proj.astype(_BF16),
               d_scale=rows(d_scale), w_ffn_in=w_ffn_in.astype(_BF16), ffn_conv_w=ffn_conv_w,
               w_ffn_out=w_ffn_out.astype(_BF16))

    yp, _, b_p, c_p, d_p, f_p = _trunk(x_prompt.reshape(batch * seq // TILE, TILE, d), seq // TILE, 1, None, 0,
                                       prm, "p")

    states = dict(b=_to_tiles(state_b_conv, ns), c=_to_tiles(state_c_conv, ns), d=_to_tiles(state_d_pool, ns),
                  f=_to_tiles(state_ffn_conv, ns))
    ys, a_s, b_s, c_s, d_s, f_s = _trunk(_to_tiles(x_sample, ns), 1, ns, states, PAST_LEN, prm, "s")

    def untile(parts, l):
        return _from_tiles(jnp.stack(parts), ns, l)

    y_prompt = yp.reshape(batch, seq, d)
    y_sample = _from_tiles(ys, ns, dec_seq)
    new_a = untile(a_s, dec_seq)
    return (y_prompt, y_sample, new_a, jnp.stack(b_p), untile(b_s, B_CONV - 1), jnp.stack(c_p),
            untile(c_s, C_CONV - 1), jnp.stack(d_p), untile(d_s, POOL_BUF), jnp.stack(f_p),
            untile(f_s, FFN_CONV - 1))
```

```python
import functools

import jax
import jax.numpy as jnp
from jax import lax
from jax.experimental import pallas as pl
from jax.experimental.pallas import tpu as pltpu

D_MODEL = 1024
CHUNK = 128
A_HEADS = 4
A_WIDTH = 512
B_WIDTH = 512
B_CONV = 3
C_WIDTH = 512
C_CONV = 31
D_WIDTH = 512
POOL_WINDOWS = (2, 4, 8, 16)
D_GROUP_DIM = 128
POOL_BUF = 15
D_FF = 2816
FFN_CONV = 3
PAST_LEN = 16384
EPS = 1e-6

SUBLANES = 8
LANES = 128
BF16_ROWS = 16
PROMPT_TILE = 1024
SAMPLE_TILE = 512
SUB_TILE = 512
ROW_BLOCK = 64
NORM_BLOCK = 32
FF_CHUNK = 256
VMEM_LIMIT = 56 * 1024 * 1024

_BF16 = jnp.bfloat16
_F32 = jnp.float32


def _round_up(n, m):
    return (n + m - 1) // m * m


def _dot(a, b):
    return jnp.dot(a, b, preferred_element_type=_F32)


def _rms_rows(x, g):
    y = x * lax.rsqrt(jnp.mean(x * x, axis=-1, keepdims=True) + EPS)
    return y * g


def _ln_rows(x, g, b=None):
    mu = jnp.mean(x, axis=-1, keepdims=True)
    xc = x - mu
    y = xc * lax.rsqrt(jnp.mean(xc * xc, axis=-1, keepdims=True) + EPS) * g
    if b is not None:
        y = y + b
    return y


def _norm_to_bf16(x_ref, g_ref, xn_ref, s0, sub):
    g = g_ref[...]
    for r0 in range(s0, s0 + sub, NORM_BLOCK):
        rows = slice(r0, r0 + NORM_BLOCK)
        xn_ref[rows, :] = _rms_rows(x_ref[rows, :], g).astype(_BF16)


def _init_history(xp_ref, state_ref, hist, first_tile):
    h_rows = _round_up(hist, SUBLANES)
    if state_ref is None:
        @pl.when(first_tile)
        def _():
            xp_ref[0:h_rows, :] = jnp.zeros((h_rows, xp_ref.shape[1]), _F32)
    else:
        xp_ref[h_rows - hist:h_rows, :] = state_ref[0]


def _emit_history(xp_ref, new_ref, hist, tm, carry):
    h_rows = _round_up(hist, SUBLANES)
    tail = xp_ref[h_rows + tm - hist:h_rows + tm, :]
    new_ref[0] = tail
    if carry:
        xp_ref[h_rows - hist:h_rows, :] = tail


def _conv_taps(xp_ref, w_ref, r0, cols, ktaps, stride, rb, xcols=None):
    xcols = cols if xcols is None else xcols
    hist = (ktaps - 1) * stride
    base = _round_up(hist, SUBLANES) - hist + r0
    acc = None
    for k in range(ktaps):
        term = w_ref[k:k + 1, cols] * xp_ref[base + k * stride:base + k * stride + rb, xcols]
        acc = term if acc is None else acc + term
    return acc


def _conv_long_stride1(xp_ref, w_ref, r0, cols, ktaps, rb):
    hist = ktaps - 1
    base = _round_up(hist, SUBLANES) - hist
    n = rb + SUBLANES
    p = None
    for r in range(SUBLANES - 1, -1, -1):
        z = None
        for k in range(ktaps):
            if (base + k) % SUBLANES != r:
                continue
            off = r0 + (base + k) - r
            term = w_ref[k:k + 1, cols] * xp_ref[off:off + n, cols]
            z = term if z is None else z + term
        if p is not None:
            p = pltpu.roll(p, n - 1, 0)
            p = p if z is None else p + z
        else:
            p = z
    return p[0:rb]


def _cast_weights_step(step, n_steps, layers, srcs, dsts, stage_in, stage_out, sem_in, sem_out):
    n_jobs = len(layers)

    def in_copy(j, s, slot):
        rc = stage_in[j].shape[1]
        return pltpu.make_async_copy(srcs[j].at[layers[j], pl.ds(s * rc, rc), :], stage_in[j].at[slot],
                                     sem_in.at[j, slot])

    def out_copy(j, s, slot):
        rc = stage_out[j].shape[1]
        return pltpu.make_async_copy(stage_out[j].at[slot], dsts[j].at[pl.ds(s * rc, rc), :], sem_out.at[j, slot])

    slot = lax.rem(step, 2)

    @pl.when(step == 0)
    def _():
        for j in range(n_jobs):
            in_copy(j, 0, 0).start()

    for j in range(n_jobs):
        in_copy(j, step, slot).wait()

    @pl.when(step + 1 < n_steps)
    def _():
        for j in range(n_jobs):
            in_copy(j, step + 1, 1 - slot).start()

    @pl.when(step >= 2)
    def _():
        for j in range(n_jobs):
            out_copy(j, step - 2, slot).wait()

    for j in range(n_jobs):
        src = stage_in[j].at[slot]
        dst = stage_out[j].at[slot]
        for r0 in range(0, src.shape[0], BF16_ROWS):
            dst[r0:r0 + BF16_ROWS, :] = src[r0:r0 + BF16_ROWS, :].astype(_BF16)
        out_copy(j, step, slot).start()

    def finish():
        @pl.when(step == n_steps - 1)
        def _():
            for j in range(n_jobs):
                if n_steps > 1:
                    out_copy(j, step - 1, 1 - slot).wait()
                out_copy(j, step, slot).wait()

    return finish


def _split_refs(refs, n_in, n_out, n_scratch, n_jobs):
    refs = list(refs)
    a = n_in
    b = a + n_jobs
    c = b + n_out
    d = c + n_jobs
    e = d + n_scratch
    return refs[:a], refs[a:b], refs[b:c], refs[c:d], refs[d:e], refs[e:]


def _maybe_cast_weights(cast_layers, grid_shape, srcs, dsts, cast_scratch):
    n_jobs = len(cast_layers)
    if n_jobs == 0:
        return lambda: None
    stage_in = cast_scratch[:n_jobs]
    stage_out = cast_scratch[n_jobs:2 * n_jobs]
    sem_in, sem_out = cast_scratch[2 * n_jobs:]
    nb, nt = grid_shape
    step = pl.program_id(0) * nt + pl.program_id(1)
    return _cast_weights_step(step, nb * nt, cast_layers, srcs, dsts, stage_in, stage_out, sem_in, sem_out)


def _even_kernel(*refs, tm, stride, has_state, gate_layer, cast_layers, grid_shape):
    ins, srcs, outs, dsts, scr, cast_scr = _split_refs(
        refs, 9 if has_state else 8, 3 if has_state else 2, 4, len(cast_layers))
    x_ref, g_ref, win_ref, wout_ref, lng_ref, cw_ref, gw_ref, gb_ref = ins[:8]
    state_ref = ins[8] if has_state else None
    out_ref, newb_ref = outs[:2]
    v_ref = outs[2] if has_state else None
    xn_ref, xp_ref, ycat_ref, vb_ref = scr
    finish_cast = _maybe_cast_weights(cast_layers, grid_shape, srcs, dsts, cast_scr)

    rb = ROW_BLOCK
    hist = (B_CONV - 1) * stride
    h_rows = _round_up(hist, SUBLANES)
    _init_history(xp_ref, state_ref, hist, pl.program_id(1) == 0)
    lng = lng_ref[...]
    allc = slice(0, B_WIDTH)
    if stride == 1:
        row_i = lax.broadcasted_iota(jnp.int32, (CHUNK, CHUNK), 0)
        col_i = lax.broadcasted_iota(jnp.int32, (CHUNK, CHUNK), 1)
        wts = [jnp.where(col_i <= row_i, gw_ref[h], 0.0).astype(_BF16) for h in range(A_HEADS)]

    sub = min(tm, SUB_TILE)
    for s0 in range(0, tm, sub):
        srows = slice(s0, s0 + sub)
        _norm_to_bf16(x_ref, g_ref, xn_ref, s0, sub)
        xn = xn_ref[srows, :]

        zbg = _dot(xn, win_ref[:, 2 * A_WIDTH:2 * A_WIDTH + B_WIDTH])
        zcg = _dot(xn, win_ref[:, 2 * A_WIDTH + B_WIDTH:2 * A_WIDTH + 2 * B_WIDTH])
        zh = _dot(xn, win_ref[:, 2 * A_WIDTH + 2 * B_WIDTH:2 * A_WIDTH + 3 * B_WIDTH])
        for r0 in range(s0, s0 + sub, rb):
            rows = slice(r0, r0 + rb)
            loc = slice(r0 - s0, r0 - s0 + rb)
            xp_ref[h_rows + r0:h_rows + r0 + rb, :] = zcg[loc] * zh[loc]
            conv = _conv_taps(xp_ref, cw_ref, r0, allc, B_CONV, stride, rb)
            ycat_ref[rows, A_WIDTH:A_WIDTH + B_WIDTH] = (zbg[loc] * conv).astype(_BF16)

        zu = _dot(xn, win_ref[:, 0:A_WIDTH])
        zv = _dot(xn, win_ref[:, A_WIDTH:2 * A_WIDTH])
        for r0 in range(s0, s0 + sub, NORM_BLOCK):
            rows = slice(r0, r0 + NORM_BLOCK)
            v = _ln_rows(jax.nn.gelu(zv[r0 - s0:r0 - s0 + NORM_BLOCK]), lng)
            if v_ref is not None:
                v_ref[rows, :] = v
            else:
                vb_ref[rows, :] = v.astype(_BF16)
        if stride == 1:
            for c0 in range(s0, s0 + sub, 2 * CHUNK):
                ra = slice(c0, c0 + CHUNK)
                rb2 = slice(c0 + CHUNK, c0 + 2 * CHUNK)
                for h in range(A_HEADS):
                    cols = slice(h * CHUNK, (h + 1) * CHUNK)
                    vpair = jnp.concatenate([vb_ref[ra, cols], vb_ref[rb2, cols]], axis=1)
                    mixed = _dot(wts[h], vpair)
                    bias = gb_ref[:, h:h + 1]
                    for half, rws in enumerate((ra, rb2)):
                        m = mixed[:, half * CHUNK:(half + 1) * CHUNK] + bias
                        u = zu[rws.start - s0:rws.stop - s0, cols]
                        ycat_ref[rws, cols] = (jax.nn.gelu(u) * m).astype(_BF16)
        else:
            steps = tm // stride
            gw_off = gate_layer * A_HEADS * steps * steps
            gb_off = gate_layer * A_HEADS * steps
            for t in range(s0 // stride, (s0 + sub) // stride):
                for n0 in range(0, stride, rb):
                    rows = slice(t * stride + n0, t * stride + n0 + rb)
                    for h in range(A_HEADS):
                        cols = slice(h * CHUNK, (h + 1) * CHUNK)
                        mixed = None
                        for s in range(t + 1):
                            w = gw_ref[gw_off + (h * steps + t) * steps + s]
                            term = w * v_ref[s * stride + n0:s * stride + n0 + rb, cols]
                            mixed = term if mixed is None else mixed + term
                        mixed = mixed + gb_ref[gb_off + h * steps + t]
                        u = zu[rows.start - s0:rows.stop - s0, cols]
                        ycat_ref[rows, cols] = (jax.nn.gelu(u) * mixed).astype(_BF16)

        y = _dot(ycat_ref[srows, :], wout_ref[...])
        out_ref[srows, :] = x_ref[srows, :] + y
    _emit_history(xp_ref, newb_ref, hist, tm, carry=not has_state)
    finish_cast()


def _pool_block(xp_ref, r0, gi, win, stride, rb, pos0, row_pos_static):
    cols = slice(gi * D_GROUP_DIM, (gi + 1) * D_GROUP_DIM)
    hist = POOL_BUF * stride
    h_rows = _round_up(hist, SUBLANES)
    if stride % SUBLANES == 0:
        s = None
        for j in range(win):
            off = h_rows + r0 - j * stride
            term = xp_ref[off:off + rb, cols]
            s = term if s is None else s + term
    else:
        halo = 2 * SUBLANES
        ext = xp_ref[h_rows + r0 - halo:h_rows + r0 + rb, cols]
        shift = 1
        while shift < win:
            ext = ext + pltpu.roll(ext, shift, 0)
            shift *= 2
        s = ext[halo:]
    if row_pos_static is not None:
        return s / float(min(row_pos_static + 1, win))
    pos = pos0 + r0 + lax.broadcasted_iota(jnp.int32, (rb, D_GROUP_DIM), 0)
    cnt = jnp.minimum(pos + 1, win).astype(_F32)
    return s / cnt


def _odd_kernel(*refs, tm, stride, has_state, start_pos, cast_layers, grid_shape):
    ins, srcs, outs, dsts, scr, cast_scr = _split_refs(refs, 12 if has_state else 10, 3, 4, len(cast_layers))
    x_ref, g_ref, win_ref, wout_ref, ccw_ref, ccb_ref, clg_ref, clb_ref, dproj_ref, dscale_ref = ins[:10]
    statec_ref, stated_ref = (ins[10], ins[11]) if has_state else (None, None)
    out_ref, newc_ref, newd_ref = outs
    xn_ref, xpc_ref, xpd_ref, ycat_ref = scr
    finish_cast = _maybe_cast_weights(cast_layers, grid_shape, srcs, dsts, cast_scr)

    rb = ROW_BLOCK
    hist_c = (C_CONV - 1) * stride
    hc_rows = _round_up(hist_c, SUBLANES)
    hist_d = POOL_BUF * stride
    hd_rows = _round_up(hist_d, SUBLANES)
    first = pl.program_id(1) == 0
    _init_history(xpc_ref, statec_ref, hist_c, first)
    _init_history(xpd_ref, stated_ref, hist_d, first)
    ccb = ccb_ref[...]
    clg = clg_ref[...]
    clb = clb_ref[...]
    pos0 = start_pos + pl.program_id(1) * tm

    sub = min(tm, SUB_TILE)
    for s0 in range(0, tm, sub):
        srows = slice(s0, s0 + sub)
        _norm_to_bf16(x_ref, g_ref, xn_ref, s0, sub)
        xn = xn_ref[srows, :]

        zca = _dot(xn, win_ref[:, 0:C_WIDTH])
        zcb = _dot(xn, win_ref[:, C_WIDTH:2 * C_WIDTH])
        zp = _dot(xn, win_ref[:, 2 * C_WIDTH:2 * C_WIDTH + D_WIDTH])
        for r0 in range(s0, s0 + sub, rb):
            loc = slice(r0 - s0, r0 - s0 + rb)
            xpc_ref[hc_rows + r0:hc_rows + r0 + rb, :] = zca[loc] * jax.nn.sigmoid(zcb[loc])
            xpd_ref[hd_rows + r0:hd_rows + r0 + rb, :] = zp[loc]
        xpc_ref[hc_rows + s0 + sub:hc_rows + s0 + sub + SUBLANES, :] = jnp.zeros((SUBLANES, C_WIDTH), _F32)

        for r0 in range(s0, s0 + sub, rb):
            rows = slice(r0, r0 + rb)
            parts = []
            for c0 in range(0, C_WIDTH, LANES):
                cols = slice(c0, c0 + LANES)
                if stride % SUBLANES == 0:
                    parts.append(_conv_taps(xpc_ref, ccw_ref, r0, cols, C_CONV, stride, rb))
                else:
                    parts.append(_conv_long_stride1(xpc_ref, ccw_ref, r0, cols, C_CONV, rb))
            cc = jnp.concatenate(parts, axis=-1) + ccb
            ycat_ref[rows, 0:C_WIDTH] = jax.nn.silu(_ln_rows(cc, clg, clb)).astype(_BF16)

        for r0 in range(s0, s0 + sub, rb):
            rows = slice(r0, r0 + rb)
            for gi, win in enumerate(POOL_WINDOWS):
                cols = slice(gi * D_GROUP_DIM, (gi + 1) * D_GROUP_DIM)
                if stride == 1:
                    static_pos = None if r0 < POOL_BUF else POOL_BUF
                else:
                    static_pos = start_pos + r0 // stride
                pooled = _pool_block(xpd_ref, r0, gi, win, stride, rb, pos0, static_pos)
                diff = (pooled - xpd_ref[hd_rows + r0:hd_rows + r0 + rb, cols]).astype(_BF16)
                yd = _dot(diff, dproj_ref[gi]) * dscale_ref[:, cols]
                ycat_ref[rows, C_WIDTH + gi * D_GROUP_DIM:C_WIDTH + (gi + 1) * D_GROUP_DIM] = yd.astype(_BF16)

        y = _dot(ycat_ref[srows, :], wout_ref[...])
        out_ref[srows, :] = x_ref[srows, :] + y
    _emit_history(xpc_ref, newc_ref, hist_c, tm, carry=not has_state)
    _emit_history(xpd_ref, newd_ref, hist_d, tm, carry=not has_state)
    finish_cast()


def _ffn_kernel(*refs, stride, has_state, final_norm):
    refs = list(refs)
    x_ref, g_ref, win_ref, wout_ref, cw_ref = refs[:5]
    del refs[:5]
    state_ref = refs.pop(0) if has_state else None
    gfin_ref = refs.pop(0) if final_norm else None
    out_ref, newf_ref, xn_ref, gp_ref, hist_ref, h_ref = refs
    rb = ROW_BLOCK
    hist = (FFN_CONV - 1) * stride
    h_rows = _round_up(hist, SUBLANES)
    _init_history(hist_ref, state_ref, hist, pl.program_id(1) == 0)
    n_sub, sub, _ = x_ref.shape

    def sub_tile(i, carry):
        xs = x_ref.at[i]
        os = out_ref.at[i]
        _norm_to_bf16(xs, g_ref, xn_ref, 0, sub)
        xn = xn_ref[...]
        for j, c0 in enumerate(range(0, D_FF, FF_CHUNK)):
            cols = slice(c0, c0 + FF_CHUNK)
            slot = slice((j % 2) * FF_CHUNK, (j % 2 + 1) * FF_CHUNK)
            gp_ref[0:h_rows, slot] = hist_ref[:, cols]
            zg = _dot(xn, win_ref[:, c0:c0 + FF_CHUNK])
            zu = _dot(xn, win_ref[:, D_FF + c0:D_FF + c0 + FF_CHUNK])
            for r0 in range(0, sub, rb):
                rows = slice(r0, r0 + rb)
                gp_ref[h_rows + r0:h_rows + r0 + rb, slot] = zg[rows]
                conv = _conv_taps(gp_ref, cw_ref, r0, cols, FFN_CONV, stride, rb, xcols=slot)
                h_ref[rows, cols] = (jax.nn.gelu(conv) * zu[rows]).astype(_BF16)
            hist_ref[:, cols] = gp_ref[sub:sub + h_rows, slot]

        y = xs[...] + _dot(h_ref[...], wout_ref[...])
        if final_norm:
            gfin = gfin_ref[...]
            os[...] = y
            for r0 in range(0, sub, NORM_BLOCK):
                rows = slice(r0, r0 + NORM_BLOCK)
                os[rows, :] = _rms_rows(os[rows, :], gfin)
        else:
            os[...] = y
        return carry

    lax.fori_loop(0, n_sub, sub_tile, 0)
    newf_ref[0] = hist_ref[h_rows - hist:h_rows, :]


def _const_spec(shape):
    nd = len(shape)
    return pl.BlockSpec(shape, lambda b, t: (0,) * nd, pipeline_mode=pl.Buffered(1))


def _layer_spec(arr, layer):
    nd = arr.ndim - 1
    return pl.BlockSpec((None,) + arr.shape[1:], lambda b, t: (layer,) + (0,) * nd, pipeline_mode=pl.Buffered(1))


def _smem_spec():
    return pl.BlockSpec(memory_space=pltpu.SMEM)


def _hbm_spec():
    return pl.BlockSpec(memory_space=pl.ANY)


def _tile_spec(tm, nt, width):
    return pl.BlockSpec((tm, width), lambda b, t: (b * nt + t, 0))


def _state_spec(layer, rows, width):
    return pl.BlockSpec((None, 1, rows, width), lambda b, t: (layer, b, 0, 0))


def _new_state_spec(rows, width):
    return pl.BlockSpec((1, rows, width), lambda b, t: (b, 0, 0))


def _params():
    return pltpu.CompilerParams(dimension_semantics=("arbitrary", "arbitrary"), vmem_limit_bytes=VMEM_LIMIT)


def _cast_plumbing(cast_jobs, n_steps):
    in_specs, args, out_specs, out_shape, stage_in, stage_out = [], [], [], [], [], []
    for arr, _ in cast_jobs:
        _, r, c = arr.shape
        rc = r // n_steps
        assert rc * n_steps == r and rc % BF16_ROWS == 0, (arr.shape, n_steps)
        in_specs.append(_hbm_spec())
        args.append(arr)
        out_specs.append(_hbm_spec())
        out_shape.append(jax.ShapeDtypeStruct((r, c), _BF16))
        stage_in.append(pltpu.VMEM((2, rc, c), _F32))
        stage_out.append(pltpu.VMEM((2, rc, c), _BF16))
    scratch = stage_in + stage_out
    if cast_jobs:
        n = len(cast_jobs)
        scratch += [pltpu.SemaphoreType.DMA((n, 2)), pltpu.SemaphoreType.DMA((n, 2))]
    return in_specs, args, out_specs, out_shape, scratch, tuple(layer for _, layer in cast_jobs)


def _even_call(x2, nb, nt, stride, state, layer, win, wout, prm, cast_jobs, name):
    has_state = state is not None
    i = layer // 2
    tm = x2.shape[0] // (nb * nt)
    hist = (B_CONV - 1) * stride
    h_rows = _round_up(hist, SUBLANES)
    c_in, c_args, c_out_specs, c_out_shape, c_scratch, cast_layers = _cast_plumbing(cast_jobs, nb * nt)
    in_specs = [_tile_spec(tm, nt, D_MODEL), _layer_spec(prm['norm_mix_g'], layer), _const_spec(win.shape),
                _const_spec(wout.shape), _layer_spec(prm['a_ln_g'], i), _layer_spec(prm['b_conv_w'], i)]
    args = [x2, prm['norm_mix_g'], win, wout, prm['a_ln_g'], prm['b_conv_w']]
    if has_state:
        in_specs += [_smem_spec(), _smem_spec(), _state_spec(i, hist, B_WIDTH)]
        args += [prm['a_ws_steps'], prm['a_bs_steps'], state]
    else:
        in_specs += [_layer_spec(prm['a_ws'], i), _layer_spec(prm['a_bs_t'], i)]
        args += [prm['a_ws'], prm['a_bs_t']]
    out_shape = [jax.ShapeDtypeStruct(x2.shape, _F32), jax.ShapeDtypeStruct((nb, hist, B_WIDTH), _F32)]
    out_specs = [_tile_spec(tm, nt, D_MODEL), _new_state_spec(hist, B_WIDTH)]
    if has_state:
        out_shape.append(jax.ShapeDtypeStruct((x2.shape[0], A_WIDTH), _F32))
        out_specs.append(_tile_spec(tm, nt, A_WIDTH))
    scratch = [pltpu.VMEM((tm, D_MODEL), _BF16), pltpu.VMEM((h_rows + tm, B_WIDTH), _F32),
               pltpu.VMEM((tm, A_WIDTH + B_WIDTH), _BF16), pltpu.VMEM((tm, A_WIDTH), _BF16)]
    return pl.pallas_call(
        functools.partial(_even_kernel, tm=tm, stride=stride, has_state=has_state, gate_layer=i,
                          cast_layers=cast_layers, grid_shape=(nb, nt)),
        grid=(nb, nt), in_specs=in_specs + c_in, out_specs=out_specs + c_out_specs,
        out_shape=out_shape + c_out_shape, scratch_shapes=scratch + c_scratch,
        compiler_params=_params(), name=name)(*args, *c_args)


def _odd_call(x2, nb, nt, stride, state_c, state_d, start_pos, layer, win, wout, prm, cast_jobs, name):
    has_state = state_c is not None
    i = layer // 2
    hist_c = (C_CONV - 1) * stride
    hist_d = POOL_BUF * stride
    hc_rows = _round_up(hist_c, SUBLANES)
    hd_rows = _round_up(hist_d, SUBLANES)
    tm = x2.shape[0] // (nb * nt)
    c_in, c_args, c_out_specs, c_out_shape, c_scratch, cast_layers = _cast_plumbing(cast_jobs, nb * nt)
    names = ['c_conv_w', 'c_conv_b', 'c_ln_g', 'c_ln_b', 'd_proj', 'd_scale']
    in_specs = [_tile_spec(tm, nt, D_MODEL), _layer_spec(prm['norm_mix_g'], layer), _const_spec(win.shape),
                _const_spec(wout.shape)] + [_layer_spec(prm[k], i) for k in names]
    args = [x2, prm['norm_mix_g'], win, wout] + [prm[k] for k in names]
    if has_state:
        in_specs += [_state_spec(i, hist_c, C_WIDTH), _state_spec(i, hist_d, D_WIDTH)]
        args += [state_c, state_d]
    out_shape = [jax.ShapeDtypeStruct(x2.shape, _F32), jax.ShapeDtypeStruct((nb, hist_c, C_WIDTH), _F32),
                 jax.ShapeDtypeStruct((nb, hist_d, D_WIDTH), _F32)]
    out_specs = [_tile_spec(tm, nt, D_MODEL), _new_state_spec(hist_c, C_WIDTH), _new_state_spec(hist_d, D_WIDTH)]
    scratch = [pltpu.VMEM((tm, D_MODEL), _BF16), pltpu.VMEM((hc_rows + tm + SUBLANES, C_WIDTH), _F32),
               pltpu.VMEM((hd_rows + tm, D_WIDTH), _F32), pltpu.VMEM((tm, C_WIDTH + D_WIDTH), _BF16)]
    return pl.pallas_call(
        functools.partial(_odd_kernel, tm=tm, stride=stride, has_state=has_state, start_pos=start_pos,
                          cast_layers=cast_layers, grid_shape=(nb, nt)),
        grid=(nb, nt), in_specs=in_specs + c_in, out_specs=out_specs + c_out_specs,
        out_shape=out_shape + c_out_shape, scratch_shapes=scratch + c_scratch,
        compiler_params=_params(), name=name)(*args, *c_args)


def _ffn_call(x2, nb, nt, stride, state, layer, final_norm, win, wout, prm, name):
    has_state = state is not None
    hist = (FFN_CONV - 1) * stride
    h_rows = _round_up(hist, SUBLANES)
    tm = x2.shape[0] // (nb * nt)
    sub = min(tm, SUB_TILE)
    n_sub = tm // sub
    x3 = x2.reshape(-1, sub, D_MODEL)
    x_spec = pl.BlockSpec((n_sub, sub, D_MODEL), lambda b, t: (b * nt + t, 0, 0))
    in_specs = [x_spec, _layer_spec(prm['norm_ffn_g'], layer), _const_spec(win.shape), _const_spec(wout.shape),
                _layer_spec(prm['ffn_conv_w'], layer)]
    args = [x3, prm['norm_ffn_g'], win, wout, prm['ffn_conv_w']]
    if has_state:
        in_specs.append(_state_spec(layer, hist, D_FF))
        args.append(state)
    if final_norm:
        in_specs.append(_const_spec((1, D_MODEL)))
        args.append(prm['norm_final_g'])
    out_shape = [jax.ShapeDtypeStruct(x3.shape, _F32), jax.ShapeDtypeStruct((nb, hist, D_FF), _F32)]
    out_specs = [x_spec, _new_state_spec(hist, D_FF)]
    scratch = [pltpu.VMEM((sub, D_MODEL), _BF16), pltpu.VMEM((h_rows + sub, 2 * FF_CHUNK), _F32),
               pltpu.VMEM((h_rows, D_FF), _F32), pltpu.VMEM((sub, D_FF), _BF16)]
    y3, newf = pl.pallas_call(
        functools.partial(_ffn_kernel, stride=stride, has_state=has_state, final_norm=final_norm),
        grid=(nb, nt), in_specs=in_specs, out_specs=out_specs, out_shape=out_shape,
        scratch_shapes=scratch, compiler_params=_params(), name=name)(*args)
    return y3.reshape(x2.shape), newf


def _to_tiles(a, ns):
    *lead, n, l, c = a.shape
    k = len(lead)
    a = a.reshape(*lead, n // ns, ns, l, c)
    a = a.transpose(*range(k), k, k + 2, k + 1, k + 3)
    return a.reshape(*lead, n // ns, l * ns, c)


def _from_tiles(a, ns, l):
    *lead, nb, _, c = a.shape
    k = len(lead)
    a = a.reshape(*lead, nb, l, ns, c)
    a = a.transpose(*range(k), k, k + 2, k + 1, k + 3)
    return a.reshape(*lead, nb * ns, l, c)


def _trunk(x2, nb, nt, stride, states, start_pos, prm, wb, raw, tag):
    new_a, new_b, new_c, new_d, new_f = [], [], [], [], []
    depth = prm['ffn_conv_w'].shape[0]
    for layer in range(depth):
        i = layer // 2
        cast_keys, cast_jobs = [], []
        if raw is not None:
            cast_keys.append(('ffn', layer))
            cast_jobs += [(raw['w_ffn_in'], layer), (raw['w_ffn_out'], layer)]
            if layer + 1 < depth:
                nxt = 'even' if (layer + 1) % 2 == 0 else 'odd'
                cast_keys.append(('mix', layer + 1))
                cast_jobs += [(raw['w_in_' + nxt], (layer + 1) // 2), (raw['w_out_' + nxt], (layer + 1) // 2)]
        win, wout = wb[('mix', layer)]
        if layer % 2 == 0:
            st = None if states is None else states['b']
            res = _even_call(x2, nb, nt, stride, st, layer, win, wout, prm, cast_jobs, name=f"{tag}_mix{layer}")
            n_own = 2 if states is None else 3
            x2 = res[0]
            new_b.append(res[1])
            if states is not None:
                new_a.append(res[2])
        else:
            st_c = None if states is None else states['c']
            st_d = None if states is None else states['d']
            res = _odd_call(x2, nb, nt, stride, st_c, st_d, start_pos, layer, win, wout, prm, cast_jobs,
                            name=f"{tag}_mix{layer}")
            n_own = 3
            x2 = res[0]
            new_c.append(res[1])
            new_d.append(res[2])
        for k, key in enumerate(cast_keys):
            wb[key] = (res[n_own + 2 * k], res[n_own + 2 * k + 1])
        st_f = None if states is None else states['f']
        win, wout = wb[('ffn', layer)]
        x2, nf = _ffn_call(x2, nb, nt, stride, st_f, layer, layer == depth - 1, win, wout, prm,
                           name=f"{tag}_ffn{layer}")
        new_f.append(nf)
    return x2, new_a, new_b, new_c, new_d, new_f


def kernel(x_prompt, x_sample, state_b_conv, state_c_conv, state_d_pool, state_ffn_conv, norm_mix_g, norm_ffn_g, norm_final_g, w_in_even, w_out_even, a_ln_g, a_ws, a_bs, b_conv_w, w_in_odd, w_out_odd, c_conv_w, c_conv_b, c_ln_g, c_ln_b, d_proj, d_scale, w_ffn_in, ffn_conv_w, w_ffn_out):
    batch, seq, d = x_prompt.shape
    dec_batch, dec_seq, _ = x_sample.shape
    ns = SAMPLE_TILE // dec_seq

    def rows(p):
        return p.reshape(p.shape[0], 1, p.shape[1])

    prm = dict(norm_mix_g=rows(norm_mix_g), norm_ffn_g=rows(norm_ffn_g), norm_final_g=norm_final_g.reshape(1, -1),
               a_ln_g=rows(a_ln_g), a_ws=a_ws, a_bs_t=jnp.swapaxes(a_bs, 1, 2),
               a_ws_steps=a_ws[:, :, :dec_seq, :dec_seq].reshape(-1), a_bs_steps=a_bs[:, :, :dec_seq].reshape(-1),
               b_conv_w=b_conv_w,
               c_conv_w=jnp.pad(c_conv_w, ((0, 0), (0, _round_up(C_CONV, SUBLANES) - C_CONV), (0, 0))),
               c_conv_b=rows(c_conv_b), c_ln_g=rows(c_ln_g), c_ln_b=rows(c_ln_b), d_proj=d_proj.astype(_BF16),
               d_scale=rows(d_scale), ffn_conv_w=ffn_conv_w)
    raw = dict(w_in_even=w_in_even, w_out_even=w_out_even, w_in_odd=w_in_odd, w_out_odd=w_out_odd,
               w_ffn_in=w_ffn_in, w_ffn_out=w_ffn_out)
    wb = {('mix', 0): (w_in_even[0].astype(_BF16), w_out_even[0].astype(_BF16))}

    yp, _, b_p, c_p, d_p, f_p = _trunk(x_prompt.reshape(batch * seq, d), batch, seq // PROMPT_TILE, 1, None, 0,
                                       prm, wb, raw, "p")

    states = dict(b=_to_tiles(state_b_conv, ns), c=_to_tiles(state_c_conv, ns), d=_to_tiles(state_d_pool, ns),
                  f=_to_tiles(state_ffn_conv, ns))
    xs = _to_tiles(x_sample, ns).reshape(dec_batch * dec_seq, d)
    ys, a_s, b_s, c_s, d_s, f_s = _trunk(xs, dec_batch // ns, 1, ns, states, PAST_LEN, prm, wb, None, "s")

    def untile(parts, l):
        return _from_tiles(jnp.stack(parts), ns, l)

    y_prompt = yp.reshape(batch, seq, d)
    y_sample = _from_tiles(ys.reshape(dec_batch // ns, dec_seq * ns, d), ns, dec_seq)
    new_a = untile([a.reshape(dec_batch // ns, dec_seq * ns, A_WIDTH) for a in a_s], dec_seq)
    return (y_prompt, y_sample, new_a, jnp.stack(b_p), untile(b_s, B_CONV - 1), jnp.stack(c_p),
            untile(c_s, C_CONV - 1), jnp.stack(d_p), untile(d_s, POOL_BUF), jnp.stack(f_p),
            untile(f_s, FFN_CONV - 1))
```

```python
import functools

import jax
import jax.numpy as jnp
from jax import lax
from jax.experimental import pallas as pl
from jax.experimental.pallas import tpu as pltpu

D_MODEL = 1024
CHUNK = 128
A_HEADS = 4
A_WIDTH = 512
B_WIDTH = 512
B_CONV = 3
C_WIDTH = 512
C_CONV = 31
D_WIDTH = 512
POOL_WINDOWS = (2, 4, 8, 16)
D_GROUP_DIM = 128
POOL_BUF = 15
D_FF = 2816
FFN_CONV = 3
PAST_LEN = 16384
EPS = 1e-6

SUBLANES = 8
LANES = 128
BF16_ROWS = 16
PROMPT_TILE = 1024
ODD_PROMPT_TILE = 512
SAMPLE_TILE = 256
SUB_TILE = 512
ROW_BLOCK = 64
NORM_BLOCK = 32
FF_CHUNK = 256
VMEM_LIMIT = 56 * 1024 * 1024

_BF16 = jnp.bfloat16
_F32 = jnp.float32


def _round_up(n, m):
    return (n + m - 1) // m * m


def _dot(a, b):
    return jnp.dot(a, b, preferred_element_type=_F32)


def _rms_rows(x, g):
    y = x * lax.rsqrt(jnp.mean(x * x, axis=-1, keepdims=True) + EPS)
    return y * g


def _ln_rows(x, g, b=None):
    mu = jnp.mean(x, axis=-1, keepdims=True)
    xc = x - mu
    y = xc * lax.rsqrt(jnp.mean(xc * xc, axis=-1, keepdims=True) + EPS) * g
    if b is not None:
        y = y + b
    return y


def _norm_to_bf16(x_ref, g_ref, xn_ref, s0, sub):
    g = g_ref[...]
    for r0 in range(s0, s0 + sub, NORM_BLOCK):
        rows = slice(r0, r0 + NORM_BLOCK)
        xn_ref[rows, :] = _rms_rows(x_ref[rows, :], g).astype(_BF16)


def _init_history(xp_ref, state_ref, hist, first_tile):
    h_rows = _round_up(hist, SUBLANES)
    if state_ref is None:
        @pl.when(first_tile)
        def _():
            xp_ref[0:h_rows, :] = jnp.zeros((h_rows, xp_ref.shape[1]), _F32)
    else:
        xp_ref[h_rows - hist:h_rows, :] = state_ref[0]


def _emit_history(xp_ref, new_ref, hist, tm, carry):
    h_rows = _round_up(hist, SUBLANES)
    tail = xp_ref[h_rows + tm - hist:h_rows + tm, :]
    new_ref[0] = tail
    if carry:
        xp_ref[h_rows - hist:h_rows, :] = tail


def _conv_taps(xp_ref, w_ref, r0, cols, ktaps, stride, rb, xcols=None):
    xcols = cols if xcols is None else xcols
    hist = (ktaps - 1) * stride
    base = _round_up(hist, SUBLANES) - hist + r0
    acc = None
    for k in range(ktaps):
        term = w_ref[k:k + 1, cols] * xp_ref[base + k * stride:base + k * stride + rb, xcols]
        acc = term if acc is None else acc + term
    return acc


def _conv_long_stride1(xp_ref, w_ref, r0, cols, ktaps, rb):
    hist = ktaps - 1
    base = _round_up(hist, SUBLANES) - hist
    n = rb + SUBLANES
    p = None
    for r in range(SUBLANES - 1, -1, -1):
        z = None
        for k in range(ktaps):
            if (base + k) % SUBLANES != r:
                continue
            off = r0 + (base + k) - r
            term = w_ref[k:k + 1, cols] * xp_ref[off:off + n, cols]
            z = term if z is None else z + term
        if p is not None:
            p = pltpu.roll(p, n - 1, 0)
            p = p if z is None else p + z
        else:
            p = z
    return p[0:rb]


def _cast_weights_step(step, n_steps, layers, srcs, dsts, stage_in, stage_out, sem_in, sem_out):
    n_jobs = len(layers)

    def in_copy(j, s, slot):
        rc = stage_in[j].shape[1]
        return pltpu.make_async_copy(srcs[j].at[layers[j], pl.ds(s * rc, rc), :], stage_in[j].at[slot],
                                     sem_in.at[j, slot])

    def out_copy(j, s, slot):
        rc = stage_out[j].shape[1]
        return pltpu.make_async_copy(stage_out[j].at[slot], dsts[j].at[pl.ds(s * rc, rc), :], sem_out.at[j, slot])

    slot = lax.rem(step, 2)

    @pl.when(step == 0)
    def _():
        for j in range(n_jobs):
            in_copy(j, 0, 0).start()

    for j in range(n_jobs):
        in_copy(j, step, slot).wait()

    @pl.when(step + 1 < n_steps)
    def _():
        for j in range(n_jobs):
            in_copy(j, step + 1, 1 - slot).start()

    @pl.when(step >= 2)
    def _():
        for j in range(n_jobs):
            out_copy(j, step - 2, slot).wait()

    for j in range(n_jobs):
        src = stage_in[j].at[slot]
        dst = stage_out[j].at[slot]
        for r0 in range(0, src.shape[0], BF16_ROWS):
            dst[r0:r0 + BF16_ROWS, :] = src[r0:r0 + BF16_ROWS, :].astype(_BF16)
        out_copy(j, step, slot).start()

    def finish():
        @pl.when(step == n_steps - 1)
        def _():
            for j in range(n_jobs):
                if n_steps > 1:
                    out_copy(j, step - 1, 1 - slot).wait()
                out_copy(j, step, slot).wait()

    return finish


def _split_refs(refs, n_in, n_out, n_scratch, n_jobs):
    refs = list(refs)
    a = n_in
    b = a + n_jobs
    c = b + n_out
    d = c + n_jobs
    e = d + n_scratch
    return refs[:a], refs[a:b], refs[b:c], refs[c:d], refs[d:e], refs[e:]


def _maybe_cast_weights(cast_layers, step, n_steps, srcs, dsts, cast_scratch):
    n_jobs = len(cast_layers)
    if n_jobs == 0:
        return lambda: None
    stage_in = cast_scratch[:n_jobs]
    stage_out = cast_scratch[n_jobs:2 * n_jobs]
    sem_in, sem_out = cast_scratch[2 * n_jobs:]
    return _cast_weights_step(step, n_steps, cast_layers, srcs, dsts, stage_in, stage_out, sem_in, sem_out)


def _even_kernel(*refs, n_prompt, nt, tm_p, tm_s, stride_s, gate_layer, cast_layers):
    ins, srcs, outs, dsts, scr, cast_scr = _split_refs(refs, 12, 5, 4, len(cast_layers))
    xp_in, xs_in, g_ref, win_ref, wout_ref, lng_ref, cw_ref, gw_v, gb_v, gw_s, gb_s, state_s = ins
    out_p, newb_p, out_s, newb_s, v_s = outs
    xn_p, xp_p, ycat_p, vb_p = scr
    xn_s, xp_s, ycat_s = xn_p, xp_p, ycat_p
    i = pl.program_id(0)

    @pl.when(i < n_prompt)
    def _():
        finish_cast = _maybe_cast_weights(cast_layers, i, n_prompt, srcs, dsts, cast_scr)
        _even_body(xp_in, g_ref, win_ref, wout_ref, lng_ref, cw_ref, gw_v, gb_v, None, out_p, newb_p, None,
                   xn_p, xp_p, ycat_p, vb_p, tm=tm_p, stride=1, gate_layer=gate_layer,
                   first_tile=lax.rem(i, nt) == 0)
        finish_cast()

    @pl.when(i >= n_prompt)
    def _():
        _even_body(xs_in, g_ref, win_ref, wout_ref, lng_ref, cw_ref, gw_s, gb_s, state_s, out_s, newb_s, v_s,
                   xn_s, xp_s, ycat_s, None, tm=tm_s, stride=stride_s, gate_layer=gate_layer, first_tile=None)


def _even_body(x_ref, g_ref, win_ref, wout_ref, lng_ref, cw_ref, gw_ref, gb_ref, state_ref, out_ref, newb_ref,
               v_ref, xn_ref, xp_ref, ycat_ref, vb_ref, *, tm, stride, gate_layer, first_tile):
    has_state = state_ref is not None
    rb = ROW_BLOCK
    hist = (B_CONV - 1) * stride
    h_rows = _round_up(hist, SUBLANES)
    _init_history(xp_ref, state_ref, hist, first_tile)
    lng = lng_ref[...]
    allc = slice(0, B_WIDTH)
    if stride == 1:
        row_i = lax.broadcasted_iota(jnp.int32, (CHUNK, CHUNK), 0)
        col_i = lax.broadcasted_iota(jnp.int32, (CHUNK, CHUNK), 1)
        wts = [jnp.where(col_i <= row_i, gw_ref[h], 0.0).astype(_BF16) for h in range(A_HEADS)]

    sub = min(tm, SUB_TILE)
    for s0 in range(0, tm, sub):
        srows = slice(s0, s0 + sub)
        _norm_to_bf16(x_ref, g_ref, xn_ref, s0, sub)
        xn = xn_ref[srows, :]

        zbg = _dot(xn, win_ref[:, 2 * A_WIDTH:2 * A_WIDTH + B_WIDTH])
        zcg = _dot(xn, win_ref[:, 2 * A_WIDTH + B_WIDTH:2 * A_WIDTH + 2 * B_WIDTH])
        zh = _dot(xn, win_ref[:, 2 * A_WIDTH + 2 * B_WIDTH:2 * A_WIDTH + 3 * B_WIDTH])
        for r0 in range(s0, s0 + sub, rb):
            rows = slice(r0, r0 + rb)
            loc = slice(r0 - s0, r0 - s0 + rb)
            xp_ref[h_rows + r0:h_rows + r0 + rb, :] = zcg[loc] * zh[loc]
            conv = _conv_taps(xp_ref, cw_ref, r0, allc, B_CONV, stride, rb)
            ycat_ref[rows, A_WIDTH:A_WIDTH + B_WIDTH] = (zbg[loc] * conv).astype(_BF16)

        zu = _dot(xn, win_ref[:, 0:A_WIDTH])
        zv = _dot(xn, win_ref[:, A_WIDTH:2 * A_WIDTH])
        for r0 in range(s0, s0 + sub, NORM_BLOCK):
            rows = slice(r0, r0 + NORM_BLOCK)
            v = _ln_rows(jax.nn.gelu(zv[r0 - s0:r0 - s0 + NORM_BLOCK]), lng)
            if v_ref is not None:
                v_ref[rows, :] = v
            else:
                vb_ref[rows, :] = v.astype(_BF16)
        if stride == 1:
            for c0 in range(s0, s0 + sub, 2 * CHUNK):
                ra = slice(c0, c0 + CHUNK)
                rb2 = slice(c0 + CHUNK, c0 + 2 * CHUNK)
                for h in range(A_HEADS):
                    cols = slice(h * CHUNK, (h + 1) * CHUNK)
                    vpair = jnp.concatenate([vb_ref[ra, cols], vb_ref[rb2, cols]], axis=1)
                    mixed = _dot(wts[h], vpair)
                    bias = gb_ref[:, h:h + 1]
                    for half, rws in enumerate((ra, rb2)):
                        m = mixed[:, half * CHUNK:(half + 1) * CHUNK] + bias
                        u = zu[rws.start - s0:rws.stop - s0, cols]
                        ycat_ref[rws, cols] = (jax.nn.gelu(u) * m).astype(_BF16)
        else:
            steps = tm // stride
            gw_off = gate_layer * A_HEADS * steps * steps
            gb_off = gate_layer * A_HEADS * steps
            nblk = min(rb, stride)
            for t in range(s0 // stride, (s0 + sub) // stride):
                for n0 in range(0, stride, nblk):
                    rows = slice(t * stride + n0, t * stride + n0 + nblk)
                    for h in range(A_HEADS):
                        cols = slice(h * CHUNK, (h + 1) * CHUNK)
                        mixed = None
                        for s in range(t + 1):
                            w = gw_ref[gw_off + (h * steps + t) * steps + s]
                            term = w * v_ref[s * stride + n0:s * stride + n0 + nblk, cols]
                            mixed = term if mixed is None else mixed + term
                        mixed = mixed + gb_ref[gb_off + h * steps + t]
                        u = zu[rows.start - s0:rows.stop - s0, cols]
                        ycat_ref[rows, cols] = (jax.nn.gelu(u) * mixed).astype(_BF16)

        y = _dot(ycat_ref[srows, :], wout_ref[...])
        out_ref[srows, :] = x_ref[srows, :] + y
    _emit_history(xp_ref, newb_ref, hist, tm, carry=not has_state)


def _pool_block(xp_ref, r0, gi, win, stride, rb, pos0, row_pos_static):
    cols = slice(gi * D_GROUP_DIM, (gi + 1) * D_GROUP_DIM)
    hist = POOL_BUF * stride
    h_rows = _round_up(hist, SUBLANES)
    if stride % SUBLANES == 0:
        s = None
        for j in range(win):
            off = h_rows + r0 - j * stride
            term = xp_ref[off:off + rb, cols]
            s = term if s is None else s + term
    else:
        halo = 2 * SUBLANES
        ext = xp_ref[h_rows + r0 - halo:h_rows + r0 + rb, cols]
        shift = 1
        while shift < win:
            ext = ext + pltpu.roll(ext, shift, 0)
            shift *= 2
        s = ext[halo:]
    if row_pos_static is not None:
        return s / float(min(row_pos_static + 1, win))
    pos = pos0 + r0 + lax.broadcasted_iota(jnp.int32, (rb, D_GROUP_DIM), 0)
    cnt = jnp.minimum(pos + 1, win).astype(_F32)
    return s / cnt


def _odd_kernel(*refs, n_prompt, nt, tm_p, tm_s, stride_s, start_pos_s, cast_layers):
    ins, srcs, outs, dsts, scr, cast_scr = _split_refs(refs, 13, 6, 4, len(cast_layers))
    xp_in, xs_in = ins[:2]
    shared = ins[2:11]
    statec_s, stated_s = ins[11:]
    out_p, newc_p, newd_p, out_s, newc_s, newd_s = outs
    scr_p = scr_s = scr
    i = pl.program_id(0)

    @pl.when(i < n_prompt)
    def _():
        finish_cast = _maybe_cast_weights(cast_layers, i, n_prompt, srcs, dsts, cast_scr)
        t = lax.rem(i, nt)
        _odd_body(xp_in, *shared, None, None, out_p, newc_p, newd_p, *scr_p, tm=tm_p, stride=1,
                  start_pos=0, first_tile=t == 0, pos0=t * tm_p)
        finish_cast()

    @pl.when(i >= n_prompt)
    def _():
        _odd_body(xs_in, *shared, statec_s, stated_s, out_s, newc_s, newd_s, *scr_s, tm=tm_s, stride=stride_s,
                  start_pos=start_pos_s, first_tile=None, pos0=None)


def _odd_body(x_ref, g_ref, win_ref, wout_ref, ccw_ref, ccb_ref, clg_ref, clb_ref, dproj_ref, dscale_ref,
              statec_ref, stated_ref, out_ref, newc_ref, newd_ref, xn_ref, xpc_ref, xpd_ref, ycat_ref, *,
              tm, stride, start_pos, first_tile, pos0):
    has_state = statec_ref is not None
    rb = ROW_BLOCK
    hist_c = (C_CONV - 1) * stride
    hc_rows = _round_up(hist_c, SUBLANES)
    hist_d = POOL_BUF * stride
    hd_rows = _round_up(hist_d, SUBLANES)
    _init_history(xpc_ref, statec_ref, hist_c, first_tile)
    _init_history(xpd_ref, stated_ref, hist_d, first_tile)
    ccb = ccb_ref[...]
    clg = clg_ref[...]
    clb = clb_ref[...]

    sub = min(tm, SUB_TILE)
    for s0 in range(0, tm, sub):
        srows = slice(s0, s0 + sub)
        _norm_to_bf16(x_ref, g_ref, xn_ref, s0, sub)
        xn = xn_ref[srows, :]

        zca = _dot(xn, win_ref[:, 0:C_WIDTH])
        zcb = _dot(xn, win_ref[:, C_WIDTH:2 * C_WIDTH])
        zp = _dot(xn, win_ref[:, 2 * C_WIDTH:2 * C_WIDTH + D_WIDTH])
        for r0 in range(s0, s0 + sub, rb):
            loc = slice(r0 - s0, r0 - s0 + rb)
            xpc_ref[hc_rows + r0:hc_rows + r0 + rb, :] = zca[loc] * jax.nn.sigmoid(zcb[loc])
            xpd_ref[hd_rows + r0:hd_rows + r0 + rb, :] = zp[loc]
        xpc_ref[hc_rows + s0 + sub:hc_rows + s0 + sub + SUBLANES, :] = jnp.zeros((SUBLANES, C_WIDTH), _F32)

        for r0 in range(s0, s0 + sub, rb):
            rows = slice(r0, r0 + rb)
            parts = []
            for c0 in range(0, C_WIDTH, LANES):
                cols = slice(c0, c0 + LANES)
                if stride % SUBLANES == 0:
                    parts.append(_conv_taps(xpc_ref, ccw_ref, r0, cols, C_CONV, stride, rb))
                else:
                    parts.append(_conv_long_stride1(xpc_ref, ccw_ref, r0, cols, C_CONV, rb))
            cc = jnp.concatenate(parts, axis=-1) + ccb
            ycat_ref[rows, 0:C_WIDTH] = jax.nn.silu(_ln_rows(cc, clg, clb)).astype(_BF16)

        pb = rb if stride == 1 else min(rb, stride)
        for r0 in range(s0, s0 + sub, pb):
            rows = slice(r0, r0 + pb)
            for gi, win in enumerate(POOL_WINDOWS):
                cols = slice(gi * D_GROUP_DIM, (gi + 1) * D_GROUP_DIM)
                if stride == 1:
                    static_pos = None if r0 < POOL_BUF else POOL_BUF
                else:
                    static_pos = start_pos + r0 // stride
                pooled = _pool_block(xpd_ref, r0, gi, win, stride, pb, pos0, static_pos)
                diff = (pooled - xpd_ref[hd_rows + r0:hd_rows + r0 + pb, cols]).astype(_BF16)
                yd = _dot(diff, dproj_ref[gi]) * dscale_ref[:, cols]
                ycat_ref[rows, C_WIDTH + gi * D_GROUP_DIM:C_WIDTH + (gi + 1) * D_GROUP_DIM] = yd.astype(_BF16)

        y = _dot(ycat_ref[srows, :], wout_ref[...])
        out_ref[srows, :] = x_ref[srows, :] + y
    _emit_history(xpc_ref, newc_ref, hist_c, tm, carry=not has_state)
    _emit_history(xpd_ref, newd_ref, hist_d, tm, carry=not has_state)


def _ffn_kernel(*refs, n_prompt, nt, stride_s, final_norm):
    refs = list(refs)
    xp_in, xs_in, g_ref, win_ref, wout_ref, cw_ref, state_s = refs[:7]
    del refs[:7]
    gfin_ref = refs.pop(0) if final_norm else None
    out_p, newf_p, out_s, newf_s, xn_p, h_p, gp_p, hist_p, xn_s, h_s, gp_s, hist_s = refs
    i = pl.program_id(0)

    @pl.when(i < n_prompt)
    def _():
        _ffn_body(xp_in, g_ref, win_ref, wout_ref, cw_ref, None, gfin_ref, out_p, newf_p, xn_p, gp_p, hist_p,
                  h_p, stride=1, first_tile=lax.rem(i, nt) == 0)

    @pl.when(i >= n_prompt)
    def _():
        _ffn_body(xs_in, g_ref, win_ref, wout_ref, cw_ref, state_s, gfin_ref, out_s, newf_s, xn_s, gp_s, hist_s,
                  h_s, stride=stride_s, first_tile=None)


def _ffn_body(x_ref, g_ref, win_ref, wout_ref, cw_ref, state_ref, gfin_ref, out_ref, newf_ref, xn_ref, gp_ref,
              hist_ref, h_ref, *, stride, first_tile):
    final_norm = gfin_ref is not None
    rb = ROW_BLOCK
    hist = (FFN_CONV - 1) * stride
    h_rows = _round_up(hist, SUBLANES)
    _init_history(hist_ref, state_ref, hist, first_tile)
    n_sub, sub, _ = x_ref.shape

    def sub_tile(i, carry):
        xs = x_ref.at[i]
        os = out_ref.at[i]
        _norm_to_bf16(xs, g_ref, xn_ref, 0, sub)
        xn = xn_ref[...]
        for j, c0 in enumerate(range(0, D_FF, FF_CHUNK)):
            cols = slice(c0, c0 + FF_CHUNK)
            slot = slice((j % 2) * FF_CHUNK, (j % 2 + 1) * FF_CHUNK)
            gp_ref[0:h_rows, slot] = hist_ref[:, cols]
            zg = _dot(xn, win_ref[:, c0:c0 + FF_CHUNK])
            zu = _dot(xn, win_ref[:, D_FF + c0:D_FF + c0 + FF_CHUNK])
            for r0 in range(0, sub, rb):
                rows = slice(r0, r0 + rb)
                gp_ref[h_rows + r0:h_rows + r0 + rb, slot] = zg[rows]
                conv = _conv_taps(gp_ref, cw_ref, r0, cols, FFN_CONV, stride, rb, xcols=slot)
                h_ref[rows, cols] = (jax.nn.gelu(conv) * zu[rows]).astype(_BF16)
            hist_ref[:, cols] = gp_ref[sub:sub + h_rows, slot]

        y = xs[...] + _dot(h_ref[...], wout_ref[...])
        if final_norm:
            gfin = gfin_ref[...]
            os[...] = y
            for r0 in range(0, sub, NORM_BLOCK):
                rows = slice(r0, r0 + NORM_BLOCK)
                os[rows, :] = _rms_rows(os[rows, :], gfin)
        else:
            os[...] = y
        return carry

    lax.fori_loop(0, n_sub, sub_tile, 0)
    newf_ref[0] = hist_ref[h_rows - hist:h_rows, :]


def _const_spec(shape):
    nd = len(shape)
    return pl.BlockSpec(shape, lambda i: (0,) * nd, pipeline_mode=pl.Buffered(1))


def _layer_spec(arr, layer):
    nd = arr.ndim - 1
    return pl.BlockSpec((None,) + arr.shape[1:], lambda i: (layer,) + (0,) * nd, pipeline_mode=pl.Buffered(1))


def _smem_spec():
    return pl.BlockSpec(memory_space=pltpu.SMEM)


def _hbm_spec():
    return pl.BlockSpec(memory_space=pl.ANY)


class _Steps:
    def __init__(self, n_prompt, nt, n_sample):
        self.n_prompt, self.nt, self.n_sample = n_prompt, nt, n_sample

    def prompt_tile(self, i):
        return jnp.minimum(i, self.n_prompt - 1)

    def sample_tile(self, i):
        return jnp.maximum(i - self.n_prompt, 0)

    def prompt_rows(self, block):
        nd = len(block) - 1
        return pl.BlockSpec(block, lambda i: (self.prompt_tile(i),) + (0,) * nd)

    def sample_rows(self, block):
        nd = len(block) - 1
        return pl.BlockSpec(block, lambda i: (self.sample_tile(i),) + (0,) * nd)

    def prompt_new_state(self, rows, width):
        return pl.BlockSpec((1, rows, width), lambda i: (self.prompt_tile(i) // self.nt, 0, 0))

    def sample_state(self, layer, rows, width):
        return pl.BlockSpec((None, 1, rows, width), lambda i: (layer, self.sample_tile(i), 0, 0),
                            pipeline_mode=pl.Buffered(1))

    def sample_new_state(self, rows, width):
        return pl.BlockSpec((1, rows, width), lambda i: (self.sample_tile(i), 0, 0))


def _params():
    return pltpu.CompilerParams(dimension_semantics=("arbitrary",), vmem_limit_bytes=VMEM_LIMIT)


def _cast_plumbing(cast_jobs, n_steps):
    in_specs, args, out_specs, out_shape, stage_in, stage_out = [], [], [], [], [], []
    for arr, _ in cast_jobs:
        _, r, c = arr.shape
        rc = r // n_steps
        assert rc * n_steps == r and rc % BF16_ROWS == 0, (arr.shape, n_steps)
        in_specs.append(_hbm_spec())
        args.append(arr)
        out_specs.append(_hbm_spec())
        out_shape.append(jax.ShapeDtypeStruct((r, c), _BF16))
        stage_in.append(pltpu.VMEM((2, rc, c), _F32))
        stage_out.append(pltpu.VMEM((2, rc, c), _BF16))
    scratch = stage_in + stage_out
    if cast_jobs:
        n = len(cast_jobs)
        scratch += [pltpu.SemaphoreType.DMA((n, 2)), pltpu.SemaphoreType.DMA((n, 2))]
    return in_specs, args, out_specs, out_shape, scratch, tuple(layer for _, layer in cast_jobs)


def _even_call(xp2, xs2, steps, stride_s, state_b, layer, win, wout, prm, cast_jobs, name):
    i = layer // 2
    tm_p = xp2.shape[0] // steps.n_prompt
    tm_s = xs2.shape[0] // steps.n_sample
    nb_p = steps.n_prompt // steps.nt
    hist_p = B_CONV - 1
    hist_s = (B_CONV - 1) * stride_s
    hp_rows = _round_up(hist_p, SUBLANES)
    hs_rows = _round_up(hist_s, SUBLANES)
    c_in, c_args, c_out_specs, c_out_shape, c_scratch, cast_layers = _cast_plumbing(cast_jobs, steps.n_prompt)
    in_specs = [steps.prompt_rows((tm_p, D_MODEL)), steps.sample_rows((tm_s, D_MODEL)),
                _layer_spec(prm['norm_mix_g'], layer), _const_spec(win.shape), _const_spec(wout.shape),
                _layer_spec(prm['a_ln_g'], i), _layer_spec(prm['b_conv_w'], i), _layer_spec(prm['a_ws'], i),
                _layer_spec(prm['a_bs_t'], i), _smem_spec(), _smem_spec(), steps.sample_state(i, hist_s, B_WIDTH)]
    args = [xp2, xs2, prm['norm_mix_g'], win, wout, prm['a_ln_g'], prm['b_conv_w'], prm['a_ws'], prm['a_bs_t'],
            prm['a_ws_steps'], prm['a_bs_steps'], state_b]
    out_shape = [jax.ShapeDtypeStruct(xp2.shape, _F32), jax.ShapeDtypeStruct((nb_p, hist_p, B_WIDTH), _F32),
                 jax.ShapeDtypeStruct(xs2.shape, _F32), jax.ShapeDtypeStruct((steps.n_sample, hist_s, B_WIDTH), _F32),
                 jax.ShapeDtypeStruct((xs2.shape[0], A_WIDTH), _F32)]
    out_specs = [steps.prompt_rows((tm_p, D_MODEL)), steps.prompt_new_state(hist_p, B_WIDTH),
                 steps.sample_rows((tm_s, D_MODEL)), steps.sample_new_state(hist_s, B_WIDTH),
                 steps.sample_rows((tm_s, A_WIDTH))]
    tm_max = max(tm_p, tm_s)
    scratch = [pltpu.VMEM((tm_max, D_MODEL), _BF16),
               pltpu.VMEM((max(hp_rows + tm_p, hs_rows + tm_s), B_WIDTH), _F32),
               pltpu.VMEM((tm_max, A_WIDTH + B_WIDTH), _BF16), pltpu.VMEM((tm_p, A_WIDTH), _BF16)]
    return pl.pallas_call(
        functools.partial(_even_kernel, n_prompt=steps.n_prompt, nt=steps.nt, tm_p=tm_p, tm_s=tm_s,
                          stride_s=stride_s, gate_layer=i, cast_layers=cast_layers),
        grid=(steps.n_prompt + steps.n_sample,), in_specs=in_specs + c_in, out_specs=out_specs + c_out_specs,
        out_shape=out_shape + c_out_shape, scratch_shapes=scratch + c_scratch,
        compiler_params=_params(), name=name)(*args, *c_args)


def _odd_call(xp2, xs2, steps, stride_s, state_c, state_d, start_pos_s, layer, win, wout, prm, cast_jobs, name):
    i = layer // 2
    tm_p = xp2.shape[0] // steps.n_prompt
    tm_s = xs2.shape[0] // steps.n_sample
    nb_p = steps.n_prompt // steps.nt
    c_in, c_args, c_out_specs, c_out_shape, c_scratch, cast_layers = _cast_plumbing(cast_jobs, steps.n_prompt)
    names = ['c_conv_w', 'c_conv_b', 'c_ln_g', 'c_ln_b', 'd_proj', 'd_scale']
    hist = {}
    for tag, stride in (('p', 1), ('s', stride_s)):
        hist[tag] = ((C_CONV - 1) * stride, POOL_BUF * stride)
    in_specs = [steps.prompt_rows((tm_p, D_MODEL)), steps.sample_rows((tm_s, D_MODEL)),
                _layer_spec(prm['norm_mix_g'], layer), _const_spec(win.shape), _const_spec(wout.shape)]
    in_specs += [_layer_spec(prm[k], i) for k in names]
    in_specs += [steps.sample_state(i, hist['s'][0], C_WIDTH), steps.sample_state(i, hist['s'][1], D_WIDTH)]
    args = [xp2, xs2, prm['norm_mix_g'], win, wout] + [prm[k] for k in names] + [state_c, state_d]
    out_shape = [jax.ShapeDtypeStruct(xp2.shape, _F32),
                 jax.ShapeDtypeStruct((nb_p, hist['p'][0], C_WIDTH), _F32),
                 jax.ShapeDtypeStruct((nb_p, hist['p'][1], D_WIDTH), _F32),
                 jax.ShapeDtypeStruct(xs2.shape, _F32),
                 jax.ShapeDtypeStruct((steps.n_sample, hist['s'][0], C_WIDTH), _F32),
                 jax.ShapeDtypeStruct((steps.n_sample, hist['s'][1], D_WIDTH), _F32)]
    out_specs = [steps.prompt_rows((tm_p, D_MODEL)), steps.prompt_new_state(hist['p'][0], C_WIDTH),
                 steps.prompt_new_state(hist['p'][1], D_WIDTH), steps.sample_rows((tm_s, D_MODEL)),
                 steps.sample_new_state(hist['s'][0], C_WIDTH), steps.sample_new_state(hist['s'][1], D_WIDTH)]
    tm_max = max(tm_p, tm_s)
    c_rows = max(_round_up(hist[tag][0], SUBLANES) + tm for tag, tm in (('p', tm_p), ('s', tm_s)))
    d_rows = max(_round_up(hist[tag][1], SUBLANES) + tm for tag, tm in (('p', tm_p), ('s', tm_s)))
    scratch = [pltpu.VMEM((tm_max, D_MODEL), _BF16), pltpu.VMEM((c_rows + SUBLANES, C_WIDTH), _F32),
               pltpu.VMEM((d_rows, D_WIDTH), _F32), pltpu.VMEM((tm_max, C_WIDTH + D_WIDTH), _BF16)]
    return pl.pallas_call(
        functools.partial(_odd_kernel, n_prompt=steps.n_prompt, nt=steps.nt, tm_p=tm_p, tm_s=tm_s,
                          stride_s=stride_s, start_pos_s=start_pos_s, cast_layers=cast_layers),
        grid=(steps.n_prompt + steps.n_sample,), in_specs=in_specs + c_in, out_specs=out_specs + c_out_specs,
        out_shape=out_shape + c_out_shape, scratch_shapes=scratch + c_scratch,
        compiler_params=_params(), name=name)(*args, *c_args)


def _ffn_call(xp2, xs2, steps, stride_s, state_f, layer, final_norm, win, wout, prm, name):
    tm_p = xp2.shape[0] // steps.n_prompt
    tm_s = xs2.shape[0] // steps.n_sample
    nb_p = steps.n_prompt // steps.nt
    sub_p = min(tm_p, SUB_TILE)
    sub_s = min(tm_s, SUB_TILE)
    xp3 = xp2.reshape(-1, sub_p, D_MODEL)
    xs3 = xs2.reshape(-1, sub_s, D_MODEL)
    p_spec = steps.prompt_rows((tm_p // sub_p, sub_p, D_MODEL))
    s_spec = steps.sample_rows((tm_s // sub_s, sub_s, D_MODEL))
    hist_p = FFN_CONV - 1
    hist_s = (FFN_CONV - 1) * stride_s
    hp_rows = _round_up(hist_p, SUBLANES)
    hs_rows = _round_up(hist_s, SUBLANES)
    in_specs = [p_spec, s_spec, _layer_spec(prm['norm_ffn_g'], layer), _const_spec(win.shape),
                _const_spec(wout.shape), _layer_spec(prm['ffn_conv_w'], layer),
                steps.sample_state(layer, hist_s, D_FF)]
    args = [xp3, xs3, prm['norm_ffn_g'], win, wout, prm['ffn_conv_w'], state_f]
    if final_norm:
        in_specs.append(_const_spec((1, D_MODEL)))
        args.append(prm['norm_final_g'])
    out_shape = [jax.ShapeDtypeStruct(xp3.shape, _F32), jax.ShapeDtypeStruct((nb_p, hist_p, D_FF), _F32),
                 jax.ShapeDtypeStruct(xs3.shape, _F32), jax.ShapeDtypeStruct((steps.n_sample, hist_s, D_FF), _F32)]
    out_specs = [p_spec, steps.prompt_new_state(hist_p, D_FF), s_spec, steps.sample_new_state(hist_s, D_FF)]
    scratch = []
    for sub, h_rows in ((sub_p, hp_rows), (sub_s, hs_rows)):
        scratch += [pltpu.VMEM((sub, D_MODEL), _BF16), pltpu.VMEM((sub, D_FF), _BF16),
                    pltpu.VMEM((h_rows + sub, 2 * FF_CHUNK), _F32), pltpu.VMEM((h_rows, D_FF), _F32)]
    yp3, newf_p, ys3, newf_s = pl.pallas_call(
        functools.partial(_ffn_kernel, n_prompt=steps.n_prompt, nt=steps.nt, stride_s=stride_s,
                          final_norm=final_norm),
        grid=(steps.n_prompt + steps.n_sample,), in_specs=in_specs, out_specs=out_specs, out_shape=out_shape,
        scratch_shapes=scratch, compiler_params=_params(), name=name)(*args)
    return yp3.reshape(xp2.shape), newf_p, ys3.reshape(xs2.shape), newf_s


def _to_tiles(a, ns):
    *lead, n, l, c = a.shape
    k = len(lead)
    a = a.reshape(*lead, n // ns, ns, l, c)
    a = a.transpose(*range(k), k, k + 2, k + 1, k + 3)
    return a.reshape(*lead, n // ns, l * ns, c)


def _from_tiles(a, ns, l):
    *lead, nb, _, c = a.shape
    k = len(lead)
    a = a.reshape(*lead, nb, l, ns, c)
    a = a.transpose(*range(k), k, k + 2, k + 1, k + 3)
    return a.reshape(*lead, nb * ns, l, c)


def _trunks(xp2, xs2, steps, steps_odd, stride_s, states, start_pos_s, prm, raw):
    new = {k: [] for k in ('a_s', 'b_p', 'b_s', 'c_p', 'c_s', 'd_p', 'd_s', 'f_p', 'f_s')}
    depth = prm['ffn_conv_w'].shape[0]
    wb = {('mix_in', 0): raw['w_in_even'][0].astype(_BF16), ('mix_out', 0): raw['w_out_even'][0].astype(_BF16)}
    for layer in range(depth):
        keys = [('ffn_in', layer)] + ([('ffn_out', layer)] if layer % 2 == 0 else [])
        if layer + 1 < depth:
            keys += [('mix_in', layer + 1), ('mix_out', layer + 1)]
            if layer % 2 == 0:
                keys.append(('ffn_out', layer + 1))
        cast_jobs = []
        for kind, lyr in keys:
            if kind.startswith('ffn'):
                cast_jobs.append((raw['w_' + kind], lyr))
            else:
                cast_jobs.append((raw[f"w_{kind[4:]}_{'even' if lyr % 2 == 0 else 'odd'}"], lyr // 2))
        mix_w = (wb[('mix_in', layer)], wb[('mix_out', layer)])
        if layer % 2 == 0:
            res = _even_call(xp2, xs2, steps, stride_s, states['b'], layer, *mix_w, prm, cast_jobs,
                             name=f"mix{layer}")
            xp2, b_p, xs2, b_s, a_s = res[:5]
            cast = res[5:]
            new['b_p'].append(b_p)
            new['b_s'].append(b_s)
            new['a_s'].append(a_s)
        else:
            res = _odd_call(xp2, xs2, steps_odd, stride_s, states['c'], states['d'], start_pos_s, layer, *mix_w,
                            prm, cast_jobs, name=f"mix{layer}")
            xp2, c_p, d_p, xs2, c_s, d_s = res[:6]
            cast = res[6:]
            new['c_p'].append(c_p)
            new['d_p'].append(d_p)
            new['c_s'].append(c_s)
            new['d_s'].append(d_s)
        wb.update(zip(keys, cast))
        ffn_w = (wb[('ffn_in', layer)], wb[('ffn_out', layer)])
        xp2, f_p, xs2, f_s = _ffn_call(xp2, xs2, steps, stride_s, states['f'], layer, layer == depth - 1, *ffn_w,
                                       prm, name=f"ffn{layer}")
        new['f_p'].append(f_p)
        new['f_s'].append(f_s)
    return xp2, xs2, new


def kernel(x_prompt, x_sample, state_b_conv, state_c_conv, state_d_pool, state_ffn_conv, norm_mix_g, norm_ffn_g, norm_final_g, w_in_even, w_out_even, a_ln_g, a_ws, a_bs, b_conv_w, w_in_odd, w_out_odd, c_conv_w, c_conv_b, c_ln_g, c_ln_b, d_proj, d_scale, w_ffn_in, ffn_conv_w, w_ffn_out):
    batch, seq, d = x_prompt.shape
    dec_batch, dec_seq, _ = x_sample.shape
    ns = SAMPLE_TILE // dec_seq

    def rows(p):
        return p.reshape(p.shape[0], 1, p.shape[1])

    prm = dict(norm_mix_g=rows(norm_mix_g), norm_ffn_g=rows(norm_ffn_g), norm_final_g=norm_final_g.reshape(1, -1),
               a_ln_g=rows(a_ln_g), a_ws=a_ws, a_bs_t=jnp.swapaxes(a_bs, 1, 2),
               a_ws_steps=a_ws[:, :, :dec_seq, :dec_seq].reshape(-1), a_bs_steps=a_bs[:, :, :dec_seq].reshape(-1),
               b_conv_w=b_conv_w,
               c_conv_w=jnp.pad(c_conv_w, ((0, 0), (0, _round_up(C_CONV, SUBLANES) - C_CONV), (0, 0))),
               c_conv_b=rows(c_conv_b), c_ln_g=rows(c_ln_g), c_ln_b=rows(c_ln_b), d_proj=d_proj.astype(_BF16),
               d_scale=rows(d_scale), ffn_conv_w=ffn_conv_w)
    raw = dict(w_in_even=w_in_even, w_out_even=w_out_even, w_in_odd=w_in_odd, w_out_odd=w_out_odd,
               w_ffn_in=w_ffn_in, w_ffn_out=w_ffn_out)

    steps = _Steps(batch * seq // PROMPT_TILE, seq // PROMPT_TILE, dec_batch // ns)
    steps_odd = _Steps(batch * seq // ODD_PROMPT_TILE, seq // ODD_PROMPT_TILE, dec_batch // ns)
    states = dict(b=_to_tiles(state_b_conv, ns), c=_to_tiles(state_c_conv, ns), d=_to_tiles(state_d_pool, ns),
                  f=_to_tiles(state_ffn_conv, ns))
    xs = _to_tiles(x_sample, ns).reshape(dec_batch * dec_seq, d)
    yp, ys, new = _trunks(x_prompt.reshape(batch * seq, d), xs, steps, steps_odd, ns, states, PAST_LEN, prm, raw)

    def untile(parts, l):
        return _from_tiles(jnp.stack(parts), ns, l)

    y_prompt = yp.reshape(batch, seq, d)
    y_sample = _from_tiles(ys.reshape(dec_batch // ns, dec_seq * ns, d), ns, dec_seq)
    new_a = untile([a.reshape(dec_batch // ns, dec_seq * ns, A_WIDTH) for a in new['a_s']], dec_seq)
    return (y_prompt, y_sample, new_a, jnp.stack(new['b_p']), untile(new['b_s'], B_CONV - 1),
            jnp.stack(new['c_p']), untile(new['c_s'], C_CONV - 1), jnp.stack(new['d_p']),
            untile(new['d_s'], POOL_BUF), jnp.stack(new['f_p']), untile(new['f_s'], FFN_CONV - 1))
```

```python
import functools

import jax
import jax.numpy as jnp
from jax import lax
from jax.experimental import pallas as pl
from jax.experimental.pallas import tpu as pltpu

D_MODEL = 1024
CHUNK = 128
A_HEADS = 4
A_WIDTH = 512
B_WIDTH = 512
B_CONV = 3
C_WIDTH = 512
C_CONV = 31
D_WIDTH = 512
POOL_WINDOWS = (2, 4, 8, 16)
D_GROUP_DIM = 128
POOL_BUF = 15
D_FF = 2816
FFN_CONV = 3
PAST_LEN = 16384
EPS = 1e-6

SUBLANES = 8
LANES = 128
BF16_ROWS = 16
MXU_COLS = 256
PROMPT_TILE = 1024
SAMPLE_TILE = 512
SUB_TILE = 512
ODD_SUB_TILE = 512
ROW_BLOCK = 64
NORM_BLOCK = 32
FF_CHUNK = 256
VMEM_LIMIT = 56 * 1024 * 1024

_BF16 = jnp.bfloat16
_F32 = jnp.float32


def _round_up(n, m):
    return (n + m - 1) // m * m


def _dot(a, b):
    return jnp.dot(a, b, preferred_element_type=_F32)


def _rms_rows(x, g):
    y = x * lax.rsqrt(jnp.mean(x * x, axis=-1, keepdims=True) + EPS)
    return y * g


def _ln_rows(x, g, b=None):
    mu = jnp.mean(x, axis=-1, keepdims=True)
    xc = x - mu
    y = xc * lax.rsqrt(jnp.mean(xc * xc, axis=-1, keepdims=True) + EPS) * g
    if b is not None:
        y = y + b
    return y


def _norm_to_bf16(x_ref, g_ref, xn_ref, s0, sub):
    g = g_ref[...]
    for r0 in range(s0, s0 + sub, NORM_BLOCK):
        rows = slice(r0, r0 + NORM_BLOCK)
        xn_ref[rows, :] = _rms_rows(x_ref[rows, :], g).astype(_BF16)


def _init_history(xp_ref, state_ref, hist, first_tile):
    h_rows = _round_up(hist, SUBLANES)
    if state_ref is None:
        @pl.when(first_tile)
        def _():
            xp_ref[0:h_rows, :] = jnp.zeros((h_rows, xp_ref.shape[1]), _F32)
    else:
        xp_ref[h_rows - hist:h_rows, :] = state_ref[0]


def _emit_history(xp_ref, new_ref, hist, tm, carry):
    h_rows = _round_up(hist, SUBLANES)
    tail = xp_ref[h_rows + tm - hist:h_rows + tm, :]
    new_ref[0] = tail
    if carry:
        xp_ref[h_rows - hist:h_rows, :] = tail


def _conv_taps(xp_ref, w_ref, r0, cols, ktaps, stride, rb, xcols=None):
    xcols = cols if xcols is None else xcols
    hist = (ktaps - 1) * stride
    base = _round_up(hist, SUBLANES) - hist + r0
    acc = None
    for k in range(ktaps):
        term = w_ref[k:k + 1, cols] * xp_ref[base + k * stride:base + k * stride + rb, xcols]
        acc = term if acc is None else acc + term
    return acc


def _conv_long_stride1(xp_ref, w_ref, r0, cols, ktaps, rb):
    hist = ktaps - 1
    base = _round_up(hist, SUBLANES) - hist
    n = rb + SUBLANES
    p = None
    for r in range(SUBLANES - 1, -1, -1):
        z = None
        for k in range(ktaps):
            if (base + k) % SUBLANES != r:
                continue
            off = r0 + (base + k) - r
            term = w_ref[k:k + 1, cols] * xp_ref[off:off + n, cols]
            z = term if z is None else z + term
        if p is not None:
            p = pltpu.roll(p, n - 1, 0)
            p = p if z is None else p + z
        else:
            p = z
    return p[0:rb]


def _ordered_after(row, token):
    if token is None:
        return row
    half = jnp.uint32(16)
    bits = lax.bitcast_convert_type(token, jnp.uint32)
    zero = lax.shift_right_logical(lax.shift_right_logical(bits, half), half)
    head = lax.bitcast_convert_type(lax.bitcast_convert_type(row[:, 0:LANES], jnp.uint32) + zero, _F32)
    if row.shape[1] == LANES:
        return head
    return jnp.concatenate([head, row[:, LANES:]], axis=1)


def _cast_weights_step(step, n_steps, layers, srcs, dsts, stage_in, stage_out, sem_in, sem_out):
    n_jobs = len(layers)

    def in_copy(j, s, slot):
        rc = stage_in[j].shape[1]
        return pltpu.make_async_copy(srcs[j].at[layers[j], pl.ds(s * rc, rc), :], stage_in[j].at[slot],
                                     sem_in.at[j, slot])

    def out_copy(j, s, slot):
        rc = stage_out[j].shape[1]
        return pltpu.make_async_copy(stage_out[j].at[slot], dsts[j].at[pl.ds(s * rc, rc), :], sem_out.at[j, slot])

    slot = lax.rem(step, 2)

    @pl.when(step == 0)
    def _():
        for j in range(n_jobs):
            in_copy(j, 0, 0).start()

    for j in range(n_jobs):
        in_copy(j, step, slot).wait()

    @pl.when(step + 1 < n_steps)
    def _():
        for j in range(n_jobs):
            in_copy(j, step + 1, 1 - slot).start()

    @pl.when(step >= 2)
    def _():
        for j in range(n_jobs):
            out_copy(j, step - 2, slot).wait()

    for j in range(n_jobs):
        src = stage_in[j].at[slot]
        dst = stage_out[j].at[slot]
        for r0 in range(0, src.shape[0], BF16_ROWS):
            dst[r0:r0 + BF16_ROWS, :] = src[r0:r0 + BF16_ROWS, :].astype(_BF16)
        out_copy(j, step, slot).start()

    def finish():
        @pl.when(step == n_steps - 1)
        def _():
            for j in range(n_jobs):
                if n_steps > 1:
                    out_copy(j, step - 1, 1 - slot).wait()
                out_copy(j, step, slot).wait()

    return finish


def _split_refs(refs, n_in, n_out, n_scratch, n_jobs):
    refs = list(refs)
    a = n_in
    b = a + n_jobs
    c = b + n_out
    d = c + n_jobs
    e = d + n_scratch
    return refs[:a], refs[a:b], refs[b:c], refs[c:d], refs[d:e], refs[e:]


def _maybe_cast_weights(cast_layers, grid_shape, srcs, dsts, cast_scratch):
    n_jobs = len(cast_layers)
    if n_jobs == 0:
        return lambda: None
    stage_in = cast_scratch[:n_jobs]
    stage_out = cast_scratch[n_jobs:2 * n_jobs]
    sem_in, sem_out = cast_scratch[2 * n_jobs:]
    nb, nt = grid_shape
    step = pl.program_id(0) * nt + pl.program_id(1)
    return _cast_weights_step(step, nb * nt, cast_layers, srcs, dsts, stage_in, stage_out, sem_in, sem_out)


def _even_kernel(*refs, tm, stride, has_state, gate_layer, cast_layers, grid_shape):
    ins, srcs, outs, dsts, scr, cast_scr = _split_refs(
        refs, 9 if has_state else 8, 3 if has_state else 2, 4, len(cast_layers))
    x_ref, g_ref, win_ref, wout_ref, lng_ref, cw_ref, gw_ref, gb_ref = ins[:8]
    state_ref = ins[8] if has_state else None
    out_ref, newb_ref = outs[:2]
    v_ref = outs[2] if has_state else None
    xn_ref, xp_ref, ycat_ref, vb_ref = scr
    finish_cast = _maybe_cast_weights(cast_layers, grid_shape, srcs, dsts, cast_scr)

    rb = ROW_BLOCK
    hist = (B_CONV - 1) * stride
    h_rows = _round_up(hist, SUBLANES)
    _init_history(xp_ref, state_ref, hist, pl.program_id(1) == 0)
    lng = lng_ref[...]
    allc = slice(0, B_WIDTH)
    if stride == 1:
        row_i = lax.broadcasted_iota(jnp.int32, (CHUNK, CHUNK), 0)
        col_i = lax.broadcasted_iota(jnp.int32, (CHUNK, CHUNK), 1)
        wts = [jnp.where(col_i <= row_i, gw_ref[h], 0.0).astype(_BF16) for h in range(A_HEADS)]

    sub = min(tm, SUB_TILE)
    for s0 in range(0, tm, sub):
        srows = slice(s0, s0 + sub)
        _norm_to_bf16(x_ref, g_ref, xn_ref, s0, sub)
        xn = xn_ref[srows, :]

        zbg = _dot(xn, win_ref[:, 2 * A_WIDTH:2 * A_WIDTH + B_WIDTH])
        zcg = _dot(xn, win_ref[:, 2 * A_WIDTH + B_WIDTH:2 * A_WIDTH + 2 * B_WIDTH])
        zh = _dot(xn, win_ref[:, 2 * A_WIDTH + 2 * B_WIDTH:2 * A_WIDTH + 3 * B_WIDTH])
        for r0 in range(s0, s0 + sub, rb):
            rows = slice(r0, r0 + rb)
            loc = slice(r0 - s0, r0 - s0 + rb)
            xp_ref[h_rows + r0:h_rows + r0 + rb, :] = zcg[loc] * zh[loc]
            conv = _conv_taps(xp_ref, cw_ref, r0, allc, B_CONV, stride, rb)
            ycat_ref[rows, A_WIDTH:A_WIDTH + B_WIDTH] = (zbg[loc] * conv).astype(_BF16)

        zu = _dot(xn, win_ref[:, 0:A_WIDTH])
        zv = _dot(xn, win_ref[:, A_WIDTH:2 * A_WIDTH])
        for r0 in range(s0, s0 + sub, NORM_BLOCK):
            rows = slice(r0, r0 + NORM_BLOCK)
            v = _ln_rows(jax.nn.gelu(zv[r0 - s0:r0 - s0 + NORM_BLOCK]), lng)
            if v_ref is not None:
                v_ref[rows, :] = v
            else:
                vb_ref[rows, :] = v.astype(_BF16)
        if stride == 1:
            for c0 in range(s0, s0 + sub, 2 * CHUNK):
                ra = slice(c0, c0 + CHUNK)
                rb2 = slice(c0 + CHUNK, c0 + 2 * CHUNK)
                for h in range(A_HEADS):
                    cols = slice(h * CHUNK, (h + 1) * CHUNK)
                    vpair = jnp.concatenate([vb_ref[ra, cols], vb_ref[rb2, cols]], axis=1)
                    mixed = _dot(wts[h], vpair)
                    bias = gb_ref[:, h:h + 1]
                    for half, rws in enumerate((ra, rb2)):
                        m = mixed[:, half * CHUNK:(half + 1) * CHUNK] + bias
                        u = zu[rws.start - s0:rws.stop - s0, cols]
                        ycat_ref[rws, cols] = (jax.nn.gelu(u) * m).astype(_BF16)
        else:
            steps = tm // stride
            gw_off = gate_layer * A_HEADS * steps * steps
            gb_off = gate_layer * A_HEADS * steps
            for t in range(s0 // stride, (s0 + sub) // stride):
                for n0 in range(0, stride, rb):
                    rows = slice(t * stride + n0, t * stride + n0 + rb)
                    for h in range(A_HEADS):
                        cols = slice(h * CHUNK, (h + 1) * CHUNK)
                        mixed = None
                        for s in range(t + 1):
                            w = gw_ref[gw_off + (h * steps + t) * steps + s]
                            term = w * v_ref[s * stride + n0:s * stride + n0 + rb, cols]
                            mixed = term if mixed is None else mixed + term
                        mixed = mixed + gb_ref[gb_off + h * steps + t]
                        u = zu[rows.start - s0:rows.stop - s0, cols]
                        ycat_ref[rows, cols] = (jax.nn.gelu(u) * mixed).astype(_BF16)

        y = _dot(ycat_ref[srows, :], wout_ref[...])
        out_ref[srows, :] = x_ref[srows, :] + y
    _emit_history(xp_ref, newb_ref, hist, tm, carry=not has_state)
    finish_cast()


def _pool_block(xp_ref, r0, gi, win, stride, rb, pos0, row_pos_static):
    cols = slice(gi * D_GROUP_DIM, (gi + 1) * D_GROUP_DIM)
    hist = POOL_BUF * stride
    h_rows = _round_up(hist, SUBLANES)
    if stride % SUBLANES == 0:
        s = None
        for j in range(win):
            off = h_rows + r0 - j * stride
            term = xp_ref[off:off + rb, cols]
            s = term if s is None else s + term
    else:
        halo = 2 * SUBLANES
        ext = xp_ref[h_rows + r0 - halo:h_rows + r0 + rb, cols]
        shift = 1
        while shift < win:
            ext = ext + pltpu.roll(ext, shift, 0)
            shift *= 2
        s = ext[halo:]
    if row_pos_static is not None:
        return s / float(min(row_pos_static + 1, win))
    pos = pos0 + r0 + lax.broadcasted_iota(jnp.int32, (rb, D_GROUP_DIM), 0)
    cnt = jnp.minimum(pos + 1, win).astype(_F32)
    return s / cnt


def _odd_kernel(*refs, tm, stride, has_state, start_pos, cast_layers, grid_shape):
    ins, srcs, outs, dsts, scr, cast_scr = _split_refs(refs, 12 if has_state else 10, 3, 5, len(cast_layers))
    x_ref, g_ref, win_ref, wout_ref, ccw_ref, ccb_ref, clg_ref, clb_ref, dproj_ref, dscale_ref = ins[:10]
    statec_ref, stated_ref = (ins[10], ins[11]) if has_state else (None, None)
    out_ref, newc_ref, newd_ref = outs
    xn_ref, xpc_ref, xpd_ref, ycat_ref, zb_ref = scr
    finish_cast = _maybe_cast_weights(cast_layers, grid_shape, srcs, dsts, cast_scr)

    rb = ROW_BLOCK
    hist_c = (C_CONV - 1) * stride
    hc_rows = _round_up(hist_c, SUBLANES)
    hist_d = POOL_BUF * stride
    hd_rows = _round_up(hist_d, SUBLANES)
    first = pl.program_id(1) == 0
    _init_history(xpc_ref, statec_ref, hist_c, first)
    _init_history(xpd_ref, stated_ref, hist_d, first)
    ccb = ccb_ref[...]
    clg = clg_ref[...]
    clb = clb_ref[...]
    pos0 = start_pos + pl.program_id(1) * tm
    sub = min(tm, ODD_SUB_TILE)
    n_sub = tm // sub
    mc = MXU_COLS


    def in_proj_pieces(s0):
        srows = slice(s0, s0 + sub)

        def glu_piece(c0):
            def emit():
                xn = xn_ref[srows, :]
                xpc_ref[hc_rows + s0:hc_rows + s0 + sub, c0:c0 + mc] = _dot(xn, win_ref[:, c0:c0 + mc])
                zb = _dot(xn, win_ref[:, C_WIDTH + c0:C_WIDTH + c0 + mc])
                zb_ref[s0:s0 + sub, c0:c0 + mc] = zb
                return zb[0:1, 0:LANES]
            return emit

        def pool_in_piece(c0):
            def emit():
                zp = _dot(xn_ref[srows, :], win_ref[:, 2 * C_WIDTH + c0:2 * C_WIDTH + c0 + mc])
                xpd_ref[hd_rows + s0:hd_rows + s0 + sub, c0:c0 + mc] = zp
                return zp[0:1, 0:LANES]
            return emit

        return ([glu_piece(c0) for c0 in range(0, C_WIDTH, mc)]
                + [pool_in_piece(c0) for c0 in range(0, D_WIDTH, mc)])

    gated = {}

    def conv_block(r0):
        def emit(after):
            s0 = r0 // sub * sub
            lo = gated.get(s0, s0)
            hi = min(r0 + rb + SUBLANES, s0 + sub)
            for g0 in range(lo, hi, rb):
                grows = slice(hc_rows + g0, hc_rows + min(g0 + rb, hi))
                xpc_ref[grows, :] = xpc_ref[grows, :] * jax.nn.sigmoid(zb_ref[g0:min(g0 + rb, hi), :])
            gated[s0] = hi
            parts = []
            for c0 in range(0, C_WIDTH, LANES):
                cols = slice(c0, c0 + LANES)
                if stride % SUBLANES == 0:
                    parts.append(_conv_taps(xpc_ref, ccw_ref, r0, cols, C_CONV, stride, rb))
                else:
                    parts.append(_conv_long_stride1(xpc_ref, ccw_ref, r0, cols, C_CONV, rb))
            cc = jnp.concatenate(parts, axis=-1) + _ordered_after(ccb, after)
            ycat_ref[r0:r0 + rb, 0:C_WIDTH] = jax.nn.silu(_ln_rows(cc, clg, clb)).astype(_BF16)
        return emit

    def pool_block(r0):
        def emit(after):
            for gi, win in enumerate(POOL_WINDOWS):
                cols = slice(gi * D_GROUP_DIM, (gi + 1) * D_GROUP_DIM)
                if stride == 1:
                    static_pos = None if r0 < POOL_BUF else POOL_BUF
                else:
                    static_pos = start_pos + r0 // stride
                pooled = _pool_block(xpd_ref, r0, gi, win, stride, rb, pos0, static_pos)
                diff = (pooled - xpd_ref[hd_rows + r0:hd_rows + r0 + rb, cols]).astype(_BF16)
                scale = dscale_ref[:, cols]
                if gi == 0:
                    scale = _ordered_after(scale, after)
                yd = _dot(diff, dproj_ref[gi]) * scale
                ycat_ref[r0:r0 + rb, C_WIDTH + gi * D_GROUP_DIM:C_WIDTH + (gi + 1) * D_GROUP_DIM] = yd.astype(_BF16)
        return emit

    def out_proj_pieces(s0):
        srows = slice(s0, s0 + sub)

        def piece(c0):
            def emit():
                y = _dot(ycat_ref[srows, :], wout_ref[:, c0:c0 + mc])
                out_ref[srows, c0:c0 + mc] = x_ref[srows, c0:c0 + mc] + y
                return y[0:1, 0:LANES]
            return emit

        return [piece(c0) for c0 in range(0, D_MODEL, mc)]

    _norm_to_bf16(x_ref, g_ref, xn_ref, 0, sub)
    for emit in in_proj_pieces(0):
        emit()
    for si in range(n_sub):
        s0 = si * sub
        xpc_ref[hc_rows + s0 + sub:hc_rows + s0 + sub + SUBLANES, :] = jnp.zeros((SUBLANES, C_WIDTH), _F32)
        side = []
        if si + 1 < n_sub:
            _norm_to_bf16(x_ref, g_ref, xn_ref, s0 + sub, sub)
            side += in_proj_pieces(s0 + sub)
        if si >= 1:
            side += out_proj_pieces(s0 - sub)
        blocks = [conv_block(r0) for r0 in range(s0, s0 + sub, rb)] + [pool_block(r0) for r0 in range(s0, s0 + sub, rb)]
        after = None
        for emit in blocks:
            emit(after)
            after = side.pop(0)() if side else None
        for emit in side:
            emit()
    for emit in out_proj_pieces((n_sub - 1) * sub):
        emit()
    _emit_history(xpc_ref, newc_ref, hist_c, tm, carry=not has_state)
    _emit_history(xpd_ref, newd_ref, hist_d, tm, carry=not has_state)
    finish_cast()


def _ffn_kernel(*refs, stride, has_state, final_norm):
    refs = list(refs)
    x_ref, g_ref, win_ref, wout_ref, cw_ref = refs[:5]
    del refs[:5]
    state_ref = refs.pop(0) if has_state else None
    gfin_ref = refs.pop(0) if final_norm else None
    out_ref, newf_ref, xn_ref, gp_ref, hist_ref, h_ref = refs
    rb = ROW_BLOCK
    hist = (FFN_CONV - 1) * stride
    h_rows = _round_up(hist, SUBLANES)
    _init_history(hist_ref, state_ref, hist, pl.program_id(1) == 0)
    n_sub, sub, _ = x_ref.shape

    def sub_tile(i, carry):
        xs = x_ref.at[i]
        os = out_ref.at[i]
        _norm_to_bf16(xs, g_ref, xn_ref, 0, sub)
        xn = xn_ref[...]
        for j, c0 in enumerate(range(0, D_FF, FF_CHUNK)):
            cols = slice(c0, c0 + FF_CHUNK)
            slot = slice((j % 2) * FF_CHUNK, (j % 2 + 1) * FF_CHUNK)
            gp_ref[0:h_rows, slot] = hist_ref[:, cols]
            zg = _dot(xn, win_ref[:, c0:c0 + FF_CHUNK])
            zu = _dot(xn, win_ref[:, D_FF + c0:D_FF + c0 + FF_CHUNK])
            for r0 in range(0, sub, rb):
                rows = slice(r0, r0 + rb)
                gp_ref[h_rows + r0:h_rows + r0 + rb, slot] = zg[rows]
                conv = _conv_taps(gp_ref, cw_ref, r0, cols, FFN_CONV, stride, rb, xcols=slot)
                h_ref[rows, cols] = (jax.nn.gelu(conv) * zu[rows]).astype(_BF16)
            hist_ref[:, cols] = gp_ref[sub:sub + h_rows, slot]

        y = xs[...] + _dot(h_ref[...], wout_ref[...])
        if final_norm:
            gfin = gfin_ref[...]
            os[...] = y
            for r0 in range(0, sub, NORM_BLOCK):
                rows = slice(r0, r0 + NORM_BLOCK)
                os[rows, :] = _rms_rows(os[rows, :], gfin)
        else:
            os[...] = y
        return carry

    lax.fori_loop(0, n_sub, sub_tile, 0)
    newf_ref[0] = hist_ref[h_rows - hist:h_rows, :]


def _const_spec(shape):
    nd = len(shape)
    return pl.BlockSpec(shape, lambda b, t: (0,) * nd, pipeline_mode=pl.Buffered(1))


def _layer_spec(arr, layer):
    nd = arr.ndim - 1
    return pl.BlockSpec((None,) + arr.shape[1:], lambda b, t: (layer,) + (0,) * nd, pipeline_mode=pl.Buffered(1))


def _smem_spec():
    return pl.BlockSpec(memory_space=pltpu.SMEM)


def _hbm_spec():
    return pl.BlockSpec(memory_space=pl.ANY)


def _tile_spec(tm, nt, width):
    return pl.BlockSpec((tm, width), lambda b, t: (b * nt + t, 0))


def _state_spec(layer, rows, width):
    return pl.BlockSpec((None, 1, rows, width), lambda b, t: (layer, b, 0, 0))


def _new_state_spec(rows, width):
    return pl.BlockSpec((1, rows, width), lambda b, t: (b, 0, 0))


def _params():
    return pltpu.CompilerParams(dimension_semantics=("arbitrary", "arbitrary"), vmem_limit_bytes=VMEM_LIMIT)


def _cast_plumbing(cast_jobs, n_steps):
    in_specs, args, out_specs, out_shape, stage_in, stage_out = [], [], [], [], [], []
    for arr, _ in cast_jobs:
        _, r, c = arr.shape
        rc = r // n_steps
        assert rc * n_steps == r and rc % BF16_ROWS == 0, (arr.shape, n_steps)
        in_specs.append(_hbm_spec())
        args.append(arr)
        out_specs.append(_hbm_spec())
        out_shape.append(jax.ShapeDtypeStruct((r, c), _BF16))
        stage_in.append(pltpu.VMEM((2, rc, c), _F32))
        stage_out.append(pltpu.VMEM((2, rc, c), _BF16))
    scratch = stage_in + stage_out
    if cast_jobs:
        n = len(cast_jobs)
        scratch += [pltpu.SemaphoreType.DMA((n, 2)), pltpu.SemaphoreType.DMA((n, 2))]
    return in_specs, args, out_specs, out_shape, scratch, tuple(layer for _, layer in cast_jobs)


def _even_call(x2, nb, nt, stride, state, layer, win, wout, prm, cast_jobs, name):
    has_state = state is not None
    i = layer // 2
    tm = x2.shape[0] // (nb * nt)
    hist = (B_CONV - 1) * stride
    h_rows = _round_up(hist, SUBLANES)
    c_in, c_args, c_out_specs, c_out_shape, c_scratch, cast_layers = _cast_plumbing(cast_jobs, nb * nt)
    in_specs = [_tile_spec(tm, nt, D_MODEL), _layer_spec(prm['norm_mix_g'], layer), _const_spec(win.shape),
                _const_spec(wout.shape), _layer_spec(prm['a_ln_g'], i), _layer_spec(prm['b_conv_w'], i)]
    args = [x2, prm['norm_mix_g'], win, wout, prm['a_ln_g'], prm['b_conv_w']]
    if has_state:
        in_specs += [_smem_spec(), _smem_spec(), _state_spec(i, hist, B_WIDTH)]
        args += [prm['a_ws_steps'], prm['a_bs_steps'], state]
    else:
        in_specs += [_layer_spec(prm['a_ws'], i), _layer_spec(prm['a_bs_t'], i)]
        args += [prm['a_ws'], prm['a_bs_t']]
    out_shape = [jax.ShapeDtypeStruct(x2.shape, _F32), jax.ShapeDtypeStruct((nb, hist, B_WIDTH), _F32)]
    out_specs = [_tile_spec(tm, nt, D_MODEL), _new_state_spec(hist, B_WIDTH)]
    if has_state:
        out_shape.append(jax.ShapeDtypeStruct((x2.shape[0], A_WIDTH), _F32))
        out_specs.append(_tile_spec(tm, nt, A_WIDTH))
    scratch = [pltpu.VMEM((tm, D_MODEL), _BF16), pltpu.VMEM((h_rows + tm, B_WIDTH), _F32),
               pltpu.VMEM((tm, A_WIDTH + B_WIDTH), _BF16), pltpu.VMEM((tm, A_WIDTH), _BF16)]
    return pl.pallas_call(
        functools.partial(_even_kernel, tm=tm, stride=stride, has_state=has_state, gate_layer=i,
                          cast_layers=cast_layers, grid_shape=(nb, nt)),
        grid=(nb, nt), in_specs=in_specs + c_in, out_specs=out_specs + c_out_specs,
        out_shape=out_shape + c_out_shape, scratch_shapes=scratch + c_scratch,
        compiler_params=_params(), name=name)(*args, *c_args)


def _odd_call(x2, nb, nt, stride, state_c, state_d, start_pos, layer, win, wout, prm, cast_jobs, name):
    has_state = state_c is not None
    i = layer // 2
    hist_c = (C_CONV - 1) * stride
    hist_d = POOL_BUF * stride
    hc_rows = _round_up(hist_c, SUBLANES)
    hd_rows = _round_up(hist_d, SUBLANES)
    tm = x2.shape[0] // (nb * nt)
    c_in, c_args, c_out_specs, c_out_shape, c_scratch, cast_layers = _cast_plumbing(cast_jobs, nb * nt)
    names = ['c_conv_w', 'c_conv_b', 'c_ln_g', 'c_ln_b', 'd_proj', 'd_scale']
    in_specs = [_tile_spec(tm, nt, D_MODEL), _layer_spec(prm['norm_mix_g'], layer), _const_spec(win.shape),
                _const_spec(wout.shape)] + [_layer_spec(prm[k], i) for k in names]
    args = [x2, prm['norm_mix_g'], win, wout] + [prm[k] for k in names]
    if has_state:
        in_specs += [_state_spec(i, hist_c, C_WIDTH), _state_spec(i, hist_d, D_WIDTH)]
        args += [state_c, state_d]
    out_shape = [jax.ShapeDtypeStruct(x2.shape, _F32), jax.ShapeDtypeStruct((nb, hist_c, C_WIDTH), _F32),
                 jax.ShapeDtypeStruct((nb, hist_d, D_WIDTH), _F32)]
    out_specs = [_tile_spec(tm, nt, D_MODEL), _new_state_spec(hist_c, C_WIDTH), _new_state_spec(hist_d, D_WIDTH)]
    scratch = [pltpu.VMEM((tm, D_MODEL), _BF16), pltpu.VMEM((hc_rows + tm + SUBLANES, C_WIDTH), _F32),
               pltpu.VMEM((hd_rows + tm, D_WIDTH), _F32), pltpu.VMEM((tm, C_WIDTH + D_WIDTH), _BF16),
               pltpu.VMEM((tm, C_WIDTH), _F32)]
    return pl.pallas_call(
        functools.partial(_odd_kernel, tm=tm, stride=stride, has_state=has_state, start_pos=start_pos,
                          cast_layers=cast_layers, grid_shape=(nb, nt)),
        grid=(nb, nt), in_specs=in_specs + c_in, out_specs=out_specs + c_out_specs,
        out_shape=out_shape + c_out_shape, scratch_shapes=scratch + c_scratch,
        compiler_params=_params(), name=name)(*args, *c_args)


def _ffn_call(x2, nb, nt, stride, state, layer, final_norm, win, wout, prm, name):
    has_state = state is not None
    hist = (FFN_CONV - 1) * stride
    h_rows = _round_up(hist, SUBLANES)
    tm = x2.shape[0] // (nb * nt)
    sub = min(tm, SUB_TILE)
    n_sub = tm // sub
    x3 = x2.reshape(-1, sub, D_MODEL)
    x_spec = pl.BlockSpec((n_sub, sub, D_MODEL), lambda b, t: (b * nt + t, 0, 0))
    in_specs = [x_spec, _layer_spec(prm['norm_ffn_g'], layer), _const_spec(win.shape), _const_spec(wout.shape),
                _layer_spec(prm['ffn_conv_w'], layer)]
    args = [x3, prm['norm_ffn_g'], win, wout, prm['ffn_conv_w']]
    if has_state:
        in_specs.append(_state_spec(layer, hist, D_FF))
        args.append(state)
    if final_norm:
        in_specs.append(_const_spec((1, D_MODEL)))
        args.append(prm['norm_final_g'])
    out_shape = [jax.ShapeDtypeStruct(x3.shape, _F32), jax.ShapeDtypeStruct((nb, hist, D_FF), _F32)]
    out_specs = [x_spec, _new_state_spec(hist, D_FF)]
    scratch = [pltpu.VMEM((sub, D_MODEL), _BF16), pltpu.VMEM((h_rows + sub, 2 * FF_CHUNK), _F32),
               pltpu.VMEM((h_rows, D_FF), _F32), pltpu.VMEM((sub, D_FF), _BF16)]
    y3, newf = pl.pallas_call(
        functools.partial(_ffn_kernel, stride=stride, has_state=has_state, final_norm=final_norm),
        grid=(nb, nt), in_specs=in_specs, out_specs=out_specs, out_shape=out_shape,
        scratch_shapes=scratch, compiler_params=_params(), name=name)(*args)
    return y3.reshape(x2.shape), newf


def _to_tiles(a, ns):
    *lead, n, l, c = a.shape
    k = len(lead)
    a = a.reshape(*lead, n // ns, ns, l, c)
    a = a.transpose(*range(k), k, k + 2, k + 1, k + 3)
    return a.reshape(*lead, n // ns, l * ns, c)


def _from_tiles(a, ns, l):
    *lead, nb, _, c = a.shape
    k = len(lead)
    a = a.reshape(*lead, nb, l, ns, c)
    a = a.transpose(*range(k), k, k + 2, k + 1, k + 3)
    return a.reshape(*lead, nb * ns, l, c)


def _trunk(x2, nb, nt, stride, states, start_pos, prm, wb, raw, tag):
    new_a, new_b, new_c, new_d, new_f = [], [], [], [], []
    depth = prm['ffn_conv_w'].shape[0]
    for layer in range(depth):
        cast_keys, cast_jobs = [], []
        if raw is not None:
            cast_keys.append(('ffn', layer))
            cast_jobs += [(raw['w_ffn_in'], layer), (raw['w_ffn_out'], layer)]
            if layer + 1 < depth:
                nxt = 'even' if (layer + 1) % 2 == 0 else 'odd'
                cast_keys.append(('mix', layer + 1))
                cast_jobs += [(raw['w_in_' + nxt], (layer + 1) // 2), (raw['w_out_' + nxt], (layer + 1) // 2)]
        win, wout = wb[('mix', layer)]
        if layer % 2 == 0:
            st = None if states is None else states['b']
            res = _even_call(x2, nb, nt, stride, st, layer, win, wout, prm, cast_jobs, name=f"{tag}_mix{layer}")
            n_own = 2 if states is None else 3
            x2 = res[0]
            new_b.append(res[1])
            if states is not None:
                new_a.append(res[2])
        else:
            st_c = None if states is None else states['c']
            st_d = None if states is None else states['d']
            res = _odd_call(x2, nb, nt, stride, st_c, st_d, start_pos, layer, win, wout, prm, cast_jobs,
                            name=f"{tag}_mix{layer}")
            n_own = 3
            x2 = res[0]
            new_c.append(res[1])
            new_d.append(res[2])
        for k, key in enumerate(cast_keys):
            wb[key] = (res[n_own + 2 * k], res[n_own + 2 * k + 1])
        st_f = None if states is None else states['f']
        win, wout = wb[('ffn', layer)]
        x2, nf = _ffn_call(x2, nb, nt, stride, st_f, layer, layer == depth - 1, win, wout, prm,
                           name=f"{tag}_ffn{layer}")
        new_f.append(nf)
    return x2, new_a, new_b, new_c, new_d, new_f


def kernel(x_prompt, x_sample, state_b_conv, state_c_conv, state_d_pool, state_ffn_conv, norm_mix_g, norm_ffn_g, norm_final_g, w_in_even, w_out_even, a_ln_g, a_ws, a_bs, b_conv_w, w_in_odd, w_out_odd, c_conv_w, c_conv_b, c_ln_g, c_ln_b, d_proj, d_scale, w_ffn_in, ffn_conv_w, w_ffn_out):
    batch, seq, d = x_prompt.shape
    dec_batch, dec_seq, _ = x_sample.shape
    ns = SAMPLE_TILE // dec_seq

    def rows(p):
        return p.reshape(p.shape[0], 1, p.shape[1])

    prm = dict(norm_mix_g=rows(norm_mix_g), norm_ffn_g=rows(norm_ffn_g), norm_final_g=norm_final_g.reshape(1, -1),
               a_ln_g=rows(a_ln_g), a_ws=a_ws, a_bs_t=jnp.swapaxes(a_bs, 1, 2),
               a_ws_steps=a_ws[:, :, :dec_seq, :dec_seq].reshape(-1), a_bs_steps=a_bs[:, :, :dec_seq].reshape(-1),
               b_conv_w=b_conv_w,
               c_conv_w=jnp.pad(c_conv_w, ((0, 0), (0, _round_up(C_CONV, SUBLANES) - C_CONV), (0, 0))),
               c_conv_b=rows(c_conv_b), c_ln_g=rows(c_ln_g), c_ln_b=rows(c_ln_b), d_proj=d_proj.astype(_BF16),
               d_scale=rows(d_scale), ffn_conv_w=ffn_conv_w)
    raw = dict(w_in_even=w_in_even, w_out_even=w_out_even, w_in_odd=w_in_odd, w_out_odd=w_out_odd,
               w_ffn_in=w_ffn_in, w_ffn_out=w_ffn_out)
    wb = {('mix', 0): (w_in_even[0].astype(_BF16), w_out_even[0].astype(_BF16))}

    yp, _, b_p, c_p, d_p, f_p = _trunk(x_prompt.reshape(batch * seq, d), batch, seq // PROMPT_TILE, 1, None, 0,
                                       prm, wb, raw, "p")

    states = dict(b=_to_tiles(state_b_conv, ns), c=_to_tiles(state_c_conv, ns), d=_to_tiles(state_d_pool, ns),
                  f=_to_tiles(state_ffn_conv, ns))
    xs = _to_tiles(x_sample, ns).reshape(dec_batch * dec_seq, d)
    ys, a_s, b_s, c_s, d_s, f_s = _trunk(xs, dec_batch // ns, 1, ns, states, PAST_LEN, prm, wb, None, "s")

    def untile(parts, l):
        return _from_tiles(jnp.stack(parts), ns, l)

    y_prompt = yp.reshape(batch, seq, d)
    y_sample = _from_tiles(ys.reshape(dec_batch // ns, dec_seq * ns, d), ns, dec_seq)
    new_a = untile([a.reshape(dec_batch // ns, dec_seq * ns, A_WIDTH) for a in a_s], dec_seq)
    return (y_prompt, y_sample, new_a, jnp.stack(b_p), untile(b_s, B_CONV - 1), jnp.stack(c_p),
            untile(c_s, C_CONV - 1), jnp.stack(d_p), untile(d_s, POOL_BUF), jnp.stack(f_p),
            untile(f_s, FFN_CONV - 1))
```

```python
import functools

import jax
import jax.numpy as jnp
from jax import lax
from jax.experimental import pallas as pl
from jax.experimental.pallas import tpu as pltpu

D_MODEL = 1024
CHUNK = 128
A_HEADS = 4
A_WIDTH = 512
B_WIDTH = 512
B_CONV = 3
C_WIDTH = 512
C_CONV = 31
D_WIDTH = 512
POOL_WINDOWS = (2, 4, 8, 16)
D_GROUP_DIM = 128
POOL_BUF = 15
D_FF = 2816
FFN_CONV = 3
PAST_LEN = 16384
EPS = 1e-6

SUBLANES = 8
LANES = 128
BF16_ROWS = 16
MXU_COLS = 256
PROMPT_TILE = 1024
SAMPLE_TILE = 512
SUB_TILE = 512
ODD_SUB_TILE = 512
ROW_BLOCK = 64
NORM_BLOCK = 32
FF_CHUNK = 256
VMEM_LIMIT = 56 * 1024 * 1024

_BF16 = jnp.bfloat16
_F32 = jnp.float32


def _round_up(n, m):
    return (n + m - 1) // m * m


def _dot(a, b):
    return jnp.dot(a, b, preferred_element_type=_F32)


def _rms_rows(x, g):
    y = x * lax.rsqrt(jnp.mean(x * x, axis=-1, keepdims=True) + EPS)
    return y * g


def _ln_rows(x, g, b=None):
    mu = jnp.mean(x, axis=-1, keepdims=True)
    xc = x - mu
    y = xc * lax.rsqrt(jnp.mean(xc * xc, axis=-1, keepdims=True) + EPS) * g
    if b is not None:
        y = y + b
    return y


def _norm_to_bf16(x_ref, g_ref, xn_ref, s0, sub):
    g = g_ref[...]
    for r0 in range(s0, s0 + sub, NORM_BLOCK):
        rows = slice(r0, r0 + NORM_BLOCK)
        xn_ref[rows, :] = _rms_rows(x_ref[rows, :], g).astype(_BF16)


def _init_history(xp_ref, state_ref, hist, first_tile):
    h_rows = _round_up(hist, SUBLANES)
    if state_ref is None:
        @pl.when(first_tile)
        def _():
            xp_ref[0:h_rows, :] = jnp.zeros((h_rows, xp_ref.shape[1]), _F32)
    else:
        xp_ref[h_rows - hist:h_rows, :] = state_ref[0]


def _emit_history(xp_ref, new_ref, hist, tm, carry):
    h_rows = _round_up(hist, SUBLANES)
    tail = xp_ref[h_rows + tm - hist:h_rows + tm, :]
    new_ref[0] = tail
    if carry:
        xp_ref[h_rows - hist:h_rows, :] = tail


def _conv_taps(xp_ref, w_ref, r0, cols, ktaps, stride, rb, xcols=None, w_rows=None):
    xcols = cols if xcols is None else xcols
    hist = (ktaps - 1) * stride
    base = _round_up(hist, SUBLANES) - hist + r0
    acc = None
    for k in range(ktaps):
        w_k = w_ref[k:k + 1, cols] if w_rows is None else w_rows[k]
        term = w_k * xp_ref[base + k * stride:base + k * stride + rb, xcols]
        acc = term if acc is None else acc + term
    return acc


def _conv_long_stride1(xp_ref, w_ref, r0, cols, ktaps, rb):
    hist = ktaps - 1
    base = _round_up(hist, SUBLANES) - hist
    n = rb + SUBLANES
    p = None
    for r in range(SUBLANES - 1, -1, -1):
        z = None
        for k in range(ktaps):
            if (base + k) % SUBLANES != r:
                continue
            off = r0 + (base + k) - r
            term = w_ref[k:k + 1, cols] * xp_ref[off:off + n, cols]
            z = term if z is None else z + term
        if p is not None:
            p = pltpu.roll(p, n - 1, 0)
            p = p if z is None else p + z
        else:
            p = z
    return p[0:rb]


def _ordered_after(row, token):
    if token is None:
        return row
    half = jnp.uint32(16)
    bits = lax.bitcast_convert_type(token, jnp.uint32)
    zero = lax.shift_right_logical(lax.shift_right_logical(bits, half), half)
    head = lax.bitcast_convert_type(lax.bitcast_convert_type(row[:, 0:LANES], jnp.uint32) + zero, _F32)
    if row.shape[1] == LANES:
        return head
    return jnp.concatenate([head, row[:, LANES:]], axis=1)


def _cast_weights_step(step, n_steps, layers, srcs, dsts, stage_in, stage_out, sem_in, sem_out):
    n_jobs = len(layers)

    def in_copy(j, s, slot):
        rc = stage_in[j].shape[1]
        return pltpu.make_async_copy(srcs[j].at[layers[j], pl.ds(s * rc, rc), :], stage_in[j].at[slot],
                                     sem_in.at[j, slot])

    def out_copy(j, s, slot):
        rc = stage_out[j].shape[1]
        return pltpu.make_async_copy(stage_out[j].at[slot], dsts[j].at[pl.ds(s * rc, rc), :], sem_out.at[j, slot])

    slot = lax.rem(step, 2)

    @pl.when(step == 0)
    def _():
        for j in range(n_jobs):
            in_copy(j, 0, 0).start()

    for j in range(n_jobs):
        in_copy(j, step, slot).wait()

    @pl.when(step + 1 < n_steps)
    def _():
        for j in range(n_jobs):
            in_copy(j, step + 1, 1 - slot).start()

    @pl.when(step >= 2)
    def _():
        for j in range(n_jobs):
            out_copy(j, step - 2, slot).wait()

    for j in range(n_jobs):
        src = stage_in[j].at[slot]
        dst = stage_out[j].at[slot]
        for r0 in range(0, src.shape[0], BF16_ROWS):
            dst[r0:r0 + BF16_ROWS, :] = src[r0:r0 + BF16_ROWS, :].astype(_BF16)
        out_copy(j, step, slot).start()

    def finish():
        @pl.when(step == n_steps - 1)
        def _():
            for j in range(n_jobs):
                if n_steps > 1:
                    out_copy(j, step - 1, 1 - slot).wait()
                out_copy(j, step, slot).wait()

    return finish


def _split_refs(refs, n_in, n_out, n_scratch, n_jobs):
    refs = list(refs)
    a = n_in
    b = a + n_jobs
    c = b + n_out
    d = c + n_jobs
    e = d + n_scratch
    return refs[:a], refs[a:b], refs[b:c], refs[c:d], refs[d:e], refs[e:]


def _maybe_cast_weights(cast_layers, grid_shape, srcs, dsts, cast_scratch):
    n_jobs = len(cast_layers)
    if n_jobs == 0:
        return lambda: None
    stage_in = cast_scratch[:n_jobs]
    stage_out = cast_scratch[n_jobs:2 * n_jobs]
    sem_in, sem_out = cast_scratch[2 * n_jobs:]
    nb, nt = grid_shape
    step = pl.program_id(0) * nt + pl.program_id(1)
    return _cast_weights_step(step, nb * nt, cast_layers, srcs, dsts, stage_in, stage_out, sem_in, sem_out)


def _even_kernel(*refs, tm, stride, has_state, gate_layer, cast_layers, grid_shape):
    ins, srcs, outs, dsts, scr, cast_scr = _split_refs(
        refs, 9 if has_state else 8, 3 if has_state else 2, 5, len(cast_layers))
    x_ref, g_ref, win_ref, wout_ref, lng_ref, cw_ref, gw_ref, gb_ref = ins[:8]
    state_ref = ins[8] if has_state else None
    out_ref, newb_ref = outs[:2]
    v_ref = outs[2] if has_state else None
    xn_ref, xp_ref, ycat_ref, vb_ref, z_ref = scr
    finish_cast = _maybe_cast_weights(cast_layers, grid_shape, srcs, dsts, cast_scr)

    rb = ROW_BLOCK
    mc = MXU_COLS
    hist = (B_CONV - 1) * stride
    h_rows = _round_up(hist, SUBLANES)
    _init_history(xp_ref, state_ref, hist, pl.program_id(1) == 0)
    lng = lng_ref[...]
    allc = slice(0, B_WIDTH)
    conv_rows = [cw_ref[k:k + 1, :] for k in range(B_CONV)]
    if stride == 1:
        row_i = lax.broadcasted_iota(jnp.int32, (CHUNK, CHUNK), 0)
        col_i = lax.broadcasted_iota(jnp.int32, (CHUNK, CHUNK), 1)
        wts = [jnp.where(col_i <= row_i, gw_ref[h], 0.0).astype(_BF16) for h in range(A_HEADS)]
    sub = min(tm, SUB_TILE)
    n_sub = tm // sub
    u0, v0, bg0, cg0, hh0 = (k * A_WIDTH for k in range(5))


    def in_proj_pieces(si):
        srows = slice(si * sub, (si + 1) * sub)

        def piece(c0):
            def emit():
                z = _dot(xn_ref[srows, :], win_ref[:, c0:c0 + mc])
                z_ref[si % 2, :, c0:c0 + mc] = z
                return z[0:1, 0:LANES]
            return emit

        order = list(range(bg0, hh0 + B_WIDTH, mc)) + list(range(u0, bg0, mc))
        return [piece(c0) for c0 in order]

    def b_block(si, r0):
        def emit(after):
            loc = slice(r0 - si * sub, r0 - si * sub + rb)
            z = z_ref.at[si % 2]
            xp_ref[h_rows + r0:h_rows + r0 + rb, :] = z[loc, cg0:cg0 + B_WIDTH] * z[loc, hh0:hh0 + B_WIDTH]
            w_rows = conv_rows[:-1] + [_ordered_after(conv_rows[-1], after)]
            conv = _conv_taps(xp_ref, cw_ref, r0, allc, B_CONV, stride, rb, w_rows=w_rows)
            ycat_ref[r0:r0 + rb, A_WIDTH:A_WIDTH + B_WIDTH] = (z[loc, bg0:bg0 + B_WIDTH] * conv).astype(_BF16)
        return emit

    def ln_block(si, r0):
        def emit(after):
            gain = _ordered_after(lng, after)
            for q0 in range(r0, r0 + rb, NORM_BLOCK):
                rows = slice(q0, q0 + NORM_BLOCK)
                loc = slice(q0 - si * sub, q0 - si * sub + NORM_BLOCK)
                v = _ln_rows(jax.nn.gelu(z_ref[si % 2, loc, v0:v0 + A_WIDTH]), gain)
                if v_ref is not None:
                    v_ref[rows, :] = v
                else:
                    vb_ref[rows, :] = v.astype(_BF16)
        return emit

    def gate_items(si):
        s0 = si * sub
        z = z_ref.at[si % 2]
        items = []
        if stride == 1:
            for c0 in range(s0, s0 + sub, 2 * CHUNK):
                for h in range(A_HEADS):
                    def emit(after, c0=c0, h=h):
                        ra = slice(c0, c0 + CHUNK)
                        rb2 = slice(c0 + CHUNK, c0 + 2 * CHUNK)
                        cols = slice(h * CHUNK, (h + 1) * CHUNK)
                        vpair = jnp.concatenate([vb_ref[ra, cols], vb_ref[rb2, cols]], axis=1)
                        mixed = _dot(wts[h], vpair)
                        bias = gb_ref[:, h:h + 1]
                        for half, rws in enumerate((ra, rb2)):
                            m = mixed[:, half * CHUNK:(half + 1) * CHUNK] + bias
                            u = z[rws.start - s0:rws.stop - s0, u0 + h * CHUNK:u0 + (h + 1) * CHUNK]
                            ycat_ref[rws, cols] = (jax.nn.gelu(u) * m).astype(_BF16)
                    items.append(emit)
        else:
            steps = tm // stride
            gw_off = gate_layer * A_HEADS * steps * steps
            gb_off = gate_layer * A_HEADS * steps
            for t in range(s0 // stride, (s0 + sub) // stride):
                def emit(after, t=t):
                    for n0 in range(0, stride, rb):
                        rows = slice(t * stride + n0, t * stride + n0 + rb)
                        for h in range(A_HEADS):
                            cols = slice(h * CHUNK, (h + 1) * CHUNK)
                            mixed = None
                            for s in range(t + 1):
                                w = gw_ref[gw_off + (h * steps + t) * steps + s]
                                term = w * v_ref[s * stride + n0:s * stride + n0 + rb, cols]
                                mixed = term if mixed is None else mixed + term
                            mixed = mixed + gb_ref[gb_off + h * steps + t]
                            u = z[rows.start - s0:rows.stop - s0, u0 + h * CHUNK:u0 + (h + 1) * CHUNK]
                            ycat_ref[rows, cols] = (jax.nn.gelu(u) * mixed).astype(_BF16)
                items.append(emit)
        return items

    def out_proj_pieces(si):
        srows = slice(si * sub, (si + 1) * sub)

        def piece(c0):
            def emit():
                y = _dot(ycat_ref[srows, :], wout_ref[:, c0:c0 + mc])
                out_ref[srows, c0:c0 + mc] = x_ref[srows, c0:c0 + mc] + y
                return y[0:1, 0:LANES]
            return emit

        return [piece(c0) for c0 in range(0, D_MODEL, mc)]

    _norm_to_bf16(x_ref, g_ref, xn_ref, 0, sub)
    for emit in in_proj_pieces(0):
        emit()
    for si in range(n_sub):
        s0 = si * sub
        side = []
        if si + 1 < n_sub:
            _norm_to_bf16(x_ref, g_ref, xn_ref, s0 + sub, sub)
            side += in_proj_pieces(si + 1)
        if si >= 1:
            side += out_proj_pieces(si - 1)
        blocks = ([b_block(si, r0) for r0 in range(s0, s0 + sub, rb)]
                  + [ln_block(si, r0) for r0 in range(s0, s0 + sub, rb)] + gate_items(si))
        after = None
        for emit in blocks:
            emit(after)
            after = side.pop(0)() if side else None
        for emit in side:
            emit()
    for emit in out_proj_pieces(n_sub - 1):
        emit()
    _emit_history(xp_ref, newb_ref, hist, tm, carry=not has_state)
    finish_cast()


def _pool_block(xp_ref, r0, gi, win, stride, rb, pos0, row_pos_static):
    cols = slice(gi * D_GROUP_DIM, (gi + 1) * D_GROUP_DIM)
    hist = POOL_BUF * stride
    h_rows = _round_up(hist, SUBLANES)
    if stride % SUBLANES == 0:
        s = None
        for j in range(win):
            off = h_rows + r0 - j * stride
            term = xp_ref[off:off + rb, cols]
            s = term if s is None else s + term
    else:
        halo = 2 * SUBLANES
        ext = xp_ref[h_rows + r0 - halo:h_rows + r0 + rb, cols]
        shift = 1
        while shift < win:
            ext = ext + pltpu.roll(ext, shift, 0)
            shift *= 2
        s = ext[halo:]
    if row_pos_static is not None:
        return s / float(min(row_pos_static + 1, win))
    pos = pos0 + r0 + lax.broadcasted_iota(jnp.int32, (rb, D_GROUP_DIM), 0)
    cnt = jnp.minimum(pos + 1, win).astype(_F32)
    return s / cnt


def _odd_kernel(*refs, tm, stride, has_state, start_pos, cast_layers, grid_shape):
    ins, srcs, outs, dsts, scr, cast_scr = _split_refs(refs, 12 if has_state else 10, 3, 5, len(cast_layers))
    x_ref, g_ref, win_ref, wout_ref, ccw_ref, ccb_ref, clg_ref, clb_ref, dproj_ref, dscale_ref = ins[:10]
    statec_ref, stated_ref = (ins[10], ins[11]) if has_state else (None, None)
    out_ref, newc_ref, newd_ref = outs
    xn_ref, xpc_ref, xpd_ref, ycat_ref, zb_ref = scr
    finish_cast = _maybe_cast_weights(cast_layers, grid_shape, srcs, dsts, cast_scr)

    rb = ROW_BLOCK
    hist_c = (C_CONV - 1) * stride
    hc_rows = _round_up(hist_c, SUBLANES)
    hist_d = POOL_BUF * stride
    hd_rows = _round_up(hist_d, SUBLANES)
    first = pl.program_id(1) == 0
    _init_history(xpc_ref, statec_ref, hist_c, first)
    _init_history(xpd_ref, stated_ref, hist_d, first)
    ccb = ccb_ref[...]
    clg = clg_ref[...]
    clb = clb_ref[...]
    pos0 = start_pos + pl.program_id(1) * tm
    sub = min(tm, ODD_SUB_TILE)
    n_sub = tm // sub
    mc = MXU_COLS


    def in_proj_pieces(s0):
        srows = slice(s0, s0 + sub)

        def glu_piece(c0):
            def emit():
                xn = xn_ref[srows, :]
                xpc_ref[hc_rows + s0:hc_rows + s0 + sub, c0:c0 + mc] = _dot(xn, win_ref[:, c0:c0 + mc])
                zb = _dot(xn, win_ref[:, C_WIDTH + c0:C_WIDTH + c0 + mc])
                zb_ref[s0:s0 + sub, c0:c0 + mc] = zb
                return zb[0:1, 0:LANES]
            return emit

        def pool_in_piece(c0):
            def emit():
                zp = _dot(xn_ref[srows, :], win_ref[:, 2 * C_WIDTH + c0:2 * C_WIDTH + c0 + mc])
                xpd_ref[hd_rows + s0:hd_rows + s0 + sub, c0:c0 + mc] = zp
                return zp[0:1, 0:LANES]
            return emit

        return ([glu_piece(c0) for c0 in range(0, C_WIDTH, mc)]
                + [pool_in_piece(c0) for c0 in range(0, D_WIDTH, mc)])

    gated = {}

    def conv_block(r0):
        def emit(after):
            s0 = r0 // sub * sub
            lo = gated.get(s0, s0)
            hi = min(r0 + rb + SUBLANES, s0 + sub)
            for g0 in range(lo, hi, rb):
                grows = slice(hc_rows + g0, hc_rows + min(g0 + rb, hi))
                xpc_ref[grows, :] = xpc_ref[grows, :] * jax.nn.sigmoid(zb_ref[g0:min(g0 + rb, hi), :])
            gated[s0] = hi
            parts = []
            for c0 in range(0, C_WIDTH, LANES):
                cols = slice(c0, c0 + LANES)
                if stride % SUBLANES == 0:
                    parts.append(_conv_taps(xpc_ref, ccw_ref, r0, cols, C_CONV, stride, rb))
                else:
                    parts.append(_conv_long_stride1(xpc_ref, ccw_ref, r0, cols, C_CONV, rb))
            cc = jnp.concatenate(parts, axis=-1) + _ordered_after(ccb, after)
            ycat_ref[r0:r0 + rb, 0:C_WIDTH] = jax.nn.silu(_ln_rows(cc, clg, clb)).astype(_BF16)
        return emit

    def pool_block(r0):
        def emit(after):
            for gi, win in enumerate(POOL_WINDOWS):
                cols = slice(gi * D_GROUP_DIM, (gi + 1) * D_GROUP_DIM)
                if stride == 1:
                    static_pos = None if r0 < POOL_BUF else POOL_BUF
                else:
                    static_pos = start_pos + r0 // stride
                pooled = _pool_block(xpd_ref, r0, gi, win, stride, rb, pos0, static_pos)
                diff = (pooled - xpd_ref[hd_rows + r0:hd_rows + r0 + rb, cols]).astype(_BF16)
                scale = dscale_ref[:, cols]
                if gi == 0:
                    scale = _ordered_after(scale, after)
                yd = _dot(diff, dproj_ref[gi]) * scale
                ycat_ref[r0:r0 + rb, C_WIDTH + gi * D_GROUP_DIM:C_WIDTH + (gi + 1) * D_GROUP_DIM] = yd.astype(_BF16)
        return emit

    def out_proj_pieces(s0):
        srows = slice(s0, s0 + sub)

        def piece(c0):
            def emit():
                y = _dot(ycat_ref[srows, :], wout_ref[:, c0:c0 + mc])
                out_ref[srows, c0:c0 + mc] = x_ref[srows, c0:c0 + mc] + y
                return y[0:1, 0:LANES]
            return emit

        return [piece(c0) for c0 in range(0, D_MODEL, mc)]

    _norm_to_bf16(x_ref, g_ref, xn_ref, 0, sub)
    for emit in in_proj_pieces(0):
        emit()
    for si in range(n_sub):
        s0 = si * sub
        xpc_ref[hc_rows + s0 + sub:hc_rows + s0 + sub + SUBLANES, :] = jnp.zeros((SUBLANES, C_WIDTH), _F32)
        side = []
        if si + 1 < n_sub:
            _norm_to_bf16(x_ref, g_ref, xn_ref, s0 + sub, sub)
            side += in_proj_pieces(s0 + sub)
        if si >= 1:
            side += out_proj_pieces(s0 - sub)
        blocks = [conv_block(r0) for r0 in range(s0, s0 + sub, rb)] + [pool_block(r0) for r0 in range(s0, s0 + sub, rb)]
        after = None
        for emit in blocks:
            emit(after)
            after = side.pop(0)() if side else None
        for emit in side:
            emit()
    for emit in out_proj_pieces((n_sub - 1) * sub):
        emit()
    _emit_history(xpc_ref, newc_ref, hist_c, tm, carry=not has_state)
    _emit_history(xpd_ref, newd_ref, hist_d, tm, carry=not has_state)
    finish_cast()


def _ffn_kernel(*refs, stride, has_state, final_norm):
    refs = list(refs)
    x_ref, g_ref, win_ref, wout_ref, cw_ref = refs[:5]
    del refs[:5]
    state_ref = refs.pop(0) if has_state else None
    gfin_ref = refs.pop(0) if final_norm else None
    out_ref, newf_ref, xn_ref, gp_ref, hist_ref, h_ref = refs
    rb = ROW_BLOCK
    hist = (FFN_CONV - 1) * stride
    h_rows = _round_up(hist, SUBLANES)
    _init_history(hist_ref, state_ref, hist, pl.program_id(1) == 0)
    n_sub, sub, _ = x_ref.shape

    def sub_tile(i, carry):
        xs = x_ref.at[i]
        os = out_ref.at[i]
        _norm_to_bf16(xs, g_ref, xn_ref, 0, sub)
        xn = xn_ref[...]
        for j, c0 in enumerate(range(0, D_FF, FF_CHUNK)):
            cols = slice(c0, c0 + FF_CHUNK)
            slot = slice((j % 2) * FF_CHUNK, (j % 2 + 1) * FF_CHUNK)
            gp_ref[0:h_rows, slot] = hist_ref[:, cols]
            zg = _dot(xn, win_ref[:, c0:c0 + FF_CHUNK])
            zu = _dot(xn, win_ref[:, D_FF + c0:D_FF + c0 + FF_CHUNK])
            for r0 in range(0, sub, rb):
                rows = slice(r0, r0 + rb)
                gp_ref[h_rows + r0:h_rows + r0 + rb, slot] = zg[rows]
                conv = _conv_taps(gp_ref, cw_ref, r0, cols, FFN_CONV, stride, rb, xcols=slot)
                h_ref[rows, cols] = (jax.nn.gelu(conv) * zu[rows]).astype(_BF16)
            hist_ref[:, cols] = gp_ref[sub:sub + h_rows, slot]

        y = xs[...] + _dot(h_ref[...], wout_ref[...])
        if final_norm:
            gfin = gfin_ref[...]
            os[...] = y
            for r0 in range(0, sub, NORM_BLOCK):
                rows = slice(r0, r0 + NORM_BLOCK)
                os[rows, :] = _rms_rows(os[rows, :], gfin)
        else:
            os[...] = y
        return carry

    lax.fori_loop(0, n_sub, sub_tile, 0)
    newf_ref[0] = hist_ref[h_rows - hist:h_rows, :]


def _const_spec(shape):
    nd = len(shape)
    return pl.BlockSpec(shape, lambda b, t: (0,) * nd, pipeline_mode=pl.Buffered(1))


def _layer_spec(arr, layer):
    nd = arr.ndim - 1
    return pl.BlockSpec((None,) + arr.shape[1:], lambda b, t: (layer,) + (0,) * nd, pipeline_mode=pl.Buffered(1))


def _smem_spec():
    return pl.BlockSpec(memory_space=pltpu.SMEM)


def _hbm_spec():
    return pl.BlockSpec(memory_space=pl.ANY)


def _tile_spec(tm, nt, width):
    return pl.BlockSpec((tm, width), lambda b, t: (b * nt + t, 0))


def _state_spec(layer, rows, width):
    return pl.BlockSpec((None, 1, rows, width), lambda b, t: (layer, b, 0, 0))


def _new_state_spec(rows, width):
    return pl.BlockSpec((1, rows, width), lambda b, t: (b, 0, 0))


def _params():
    return pltpu.CompilerParams(dimension_semantics=("arbitrary", "arbitrary"), vmem_limit_bytes=VMEM_LIMIT)


def _cast_plumbing(cast_jobs, n_steps):
    in_specs, args, out_specs, out_shape, stage_in, stage_out = [], [], [], [], [], []
    for arr, _ in cast_jobs:
        _, r, c = arr.shape
        rc = r // n_steps
        assert rc * n_steps == r and rc % BF16_ROWS == 0, (arr.shape, n_steps)
        in_specs.append(_hbm_spec())
        args.append(arr)
        out_specs.append(_hbm_spec())
        out_shape.append(jax.ShapeDtypeStruct((r, c), _BF16))
        stage_in.append(pltpu.VMEM((2, rc, c), _F32))
        stage_out.append(pltpu.VMEM((2, rc, c), _BF16))
    scratch = stage_in + stage_out
    if cast_jobs:
        n = len(cast_jobs)
        scratch += [pltpu.SemaphoreType.DMA((n, 2)), pltpu.SemaphoreType.DMA((n, 2))]
    return in_specs, args, out_specs, out_shape, scratch, tuple(layer for _, layer in cast_jobs)


def _even_call(x2, nb, nt, stride, state, layer, win, wout, prm, cast_jobs, name):
    has_state = state is not None
    i = layer // 2
    tm = x2.shape[0] // (nb * nt)
    hist = (B_CONV - 1) * stride
    h_rows = _round_up(hist, SUBLANES)
    c_in, c_args, c_out_specs, c_out_shape, c_scratch, cast_layers = _cast_plumbing(cast_jobs, nb * nt)
    in_specs = [_tile_spec(tm, nt, D_MODEL), _layer_spec(prm['norm_mix_g'], layer), _const_spec(win.shape),
                _const_spec(wout.shape), _layer_spec(prm['a_ln_g'], i), _layer_spec(prm['b_conv_w'], i)]
    args = [x2, prm['norm_mix_g'], win, wout, prm['a_ln_g'], prm['b_conv_w']]
    if has_state:
        in_specs += [_smem_spec(), _smem_spec(), _state_spec(i, hist, B_WIDTH)]
        args += [prm['a_ws_steps'], prm['a_bs_steps'], state]
    else:
        in_specs += [_layer_spec(prm['a_ws'], i), _layer_spec(prm['a_bs_t'], i)]
        args += [prm['a_ws'], prm['a_bs_t']]
    out_shape = [jax.ShapeDtypeStruct(x2.shape, _F32), jax.ShapeDtypeStruct((nb, hist, B_WIDTH), _F32)]
    out_specs = [_tile_spec(tm, nt, D_MODEL), _new_state_spec(hist, B_WIDTH)]
    if has_state:
        out_shape.append(jax.ShapeDtypeStruct((x2.shape[0], A_WIDTH), _F32))
        out_specs.append(_tile_spec(tm, nt, A_WIDTH))
    sub = min(tm, SUB_TILE)
    scratch = [pltpu.VMEM((tm, D_MODEL), _BF16), pltpu.VMEM((h_rows + tm, B_WIDTH), _F32),
               pltpu.VMEM((tm, A_WIDTH + B_WIDTH), _BF16), pltpu.VMEM((tm, A_WIDTH), _BF16),
               pltpu.VMEM((min(tm // sub, 2), sub, win.shape[1]), _F32)]
    return pl.pallas_call(
        functools.partial(_even_kernel, tm=tm, stride=stride, has_state=has_state, gate_layer=i,
                          cast_layers=cast_layers, grid_shape=(nb, nt)),
        grid=(nb, nt), in_specs=in_specs + c_in, out_specs=out_specs + c_out_specs,
        out_shape=out_shape + c_out_shape, scratch_shapes=scratch + c_scratch,
        compiler_params=_params(), name=name)(*args, *c_args)


def _odd_call(x2, nb, nt, stride, state_c, state_d, start_pos, layer, win, wout, prm, cast_jobs, name):
    has_state = state_c is not None
    i = layer // 2
    hist_c = (C_CONV - 1) * stride
    hist_d = POOL_BUF * stride
    hc_rows = _round_up(hist_c, SUBLANES)
    hd_rows = _round_up(hist_d, SUBLANES)
    tm = x2.shape[0] // (nb * nt)
    c_in, c_args, c_out_specs, c_out_shape, c_scratch, cast_layers = _cast_plumbing(cast_jobs, nb * nt)
    names = ['c_conv_w', 'c_conv_b', 'c_ln_g', 'c_ln_b', 'd_proj', 'd_scale']
    in_specs = [_tile_spec(tm, nt, D_MODEL), _layer_spec(prm['norm_mix_g'], layer), _const_spec(win.shape),
                _const_spec(wout.shape)] + [_layer_spec(prm[k], i) for k in names]
    args = [x2, prm['norm_mix_g'], win, wout] + [prm[k] for k in names]
    if has_state:
        in_specs += [_state_spec(i, hist_c, C_WIDTH), _state_spec(i, hist_d, D_WIDTH)]
        args += [state_c, state_d]
    out_shape = [jax.ShapeDtypeStruct(x2.shape, _F32), jax.ShapeDtypeStruct((nb, hist_c, C_WIDTH), _F32),
                 jax.ShapeDtypeStruct((nb, hist_d, D_WIDTH), _F32)]
    out_specs = [_tile_spec(tm, nt, D_MODEL), _new_state_spec(hist_c, C_WIDTH), _new_state_spec(hist_d, D_WIDTH)]
    scratch = [pltpu.VMEM((tm, D_MODEL), _BF16), pltpu.VMEM((hc_rows + tm + SUBLANES, C_WIDTH), _F32),
               pltpu.VMEM((hd_rows + tm, D_WIDTH), _F32), pltpu.VMEM((tm, C_WIDTH + D_WIDTH), _BF16),
               pltpu.VMEM((tm, C_WIDTH), _F32)]
    return pl.pallas_call(
        functools.partial(_odd_kernel, tm=tm, stride=stride, has_state=has_state, start_pos=start_pos,
                          cast_layers=cast_layers, grid_shape=(nb, nt)),
        grid=(nb, nt), in_specs=in_specs + c_in, out_specs=out_specs + c_out_specs,
        out_shape=out_shape + c_out_shape, scratch_shapes=scratch + c_scratch,
        compiler_params=_params(), name=name)(*args, *c_args)


def _ffn_call(x2, nb, nt, stride, state, layer, final_norm, win, wout, prm, name):
    has_state = state is not None
    hist = (FFN_CONV - 1) * stride
    h_rows = _round_up(hist, SUBLANES)
    tm = x2.shape[0] // (nb * nt)
    sub = min(tm, SUB_TILE)
    n_sub = tm // sub
    x3 = x2.reshape(-1, sub, D_MODEL)
    x_spec = pl.BlockSpec((n_sub, sub, D_MODEL), lambda b, t: (b * nt + t, 0, 0))
    in_specs = [x_spec, _layer_spec(prm['norm_ffn_g'], layer), _const_spec(win.shape), _const_spec(wout.shape),
                _layer_spec(prm['ffn_conv_w'], layer)]
    args = [x3, prm['norm_ffn_g'], win, wout, prm['ffn_conv_w']]
    if has_state:
        in_specs.append(_state_spec(layer, hist, D_FF))
        args.append(state)
    if final_norm:
        in_specs.append(_const_spec((1, D_MODEL)))
        args.append(prm['norm_final_g'])
    out_shape = [jax.ShapeDtypeStruct(x3.shape, _F32), jax.ShapeDtypeStruct((nb, hist, D_FF), _F32)]
    out_specs = [x_spec, _new_state_spec(hist, D_FF)]
    scratch = [pltpu.VMEM((sub, D_MODEL), _BF16), pltpu.VMEM((h_rows + sub, 2 * FF_CHUNK), _F32),
               pltpu.VMEM((h_rows, D_FF), _F32), pltpu.VMEM((sub, D_FF), _BF16)]
    y3, newf = pl.pallas_call(
        functools.partial(_ffn_kernel, stride=stride, has_state=has_state, final_norm=final_norm),
        grid=(nb, nt), in_specs=in_specs, out_specs=out_specs, out_shape=out_shape,
        scratch_shapes=scratch, compiler_params=_params(), name=name)(*args)
    return y3.reshape(x2.shape), newf


def _to_tiles(a, ns):
    *lead, n, l, c = a.shape
    k = len(lead)
    a = a.reshape(*lead, n // ns, ns, l, c)
    a = a.transpose(*range(k), k, k + 2, k + 1, k + 3)
    return a.reshape(*lead, n // ns, l * ns, c)


def _from_tiles(a, ns, l):
    *lead, nb, _, c = a.shape
    k = len(lead)
    a = a.reshape(*lead, nb, l, ns, c)
    a = a.transpose(*range(k), k, k + 2, k + 1, k + 3)
    return a.reshape(*lead, nb * ns, l, c)


def _trunk(x2, nb, nt, stride, states, start_pos, prm, wb, raw, tag):
    new_a, new_b, new_c, new_d, new_f = [], [], [], [], []
    depth = prm['ffn_conv_w'].shape[0]
    for layer in range(depth):
        cast_keys, cast_jobs = [], []
        if raw is not None:
            cast_keys.append(('ffn', layer))
            cast_jobs += [(raw['w_ffn_in'], layer), (raw['w_ffn_out'], layer)]
            if layer + 1 < depth:
                nxt = 'even' if (layer + 1) % 2 == 0 else 'odd'
                cast_keys.append(('mix', layer + 1))
                cast_jobs += [(raw['w_in_' + nxt], (layer + 1) // 2), (raw['w_out_' + nxt], (layer + 1) // 2)]
        win, wout = wb[('mix', layer)]
        if layer % 2 == 0:
            st = None if states is None else states['b']
            res = _even_call(x2, nb, nt, stride, st, layer, win, wout, prm, cast_jobs, name=f"{tag}_mix{layer}")
            n_own = 2 if states is None else 3
            x2 = res[0]
            new_b.append(res[1])
            if states is not None:
                new_a.append(res[2])
        else:
            st_c = None if states is None else states['c']
            st_d = None if states is None else states['d']
            res = _odd_call(x2, nb, nt, stride, st_c, st_d, start_pos, layer, win, wout, prm, cast_jobs,
                            name=f"{tag}_mix{layer}")
            n_own = 3
            x2 = res[0]
            new_c.append(res[1])
            new_d.append(res[2])
        for k, key in enumerate(cast_keys):
            wb[key] = (res[n_own + 2 * k], res[n_own + 2 * k + 1])
        st_f = None if states is None else states['f']
        win, wout = wb[('ffn', layer)]
        x2, nf = _ffn_call(x2, nb, nt, stride, st_f, layer, layer == depth - 1, win, wout, prm,
                           name=f"{tag}_ffn{layer}")
        new_f.append(nf)
    return x2, new_a, new_b, new_c, new_d, new_f


def kernel(x_prompt, x_sample, state_b_conv, state_c_conv, state_d_pool, state_ffn_conv, norm_mix_g, norm_ffn_g, norm_final_g, w_in_even, w_out_even, a_ln_g, a_ws, a_bs, b_conv_w, w_in_odd, w_out_odd, c_conv_w, c_conv_b, c_ln_g, c_ln_b, d_proj, d_scale, w_ffn_in, ffn_conv_w, w_ffn_out):
    batch, seq, d = x_prompt.shape
    dec_batch, dec_seq, _ = x_sample.shape
    ns = SAMPLE_TILE // dec_seq

    def rows(p):
        return p.reshape(p.shape[0], 1, p.shape[1])

    prm = dict(norm_mix_g=rows(norm_mix_g), norm_ffn_g=rows(norm_ffn_g), norm_final_g=norm_final_g.reshape(1, -1),
               a_ln_g=rows(a_ln_g), a_ws=a_ws, a_bs_t=jnp.swapaxes(a_bs, 1, 2),
               a_ws_steps=a_ws[:, :, :dec_seq, :dec_seq].reshape(-1), a_bs_steps=a_bs[:, :, :dec_seq].reshape(-1),
               b_conv_w=b_conv_w,
               c_conv_w=jnp.pad(c_conv_w, ((0, 0), (0, _round_up(C_CONV, SUBLANES) - C_CONV), (0, 0))),
               c_conv_b=rows(c_conv_b), c_ln_g=rows(c_ln_g), c_ln_b=rows(c_ln_b), d_proj=d_proj.astype(_BF16),
               d_scale=rows(d_scale), ffn_conv_w=ffn_conv_w)
    raw = dict(w_in_even=w_in_even, w_out_even=w_out_even, w_in_odd=w_in_odd, w_out_odd=w_out_odd,
               w_ffn_in=w_ffn_in, w_ffn_out=w_ffn_out)
    wb = {('mix', 0): (w_in_even[0].astype(_BF16), w_out_even[0].astype(_BF16))}

    yp, _, b_p, c_p, d_p, f_p = _trunk(x_prompt.reshape(batch * seq, d), batch, seq // PROMPT_TILE, 1, None, 0,
                                       prm, wb, raw, "p")

    states = dict(b=_to_tiles(state_b_conv, ns), c=_to_tiles(state_c_conv, ns), d=_to_tiles(state_d_pool, ns),
                  f=_to_tiles(state_ffn_conv, ns))
    xs = _to_tiles(x_sample, ns).reshape(dec_batch * dec_seq, d)
    ys, a_s, b_s, c_s, d_s, f_s = _trunk(xs, dec_batch // ns, 1, ns, states, PAST_LEN, prm, wb, None, "s")

    def untile(parts, l):
        return _from_tiles(jnp.stack(parts), ns, l)

    y_prompt = yp.reshape(batch, seq, d)
    y_sample = _from_tiles(ys.reshape(dec_batch // ns, dec_seq * ns, d), ns, dec_seq)
    new_a = untile([a.reshape(dec_batch // ns, dec_seq * ns, A_WIDTH) for a in a_s], dec_seq)
    return (y_prompt, y_sample, new_a, jnp.stack(b_p), untile(b_s, B_CONV - 1), jnp.stack(c_p),
            untile(c_s, C_CONV - 1), jnp.stack(d_p), untile(d_s, POOL_BUF), jnp.stack(f_p),
            untile(f_s, FFN_CONV - 1))
```

```python
import functools

import jax
import jax.numpy as jnp
from jax import lax
from jax.experimental import pallas as pl
from jax.experimental.pallas import tpu as pltpu

D_MODEL = 1024
CHUNK = 128
A_HEADS = 4
A_WIDTH = 512
B_WIDTH = 512
B_CONV = 3
C_WIDTH = 512
C_CONV = 31
D_WIDTH = 512
POOL_WINDOWS = (2, 4, 8, 16)
D_GROUP_DIM = 128
POOL_BUF = 15
D_FF = 2816
FFN_CONV = 3
PAST_LEN = 16384
EPS = 1e-6

SUBLANES = 8
LANES = 128
BF16_ROWS = 16
MXU_COLS = 256
PROMPT_TILE = 1024
SAMPLE_TILE = 512
SUB_TILE = 512
ODD_SUB_TILE = 512
ROW_BLOCK = 64
NORM_BLOCK = 32
FF_CHUNK = 256
VMEM_LIMIT = 56 * 1024 * 1024

_BF16 = jnp.bfloat16
_F32 = jnp.float32


def _round_up(n, m):
    return (n + m - 1) // m * m


def _dot(a, b):
    return jnp.dot(a, b, preferred_element_type=_F32)


def _rms_rows(x, g):
    y = x * lax.rsqrt(jnp.mean(x * x, axis=-1, keepdims=True) + EPS)
    return y * g


def _ln_rows(x, g, b=None):
    mu = jnp.mean(x, axis=-1, keepdims=True)
    xc = x - mu
    y = xc * lax.rsqrt(jnp.mean(xc * xc, axis=-1, keepdims=True) + EPS) * g
    if b is not None:
        y = y + b
    return y


def _norm_to_bf16(x_ref, g_ref, xn_ref, s0, sub):
    g = g_ref[...]
    for r0 in range(s0, s0 + sub, NORM_BLOCK):
        rows = slice(r0, r0 + NORM_BLOCK)
        xn_ref[rows, :] = _rms_rows(x_ref[rows, :], g).astype(_BF16)


def _init_history(xp_ref, state_ref, hist, first_tile):
    h_rows = _round_up(hist, SUBLANES)
    if state_ref is None:
        @pl.when(first_tile)
        def _():
            xp_ref[0:h_rows, :] = jnp.zeros((h_rows, xp_ref.shape[1]), _F32)
    else:
        xp_ref[h_rows - hist:h_rows, :] = state_ref[0]


def _emit_history(xp_ref, new_ref, hist, tm, carry):
    h_rows = _round_up(hist, SUBLANES)
    tail = xp_ref[h_rows + tm - hist:h_rows + tm, :]
    new_ref[0] = tail
    if carry:
        xp_ref[h_rows - hist:h_rows, :] = tail


def _conv_taps(xp_ref, w_ref, r0, cols, ktaps, stride, rb, xcols=None, w_rows=None):
    xcols = cols if xcols is None else xcols
    hist = (ktaps - 1) * stride
    base = _round_up(hist, SUBLANES) - hist + r0
    acc = None
    for k in range(ktaps):
        w_k = w_ref[k:k + 1, cols] if w_rows is None else w_rows[k]
        term = w_k * xp_ref[base + k * stride:base + k * stride + rb, xcols]
        acc = term if acc is None else acc + term
    return acc


def _conv_long_stride1(xp_ref, w_ref, r0, cols, ktaps, rb):
    hist = ktaps - 1
    base = _round_up(hist, SUBLANES) - hist
    n = rb + SUBLANES
    p = None
    for r in range(SUBLANES - 1, -1, -1):
        z = None
        for k in range(ktaps):
            if (base + k) % SUBLANES != r:
                continue
            off = r0 + (base + k) - r
            term = w_ref[k:k + 1, cols] * xp_ref[off:off + n, cols]
            z = term if z is None else z + term
        if p is not None:
            p = pltpu.roll(p, n - 1, 0)
            p = p if z is None else p + z
        else:
            p = z
    return p[0:rb]


def _ordered_after(row, token):
    if token is None:
        return row
    half = jnp.uint32(16)
    bits = lax.bitcast_convert_type(token, jnp.uint32)
    zero = lax.shift_right_logical(lax.shift_right_logical(bits, half), half)
    head = lax.bitcast_convert_type(lax.bitcast_convert_type(row[:, 0:LANES], jnp.uint32) + zero, _F32)
    if row.shape[1] == LANES:
        return head
    return jnp.concatenate([head, row[:, LANES:]], axis=1)


def _cast_weights_step(step, n_steps, layers, srcs, dsts, stage_in, stage_out, sem_in, sem_out):
    n_jobs = len(layers)

    def in_copy(j, s, slot):
        rc = stage_in[j].shape[1]
        return pltpu.make_async_copy(srcs[j].at[layers[j], pl.ds(s * rc, rc), :], stage_in[j].at[slot],
                                     sem_in.at[j, slot])

    def out_copy(j, s, slot):
        rc = stage_out[j].shape[1]
        return pltpu.make_async_copy(stage_out[j].at[slot], dsts[j].at[pl.ds(s * rc, rc), :], sem_out.at[j, slot])

    slot = lax.rem(step, 2)

    @pl.when(step == 0)
    def _():
        for j in range(n_jobs):
            in_copy(j, 0, 0).start()

    for j in range(n_jobs):
        in_copy(j, step, slot).wait()

    @pl.when(step + 1 < n_steps)
    def _():
        for j in range(n_jobs):
            in_copy(j, step + 1, 1 - slot).start()

    @pl.when(step >= 2)
    def _():
        for j in range(n_jobs):
            out_copy(j, step - 2, slot).wait()

    for j in range(n_jobs):
        src = stage_in[j].at[slot]
        dst = stage_out[j].at[slot]
        for r0 in range(0, src.shape[0], BF16_ROWS):
            dst[r0:r0 + BF16_ROWS, :] = src[r0:r0 + BF16_ROWS, :].astype(_BF16)
        out_copy(j, step, slot).start()

    def finish():
        @pl.when(step == n_steps - 1)
        def _():
            for j in range(n_jobs):
                if n_steps > 1:
                    out_copy(j, step - 1, 1 - slot).wait()
                out_copy(j, step, slot).wait()

    return finish


def _split_refs(refs, n_in, n_out, n_scratch, n_jobs):
    refs = list(refs)
    a = n_in
    b = a + n_jobs
    c = b + n_out
    d = c + n_jobs
    e = d + n_scratch
    return refs[:a], refs[a:b], refs[b:c], refs[c:d], refs[d:e], refs[e:]


def _maybe_cast_weights(cast_layers, grid_shape, srcs, dsts, cast_scratch):
    n_jobs = len(cast_layers)
    if n_jobs == 0:
        return lambda: None
    stage_in = cast_scratch[:n_jobs]
    stage_out = cast_scratch[n_jobs:2 * n_jobs]
    sem_in, sem_out = cast_scratch[2 * n_jobs:]
    nb, nt = grid_shape
    step = pl.program_id(0) * nt + pl.program_id(1)
    return _cast_weights_step(step, nb * nt, cast_layers, srcs, dsts, stage_in, stage_out, sem_in, sem_out)


def _even_kernel(*refs, tm, stride, has_state, gate_layer, cast_layers, grid_shape):
    ins, srcs, outs, dsts, scr, cast_scr = _split_refs(
        refs, 9 if has_state else 8, 3 if has_state else 2, 5, len(cast_layers))
    x_ref, g_ref, win_ref, wout_ref, lng_ref, cw_ref, gw_ref, gb_ref = ins[:8]
    state_ref = ins[8] if has_state else None
    out_ref, newb_ref = outs[:2]
    v_ref = outs[2] if has_state else None
    xn_ref, xp_ref, ycat_ref, vb_ref, z_ref = scr
    finish_cast = _maybe_cast_weights(cast_layers, grid_shape, srcs, dsts, cast_scr)

    rb = ROW_BLOCK
    mc = MXU_COLS
    hist = (B_CONV - 1) * stride
    h_rows = _round_up(hist, SUBLANES)
    _init_history(xp_ref, state_ref, hist, pl.program_id(1) == 0)
    lng = lng_ref[...]
    allc = slice(0, B_WIDTH)
    conv_rows = [cw_ref[k:k + 1, :] for k in range(B_CONV)]
    if stride == 1:
        row_i = lax.broadcasted_iota(jnp.int32, (CHUNK, CHUNK), 0)
        col_i = lax.broadcasted_iota(jnp.int32, (CHUNK, CHUNK), 1)
        wts = [jnp.where(col_i <= row_i, gw_ref[h], 0.0).astype(_BF16) for h in range(A_HEADS)]
    sub = min(tm, SUB_TILE)
    n_sub = tm // sub
    u0, v0, bg0, cg0, hh0 = (k * A_WIDTH for k in range(5))


    def in_proj_pieces(si):
        srows = slice(si * sub, (si + 1) * sub)

        def piece(c0):
            def emit():
                z = _dot(xn_ref[srows, :], win_ref[:, c0:c0 + mc])
                z_ref[si % 2, :, c0:c0 + mc] = z
                return z[0:1, 0:LANES]
            return emit

        order = list(range(bg0, hh0 + B_WIDTH, mc)) + list(range(u0, bg0, mc))
        return [piece(c0) for c0 in order]

    def b_block(si, r0):
        def emit(after):
            loc = slice(r0 - si * sub, r0 - si * sub + rb)
            z = z_ref.at[si % 2]
            xp_ref[h_rows + r0:h_rows + r0 + rb, :] = z[loc, cg0:cg0 + B_WIDTH] * z[loc, hh0:hh0 + B_WIDTH]
            w_rows = conv_rows[:-1] + [_ordered_after(conv_rows[-1], after)]
            conv = _conv_taps(xp_ref, cw_ref, r0, allc, B_CONV, stride, rb, w_rows=w_rows)
            ycat_ref[r0:r0 + rb, A_WIDTH:A_WIDTH + B_WIDTH] = (z[loc, bg0:bg0 + B_WIDTH] * conv).astype(_BF16)
        return emit

    def ln_block(si, r0):
        def emit(after):
            gain = _ordered_after(lng, after)
            for q0 in range(r0, r0 + rb, NORM_BLOCK):
                rows = slice(q0, q0 + NORM_BLOCK)
                loc = slice(q0 - si * sub, q0 - si * sub + NORM_BLOCK)
                v = _ln_rows(jax.nn.gelu(z_ref[si % 2, loc, v0:v0 + A_WIDTH]), gain)
                if v_ref is not None:
                    v_ref[rows, :] = v
                else:
                    vb_ref[rows, :] = v.astype(_BF16)
        return emit

    def gate_items(si):
        s0 = si * sub
        z = z_ref.at[si % 2]
        items = []
        if stride == 1:
            for c0 in range(s0, s0 + sub, 2 * CHUNK):
                for h in range(A_HEADS):
                    def emit(after, c0=c0, h=h):
                        ra = slice(c0, c0 + CHUNK)
                        rb2 = slice(c0 + CHUNK, c0 + 2 * CHUNK)
                        cols = slice(h * CHUNK, (h + 1) * CHUNK)
                        vpair = jnp.concatenate([vb_ref[ra, cols], vb_ref[rb2, cols]], axis=1)
                        mixed = _dot(wts[h], vpair)
                        bias = gb_ref[:, h:h + 1]
                        for half, rws in enumerate((ra, rb2)):
                            m = mixed[:, half * CHUNK:(half + 1) * CHUNK] + bias
                            u = z[rws.start - s0:rws.stop - s0, u0 + h * CHUNK:u0 + (h + 1) * CHUNK]
                            ycat_ref[rws, cols] = (jax.nn.gelu(u) * m).astype(_BF16)
                    items.append(emit)
        else:
            steps = tm // stride
            gw_off = gate_layer * A_HEADS * steps * steps
            gb_off = gate_layer * A_HEADS * steps
            for t in range(s0 // stride, (s0 + sub) // stride):
                def emit(after, t=t):
                    for n0 in range(0, stride, rb):
                        rows = slice(t * stride + n0, t * stride + n0 + rb)
                        for h in range(A_HEADS):
                            cols = slice(h * CHUNK, (h + 1) * CHUNK)
                            mixed = None
                            for s in range(t + 1):
                                w = gw_ref[gw_off + (h * steps + t) * steps + s]
                                term = w * v_ref[s * stride + n0:s * stride + n0 + rb, cols]
                                mixed = term if mixed is None else mixed + term
                            mixed = mixed + gb_ref[gb_off + h * steps + t]
                            u = z[rows.start - s0:rows.stop - s0, u0 + h * CHUNK:u0 + (h + 1) * CHUNK]
                            ycat_ref[rows, cols] = (jax.nn.gelu(u) * mixed).astype(_BF16)
                items.append(emit)
        return items

    def out_proj_pieces(si):
        srows = slice(si * sub, (si + 1) * sub)

        def piece(c0):
            def emit():
                y = _dot(ycat_ref[srows, :], wout_ref[:, c0:c0 + mc])
                out_ref[srows, c0:c0 + mc] = x_ref[srows, c0:c0 + mc] + y
                return y[0:1, 0:LANES]
            return emit

        return [piece(c0) for c0 in range(0, D_MODEL, mc)]

    _norm_to_bf16(x_ref, g_ref, xn_ref, 0, sub)
    for emit in in_proj_pieces(0):
        emit()
    for si in range(n_sub):
        s0 = si * sub
        side = []
        if si + 1 < n_sub:
            _norm_to_bf16(x_ref, g_ref, xn_ref, s0 + sub, sub)
            side += in_proj_pieces(si + 1)
        if si >= 1:
            side += out_proj_pieces(si - 1)
        blocks = ([b_block(si, r0) for r0 in range(s0, s0 + sub, rb)]
                  + [ln_block(si, r0) for r0 in range(s0, s0 + sub, rb)] + gate_items(si))
        after = None
        for emit in blocks:
            emit(after)
            after = side.pop(0)() if side else None
        for emit in side:
            emit()
    for emit in out_proj_pieces(n_sub - 1):
        emit()
    _emit_history(xp_ref, newb_ref, hist, tm, carry=not has_state)
    finish_cast()


def _pool_block(xp_ref, r0, gi, win, stride, rb, pos0, row_pos_static):
    cols = slice(gi * D_GROUP_DIM, (gi + 1) * D_GROUP_DIM)
    hist = POOL_BUF * stride
    h_rows = _round_up(hist, SUBLANES)
    if stride % SUBLANES == 0:
        s = None
        for j in range(win):
            off = h_rows + r0 - j * stride
            term = xp_ref[off:off + rb, cols]
            s = term if s is None else s + term
    else:
        halo = 2 * SUBLANES
        ext = xp_ref[h_rows + r0 - halo:h_rows + r0 + rb, cols]
        shift = 1
        while shift < win:
            ext = ext + pltpu.roll(ext, shift, 0)
            shift *= 2
        s = ext[halo:]
    if row_pos_static is not None:
        return s / float(min(row_pos_static + 1, win))
    pos = pos0 + r0 + lax.broadcasted_iota(jnp.int32, (rb, D_GROUP_DIM), 0)
    cnt = jnp.minimum(pos + 1, win).astype(_F32)
    return s / cnt


def _odd_kernel(*refs, tm, stride, has_state, start_pos, cast_layers, grid_shape):
    ins, srcs, outs, dsts, scr, cast_scr = _split_refs(refs, 12 if has_state else 10, 3, 5, len(cast_layers))
    x_ref, g_ref, win_ref, wout_ref, ccw_ref, ccb_ref, clg_ref, clb_ref, dproj_ref, dscale_ref = ins[:10]
    statec_ref, stated_ref = (ins[10], ins[11]) if has_state else (None, None)
    out_ref, newc_ref, newd_ref = outs
    xn_ref, xpc_ref, xpd_ref, ycat_ref, zb_ref = scr
    finish_cast = _maybe_cast_weights(cast_layers, grid_shape, srcs, dsts, cast_scr)

    rb = ROW_BLOCK
    hist_c = (C_CONV - 1) * stride
    hc_rows = _round_up(hist_c, SUBLANES)
    hist_d = POOL_BUF * stride
    hd_rows = _round_up(hist_d, SUBLANES)
    first = pl.program_id(1) == 0
    _init_history(xpc_ref, statec_ref, hist_c, first)
    _init_history(xpd_ref, stated_ref, hist_d, first)
    ccb = ccb_ref[...]
    clg = clg_ref[...]
    clb = clb_ref[...]
    pos0 = start_pos + pl.program_id(1) * tm
    mc = MXU_COLS
    cuts = list(range(0, tm + 1, min(tm, ODD_SUB_TILE)))
    tiles = list(zip(cuts[:-1], cuts[1:]))
    n_sub = len(tiles)


    def in_proj_pieces(s0, s1):
        srows = slice(s0, s1)

        def glu_piece(c0):
            def emit():
                xn = xn_ref[srows, :]
                xpc_ref[hc_rows + s0:hc_rows + s1, c0:c0 + mc] = _dot(xn, win_ref[:, c0:c0 + mc])
                zb = _dot(xn, win_ref[:, C_WIDTH + c0:C_WIDTH + c0 + mc])
                zb_ref[srows, c0:c0 + mc] = zb
                return zb[0:1, 0:LANES]
            return emit

        def pool_in_piece(c0):
            def emit():
                zp = _dot(xn_ref[srows, :], win_ref[:, 2 * C_WIDTH + c0:2 * C_WIDTH + c0 + mc])
                xpd_ref[hd_rows + s0:hd_rows + s1, c0:c0 + mc] = zp
                return zp[0:1, 0:LANES]
            return emit

        return ([glu_piece(c0) for c0 in range(0, C_WIDTH, mc)]
                + [pool_in_piece(c0) for c0 in range(0, D_WIDTH, mc)])

    gated = {}

    def conv_block(s0, s1, r0):
        def emit(after):
            lo = gated.get(s0, s0)
            hi = min(r0 + rb + SUBLANES, s1)
            for g0 in range(lo, hi, rb):
                grows = slice(hc_rows + g0, hc_rows + min(g0 + rb, hi))
                xpc_ref[grows, :] = xpc_ref[grows, :] * jax.nn.sigmoid(zb_ref[g0:min(g0 + rb, hi), :])
            gated[s0] = hi
            parts = []
            for c0 in range(0, C_WIDTH, LANES):
                cols = slice(c0, c0 + LANES)
                if stride % SUBLANES == 0:
                    parts.append(_conv_taps(xpc_ref, ccw_ref, r0, cols, C_CONV, stride, rb))
                else:
                    parts.append(_conv_long_stride1(xpc_ref, ccw_ref, r0, cols, C_CONV, rb))
            cc = jnp.concatenate(parts, axis=-1) + _ordered_after(ccb, after)
            ycat_ref[r0:r0 + rb, 0:C_WIDTH] = jax.nn.silu(_ln_rows(cc, clg, clb)).astype(_BF16)
        return emit

    def pool_block(r0):
        def emit(after):
            for gi, win in enumerate(POOL_WINDOWS):
                cols = slice(gi * D_GROUP_DIM, (gi + 1) * D_GROUP_DIM)
                if stride == 1:
                    static_pos = None if r0 < POOL_BUF else POOL_BUF
                else:
                    static_pos = start_pos + r0 // stride
                pooled = _pool_block(xpd_ref, r0, gi, win, stride, rb, pos0, static_pos)
                diff = (pooled - xpd_ref[hd_rows + r0:hd_rows + r0 + rb, cols]).astype(_BF16)
                scale = dscale_ref[:, cols]
                if gi == 0:
                    scale = _ordered_after(scale, after)
                yd = _dot(diff, dproj_ref[gi]) * scale
                ycat_ref[r0:r0 + rb, C_WIDTH + gi * D_GROUP_DIM:C_WIDTH + (gi + 1) * D_GROUP_DIM] = yd.astype(_BF16)
        return emit

    def out_proj_pieces(s0, s1):
        srows = slice(s0, s1)

        def piece(c0):
            def emit():
                y = _dot(ycat_ref[srows, :], wout_ref[:, c0:c0 + mc])
                out_ref[srows, c0:c0 + mc] = x_ref[srows, c0:c0 + mc] + y
                return y[0:1, 0:LANES]
            return emit

        return [piece(c0) for c0 in range(0, D_MODEL, mc)]

    _norm_to_bf16(x_ref, g_ref, xn_ref, 0, tiles[0][1])
    for emit in in_proj_pieces(*tiles[0]):
        emit()
    for si, (s0, s1) in enumerate(tiles):
        xpc_ref[hc_rows + s1:hc_rows + s1 + SUBLANES, :] = jnp.zeros((SUBLANES, C_WIDTH), _F32)
        side = []
        if si + 1 < n_sub:
            n0, n1 = tiles[si + 1]
            _norm_to_bf16(x_ref, g_ref, xn_ref, n0, n1 - n0)
            side += in_proj_pieces(n0, n1)
        if si >= 1:
            side += out_proj_pieces(*tiles[si - 1])
        blocks = [conv_block(s0, s1, r0) for r0 in range(s0, s1, rb)] + [pool_block(r0) for r0 in range(s0, s1, rb)]
        after = None
        for emit in blocks:
            emit(after)
            after = side.pop(0)() if side else None
        for emit in side:
            emit()
    for emit in out_proj_pieces(*tiles[-1]):
        emit()
    _emit_history(xpc_ref, newc_ref, hist_c, tm, carry=not has_state)
    _emit_history(xpd_ref, newd_ref, hist_d, tm, carry=not has_state)
    finish_cast()


def _ffn_kernel(*refs, stride, has_state, final_norm):
    refs = list(refs)
    x_ref, g_ref, win_ref, wout_ref, cw_ref = refs[:5]
    del refs[:5]
    state_ref = refs.pop(0) if has_state else None
    gfin_ref = refs.pop(0) if final_norm else None
    out_ref, newf_ref, xn_ref, gp_ref, hist_ref, h_ref = refs
    rb = ROW_BLOCK
    hist = (FFN_CONV - 1) * stride
    h_rows = _round_up(hist, SUBLANES)
    _init_history(hist_ref, state_ref, hist, pl.program_id(1) == 0)
    n_sub, sub, _ = x_ref.shape

    def sub_tile(i, carry):
        xs = x_ref.at[i]
        os = out_ref.at[i]
        _norm_to_bf16(xs, g_ref, xn_ref, 0, sub)
        xn = xn_ref[...]
        for j, c0 in enumerate(range(0, D_FF, FF_CHUNK)):
            cols = slice(c0, c0 + FF_CHUNK)
            slot = slice((j % 2) * FF_CHUNK, (j % 2 + 1) * FF_CHUNK)
            gp_ref[0:h_rows, slot] = hist_ref[:, cols]
            zg = _dot(xn, win_ref[:, c0:c0 + FF_CHUNK])
            zu = _dot(xn, win_ref[:, D_FF + c0:D_FF + c0 + FF_CHUNK])
            for r0 in range(0, sub, rb):
                rows = slice(r0, r0 + rb)
                gp_ref[h_rows + r0:h_rows + r0 + rb, slot] = zg[rows]
                conv = _conv_taps(gp_ref, cw_ref, r0, cols, FFN_CONV, stride, rb, xcols=slot)
                h_ref[rows, cols] = (jax.nn.gelu(conv) * zu[rows]).astype(_BF16)
            hist_ref[:, cols] = gp_ref[sub:sub + h_rows, slot]

        y = xs[...] + _dot(h_ref[...], wout_ref[...])
        if final_norm:
            gfin = gfin_ref[...]
            os[...] = y
            for r0 in range(0, sub, NORM_BLOCK):
                rows = slice(r0, r0 + NORM_BLOCK)
                os[rows, :] = _rms_rows(os[rows, :], gfin)
        else:
            os[...] = y
        return carry

    lax.fori_loop(0, n_sub, sub_tile, 0)
    newf_ref[0] = hist_ref[h_rows - hist:h_rows, :]


def _const_spec(shape):
    nd = len(shape)
    return pl.BlockSpec(shape, lambda b, t: (0,) * nd, pipeline_mode=pl.Buffered(1))


def _layer_spec(arr, layer):
    nd = arr.ndim - 1
    return pl.BlockSpec((None,) + arr.shape[1:], lambda b, t: (layer,) + (0,) * nd, pipeline_mode=pl.Buffered(1))


def _smem_spec():
    return pl.BlockSpec(memory_space=pltpu.SMEM)


def _hbm_spec():
    return pl.BlockSpec(memory_space=pl.ANY)


def _tile_spec(tm, nt, width):
    return pl.BlockSpec((tm, width), lambda b, t: (b * nt + t, 0))


def _state_spec(layer, rows, width):
    return pl.BlockSpec((None, 1, rows, width), lambda b, t: (layer, b, 0, 0))


def _new_state_spec(rows, width):
    return pl.BlockSpec((1, rows, width), lambda b, t: (b, 0, 0))


def _params():
    return pltpu.CompilerParams(dimension_semantics=("arbitrary", "arbitrary"), vmem_limit_bytes=VMEM_LIMIT)


def _cast_plumbing(cast_jobs, n_steps):
    in_specs, args, out_specs, out_shape, stage_in, stage_out = [], [], [], [], [], []
    for arr, _ in cast_jobs:
        _, r, c = arr.shape
        rc = r // n_steps
        assert rc * n_steps == r and rc % BF16_ROWS == 0, (arr.shape, n_steps)
        in_specs.append(_hbm_spec())
        args.append(arr)
        out_specs.append(_hbm_spec())
        out_shape.append(jax.ShapeDtypeStruct((r, c), _BF16))
        stage_in.append(pltpu.VMEM((2, rc, c), _F32))
        stage_out.append(pltpu.VMEM((2, rc, c), _BF16))
    scratch = stage_in + stage_out
    if cast_jobs:
        n = len(cast_jobs)
        scratch += [pltpu.SemaphoreType.DMA((n, 2)), pltpu.SemaphoreType.DMA((n, 2))]
    return in_specs, args, out_specs, out_shape, scratch, tuple(layer for _, layer in cast_jobs)


def _even_call(x2, nb, nt, stride, state, layer, win, wout, prm, cast_jobs, name):
    has_state = state is not None
    i = layer // 2
    tm = x2.shape[0] // (nb * nt)
    hist = (B_CONV - 1) * stride
    h_rows = _round_up(hist, SUBLANES)
    c_in, c_args, c_out_specs, c_out_shape, c_scratch, cast_layers = _cast_plumbing(cast_jobs, nb * nt)
    in_specs = [_tile_spec(tm, nt, D_MODEL), _layer_spec(prm['norm_mix_g'], layer), _const_spec(win.shape),
                _const_spec(wout.shape), _layer_spec(prm['a_ln_g'], i), _layer_spec(prm['b_conv_w'], i)]
    args = [x2, prm['norm_mix_g'], win, wout, prm['a_ln_g'], prm['b_conv_w']]
    if has_state:
        in_specs += [_smem_spec(), _smem_spec(), _state_spec(i, hist, B_WIDTH)]
        args += [prm['a_ws_steps'], prm['a_bs_steps'], state]
    else:
        in_specs += [_layer_spec(prm['a_ws'], i), _layer_spec(prm['a_bs_t'], i)]
        args += [prm['a_ws'], prm['a_bs_t']]
    out_shape = [jax.ShapeDtypeStruct(x2.shape, _F32), jax.ShapeDtypeStruct((nb, hist, B_WIDTH), _F32)]
    out_specs = [_tile_spec(tm, nt, D_MODEL), _new_state_spec(hist, B_WIDTH)]
    if has_state:
        out_shape.append(jax.ShapeDtypeStruct((x2.shape[0], A_WIDTH), _F32))
        out_specs.append(_tile_spec(tm, nt, A_WIDTH))
    sub = min(tm, SUB_TILE)
    scratch = [pltpu.VMEM((tm, D_MODEL), _BF16), pltpu.VMEM((h_rows + tm, B_WIDTH), _F32),
               pltpu.VMEM((tm, A_WIDTH + B_WIDTH), _BF16), pltpu.VMEM((tm, A_WIDTH), _BF16),
               pltpu.VMEM((min(tm // sub, 2), sub, win.shape[1]), _F32)]
    return pl.pallas_call(
        functools.partial(_even_kernel, tm=tm, stride=stride, has_state=has_state, gate_layer=i,
                          cast_layers=cast_layers, grid_shape=(nb, nt)),
        grid=(nb, nt), in_specs=in_specs + c_in, out_specs=out_specs + c_out_specs,
        out_shape=out_shape + c_out_shape, scratch_shapes=scratch + c_scratch,
        compiler_params=_params(), name=name)(*args, *c_args)


def _odd_call(x2, nb, nt, stride, state_c, state_d, start_pos, layer, win, wout, prm, cast_jobs, name):
    has_state = state_c is not None
    i = layer // 2
    hist_c = (C_CONV - 1) * stride
    hist_d = POOL_BUF * stride
    hc_rows = _round_up(hist_c, SUBLANES)
    hd_rows = _round_up(hist_d, SUBLANES)
    tm = x2.shape[0] // (nb * nt)
    c_in, c_args, c_out_specs, c_out_shape, c_scratch, cast_layers = _cast_plumbing(cast_jobs, nb * nt)
    names = ['c_conv_w', 'c_conv_b', 'c_ln_g', 'c_ln_b', 'd_proj', 'd_scale']
    in_specs = [_tile_spec(tm, nt, D_MODEL), _layer_spec(prm['norm_mix_g'], layer), _const_spec(win.shape),
                _const_spec(wout.shape)] + [_layer_spec(prm[k], i) for k in names]
    args = [x2, prm['norm_mix_g'], win, wout] + [prm[k] for k in names]
    if has_state:
        in_specs += [_state_spec(i, hist_c, C_WIDTH), _state_spec(i, hist_d, D_WIDTH)]
        args += [state_c, state_d]
    out_shape = [jax.ShapeDtypeStruct(x2.shape, _F32), jax.ShapeDtypeStruct((nb, hist_c, C_WIDTH), _F32),
                 jax.ShapeDtypeStruct((nb, hist_d, D_WIDTH), _F32)]
    out_specs = [_tile_spec(tm, nt, D_MODEL), _new_state_spec(hist_c, C_WIDTH), _new_state_spec(hist_d, D_WIDTH)]
    scratch = [pltpu.VMEM((tm, D_MODEL), _BF16), pltpu.VMEM((hc_rows + tm + SUBLANES, C_WIDTH), _F32),
               pltpu.VMEM((hd_rows + tm, D_WIDTH), _F32), pltpu.VMEM((tm, C_WIDTH + D_WIDTH), _BF16),
               pltpu.VMEM((tm, C_WIDTH), _F32)]
    return pl.pallas_call(
        functools.partial(_odd_kernel, tm=tm, stride=stride, has_state=has_state, start_pos=start_pos,
                          cast_layers=cast_layers, grid_shape=(nb, nt)),
        grid=(nb, nt), in_specs=in_specs + c_in, out_specs=out_specs + c_out_specs,
        out_shape=out_shape + c_out_shape, scratch_shapes=scratch + c_scratch,
        compiler_params=_params(), name=name)(*args, *c_args)


def _ffn_call(x2, nb, nt, stride, state, layer, final_norm, win, wout, prm, name):
    has_state = state is not None
    hist = (FFN_CONV - 1) * stride
    h_rows = _round_up(hist, SUBLANES)
    tm = x2.shape[0] // (nb * nt)
    sub = min(tm, SUB_TILE)
    n_sub = tm // sub
    x3 = x2.reshape(-1, sub, D_MODEL)
    x_spec = pl.BlockSpec((n_sub, sub, D_MODEL), lambda b, t: (b * nt + t, 0, 0))
    in_specs = [x_spec, _layer_spec(prm['norm_ffn_g'], layer), _const_spec(win.shape), _const_spec(wout.shape),
                _layer_spec(prm['ffn_conv_w'], layer)]
    args = [x3, prm['norm_ffn_g'], win, wout, prm['ffn_conv_w']]
    if has_state:
        in_specs.append(_state_spec(layer, hist, D_FF))
        args.append(state)
    if final_norm:
        in_specs.append(_const_spec((1, D_MODEL)))
        args.append(prm['norm_final_g'])
    out_shape = [jax.ShapeDtypeStruct(x3.shape, _F32), jax.ShapeDtypeStruct((nb, hist, D_FF), _F32)]
    out_specs = [x_spec, _new_state_spec(hist, D_FF)]
    scratch = [pltpu.VMEM((sub, D_MODEL), _BF16), pltpu.VMEM((h_rows + sub, 2 * FF_CHUNK), _F32),
               pltpu.VMEM((h_rows, D_FF), _F32), pltpu.VMEM((sub, D_FF), _BF16)]
    y3, newf = pl.pallas_call(
        functools.partial(_ffn_kernel, stride=stride, has_state=has_state, final_norm=final_norm),
        grid=(nb, nt), in_specs=in_specs, out_specs=out_specs, out_shape=out_shape,
        scratch_shapes=scratch, compiler_params=_params(), name=name)(*args)
    return y3.reshape(x2.shape), newf


def _to_tiles(a, ns):
    *lead, n, l, c = a.shape
    k = len(lead)
    a = a.reshape(*lead, n // ns, ns, l, c)
    a = a.transpose(*range(k), k, k + 2, k + 1, k + 3)
    return a.reshape(*lead, n // ns, l * ns, c)


def _from_tiles(a, ns, l):
    *lead, nb, _, c = a.shape
    k = len(lead)
    a = a.reshape(*lead, nb, l, ns, c)
    a = a.transpose(*range(k), k, k + 2, k + 1, k + 3)
    return a.reshape(*lead, nb * ns, l, c)


def _trunk(x2, nb, nt, stride, states, start_pos, prm, wb, raw, tag):
    new_a, new_b, new_c, new_d, new_f = [], [], [], [], []
    depth = prm['ffn_conv_w'].shape[0]
    for layer in range(depth):
        cast_keys, cast_jobs = [], []
        if raw is not None:
            cast_keys.append(('ffn', layer))
            cast_jobs += [(raw['w_ffn_in'], layer), (raw['w_ffn_out'], layer)]
            if layer + 1 < depth:
                nxt = 'even' if (layer + 1) % 2 == 0 else 'odd'
                cast_keys.append(('mix', layer + 1))
                cast_jobs += [(raw['w_in_' + nxt], (layer + 1) // 2), (raw['w_out_' + nxt], (layer + 1) // 2)]
        win, wout = wb[('mix', layer)]
        if layer % 2 == 0:
            st = None if states is None else states['b']
            res = _even_call(x2, nb, nt, stride, st, layer, win, wout, prm, cast_jobs, name=f"{tag}_mix{layer}")
            n_own = 2 if states is None else 3
            x2 = res[0]
            new_b.append(res[1])
            if states is not None:
                new_a.append(res[2])
        else:
            st_c = None if states is None else states['c']
            st_d = None if states is None else states['d']
            res = _odd_call(x2, nb, nt, stride, st_c, st_d, start_pos, layer, win, wout, prm, cast_jobs,
                            name=f"{tag}_mix{layer}")
            n_own = 3
            x2 = res[0]
            new_c.append(res[1])
            new_d.append(res[2])
        for k, key in enumerate(cast_keys):
            wb[key] = (res[n_own + 2 * k], res[n_own + 2 * k + 1])
        st_f = None if states is None else states['f']
        win, wout = wb[('ffn', layer)]
        x2, nf = _ffn_call(x2, nb, nt, stride, st_f, layer, layer == depth - 1, win, wout, prm,
                           name=f"{tag}_ffn{layer}")
        new_f.append(nf)
    return x2, new_a, new_b, new_c, new_d, new_f


def kernel(x_prompt, x_sample, state_b_conv, state_c_conv, state_d_pool, state_ffn_conv, norm_mix_g, norm_ffn_g, norm_final_g, w_in_even, w_out_even, a_ln_g, a_ws, a_bs, b_conv_w, w_in_odd, w_out_odd, c_conv_w, c_conv_b, c_ln_g, c_ln_b, d_proj, d_scale, w_ffn_in, ffn_conv_w, w_ffn_out):
    batch, seq, d = x_prompt.shape
    dec_batch, dec_seq, _ = x_sample.shape
    ns = SAMPLE_TILE // dec_seq

    def rows(p):
        return p.reshape(p.shape[0], 1, p.shape[1])

    prm = dict(norm_mix_g=rows(norm_mix_g), norm_ffn_g=rows(norm_ffn_g), norm_final_g=norm_final_g.reshape(1, -1),
               a_ln_g=rows(a_ln_g), a_ws=a_ws, a_bs_t=jnp.swapaxes(a_bs, 1, 2),
               a_ws_steps=a_ws[:, :, :dec_seq, :dec_seq].reshape(-1), a_bs_steps=a_bs[:, :, :dec_seq].reshape(-1),
               b_conv_w=b_conv_w,
               c_conv_w=jnp.pad(c_conv_w, ((0, 0), (0, _round_up(C_CONV, SUBLANES) - C_CONV), (0, 0))),
               c_conv_b=rows(c_conv_b), c_ln_g=rows(c_ln_g), c_ln_b=rows(c_ln_b), d_proj=d_proj.astype(_BF16),
               d_scale=rows(d_scale), ffn_conv_w=ffn_conv_w)
    raw = dict(w_in_even=w_in_even, w_out_even=w_out_even, w_in_odd=w_in_odd, w_out_odd=w_out_odd,
               w_ffn_in=w_ffn_in, w_ffn_out=w_ffn_out)
    wb = {('mix', 0): (w_in_even[0].astype(_BF16), w_out_even[0].astype(_BF16))}

    yp, _, b_p, c_p, d_p, f_p = _trunk(x_prompt.reshape(batch * seq, d), batch, seq // PROMPT_TILE, 1, None, 0,
                                       prm, wb, raw, "p")

    states = dict(b=_to_tiles(state_b_conv, ns), c=_to_tiles(state_c_conv, ns), d=_to_tiles(state_d_pool, ns),
                  f=_to_tiles(state_ffn_conv, ns))
    xs = _to_tiles(x_sample, ns).reshape(dec_batch * dec_seq, d)
    ys, a_s, b_s, c_s, d_s, f_s = _trunk(xs, dec_batch // ns, 1, ns, states, PAST_LEN, prm, wb, None, "s")

    def untile(parts, l):
        return _from_tiles(jnp.stack(parts), ns, l)

    y_prompt = yp.reshape(batch, seq, d)
    y_sample = _from_tiles(ys.reshape(dec_batch // ns, dec_seq * ns, d), ns, dec_seq)
    new_a = untile([a.reshape(dec_batch // ns, dec_seq * ns, A_WIDTH) for a in a_s], dec_seq)
    return (y_prompt, y_sample, new_a, jnp.stack(b_p), untile(b_s, B_CONV - 1), jnp.stack(c_p),
            untile(c_s, C_CONV - 1), jnp.stack(d_p), untile(d_s, POOL_BUF), jnp.stack(f_p),
            untile(f_s, FFN_CONV - 1))
```

```python
import functools

import jax
import jax.numpy as jnp
from jax import lax
from jax.experimental import pallas as pl
from jax.experimental.pallas import tpu as pltpu

D_MODEL = 1024
CHUNK = 128
A_HEADS = 4
A_WIDTH = 512
B_WIDTH = 512
B_CONV = 3
C_WIDTH = 512
C_CONV = 31
D_WIDTH = 512
POOL_WINDOWS = (2, 4, 8, 16)
D_GROUP_DIM = 128
POOL_BUF = 15
D_FF = 2816
FFN_CONV = 3
PAST_LEN = 16384
EPS = 1e-6

SUBLANES = 8
LANES = 128
BF16_ROWS = 16
MXU_COLS = 256
PROMPT_TILE = 1024
SAMPLE_TILE = 512
SUB_TILE = 512
ODD_SUB_TILE = 512
ROW_BLOCK = 64
SHIFT_ROW_BLOCK = 128
NORM_BLOCK = 32
FF_CHUNK = 256
VMEM_LIMIT = 56 * 1024 * 1024

_BF16 = jnp.bfloat16
_F32 = jnp.float32


def _round_up(n, m):
    return (n + m - 1) // m * m


def _dot(a, b):
    return jnp.dot(a, b, preferred_element_type=_F32)


def _rms_rows(x, g):
    y = x * lax.rsqrt(jnp.mean(x * x, axis=-1, keepdims=True) + EPS)
    return y * g


def _ln_rows(x, g, b=None):
    mu = jnp.mean(x, axis=-1, keepdims=True)
    xc = x - mu
    y = xc * lax.rsqrt(jnp.mean(xc * xc, axis=-1, keepdims=True) + EPS) * g
    if b is not None:
        y = y + b
    return y


def _norm_to_bf16(x_ref, g_ref, xn_ref, s0, sub):
    g = g_ref[...]
    for r0 in range(s0, s0 + sub, NORM_BLOCK):
        rows = slice(r0, r0 + NORM_BLOCK)
        xn_ref[rows, :] = _rms_rows(x_ref[rows, :], g).astype(_BF16)


def _init_history(xp_ref, state_ref, hist, first_tile):
    h_rows = _round_up(hist, SUBLANES)
    if state_ref is None:
        @pl.when(first_tile)
        def _():
            xp_ref[0:h_rows, :] = jnp.zeros((h_rows, xp_ref.shape[1]), _F32)
    else:
        xp_ref[h_rows - hist:h_rows, :] = state_ref[0]


def _emit_history(xp_ref, new_ref, hist, tm, carry):
    h_rows = _round_up(hist, SUBLANES)
    tail = xp_ref[h_rows + tm - hist:h_rows + tm, :]
    new_ref[0] = tail
    if carry:
        xp_ref[h_rows - hist:h_rows, :] = tail


def _conv_taps(xp_ref, w_ref, r0, cols, ktaps, stride, rb, xcols=None, w_rows=None):
    xcols = cols if xcols is None else xcols
    hist = (ktaps - 1) * stride
    base = _round_up(hist, SUBLANES) - hist + r0
    acc = None
    for k in range(ktaps):
        w_k = w_ref[k:k + 1, cols] if w_rows is None else w_rows[k]
        term = w_k * xp_ref[base + k * stride:base + k * stride + rb, xcols]
        acc = term if acc is None else acc + term
    return acc


def _conv_long_stride1(xp_ref, w_ref, r0, cols, ktaps, rb):
    hist = ktaps - 1
    base = _round_up(hist, SUBLANES) - hist
    n = rb + SUBLANES
    p = None
    for r in range(SUBLANES - 1, -1, -1):
        z = None
        for k in range(ktaps):
            if (base + k) % SUBLANES != r:
                continue
            off = r0 + (base + k) - r
            term = w_ref[k:k + 1, cols] * xp_ref[off:off + n, cols]
            z = term if z is None else z + term
        if p is not None:
            p = pltpu.roll(p, n - 1, 0)
            p = p if z is None else p + z
        else:
            p = z
    return p[0:rb]


def _ordered_after(row, token):
    if token is None:
        return row
    half = jnp.uint32(16)
    bits = lax.bitcast_convert_type(token, jnp.uint32)
    zero = lax.shift_right_logical(lax.shift_right_logical(bits, half), half)
    head = lax.bitcast_convert_type(lax.bitcast_convert_type(row[:, 0:LANES], jnp.uint32) + zero, _F32)
    if row.shape[1] == LANES:
        return head
    return jnp.concatenate([head, row[:, LANES:]], axis=1)


def _cast_weights_step(step, n_steps, layers, srcs, dsts, stage_in, stage_out, sem_in, sem_out):
    n_jobs = len(layers)

    def in_copy(j, s, slot):
        rc = stage_in[j].shape[1]
        return pltpu.make_async_copy(srcs[j].at[layers[j], pl.ds(s * rc, rc), :], stage_in[j].at[slot],
                                     sem_in.at[j, slot])

    def out_copy(j, s, slot):
        rc = stage_out[j].shape[1]
        return pltpu.make_async_copy(stage_out[j].at[slot], dsts[j].at[pl.ds(s * rc, rc), :], sem_out.at[j, slot])

    slot = lax.rem(step, 2)

    @pl.when(step == 0)
    def _():
        for j in range(n_jobs):
            in_copy(j, 0, 0).start()

    for j in range(n_jobs):
        in_copy(j, step, slot).wait()

    @pl.when(step + 1 < n_steps)
    def _():
        for j in range(n_jobs):
            in_copy(j, step + 1, 1 - slot).start()

    @pl.when(step >= 2)
    def _():
        for j in range(n_jobs):
            out_copy(j, step - 2, slot).wait()

    for j in range(n_jobs):
        src = stage_in[j].at[slot]
        dst = stage_out[j].at[slot]
        for r0 in range(0, src.shape[0], BF16_ROWS):
            dst[r0:r0 + BF16_ROWS, :] = src[r0:r0 + BF16_ROWS, :].astype(_BF16)
        out_copy(j, step, slot).start()

    def finish():
        @pl.when(step == n_steps - 1)
        def _():
            for j in range(n_jobs):
                if n_steps > 1:
                    out_copy(j, step - 1, 1 - slot).wait()
                out_copy(j, step, slot).wait()

    return finish


def _split_refs(refs, n_in, n_out, n_scratch, n_jobs):
    refs = list(refs)
    a = n_in
    b = a + n_jobs
    c = b + n_out
    d = c + n_jobs
    e = d + n_scratch
    return refs[:a], refs[a:b], refs[b:c], refs[c:d], refs[d:e], refs[e:]


def _maybe_cast_weights(cast_layers, grid_shape, srcs, dsts, cast_scratch):
    n_jobs = len(cast_layers)
    if n_jobs == 0:
        return lambda: None
    stage_in = cast_scratch[:n_jobs]
    stage_out = cast_scratch[n_jobs:2 * n_jobs]
    sem_in, sem_out = cast_scratch[2 * n_jobs:]
    nb, nt = grid_shape
    step = pl.program_id(0) * nt + pl.program_id(1)
    return _cast_weights_step(step, nb * nt, cast_layers, srcs, dsts, stage_in, stage_out, sem_in, sem_out)


def _even_kernel(*refs, tm, stride, has_state, gate_layer, cast_layers, grid_shape):
    ins, srcs, outs, dsts, scr, cast_scr = _split_refs(
        refs, 9 if has_state else 8, 3 if has_state else 2, 5, len(cast_layers))
    x_ref, g_ref, win_ref, wout_ref, lng_ref, cw_ref, gw_ref, gb_ref = ins[:8]
    state_ref = ins[8] if has_state else None
    out_ref, newb_ref = outs[:2]
    v_ref = outs[2] if has_state else None
    xn_ref, xp_ref, ycat_ref, vb_ref, z_ref = scr
    finish_cast = _maybe_cast_weights(cast_layers, grid_shape, srcs, dsts, cast_scr)

    rb = ROW_BLOCK
    mc = MXU_COLS
    hist = (B_CONV - 1) * stride
    h_rows = _round_up(hist, SUBLANES)
    _init_history(xp_ref, state_ref, hist, pl.program_id(1) == 0)
    lng = lng_ref[...]
    allc = slice(0, B_WIDTH)
    conv_rows = [cw_ref[k:k + 1, :] for k in range(B_CONV)]
    if stride == 1:
        row_i = lax.broadcasted_iota(jnp.int32, (CHUNK, CHUNK), 0)
        col_i = lax.broadcasted_iota(jnp.int32, (CHUNK, CHUNK), 1)
        wts = [jnp.where(col_i <= row_i, gw_ref[h], 0.0).astype(_BF16) for h in range(A_HEADS)]
    sub = min(tm, SUB_TILE)
    n_sub = tm // sub
    u0, v0, bg0, cg0, hh0 = (k * A_WIDTH for k in range(5))


    def in_proj_pieces(si):
        srows = slice(si * sub, (si + 1) * sub)

        def piece(c0):
            def emit():
                z = _dot(xn_ref[srows, :], win_ref[:, c0:c0 + mc])
                z_ref[si % 2, :, c0:c0 + mc] = z
                return z[0:1, 0:LANES]
            return emit

        order = list(range(bg0, hh0 + B_WIDTH, mc)) + list(range(u0, bg0, mc))
        return [piece(c0) for c0 in order]

    def b_block(si, r0):
        def emit(after):
            loc = slice(r0 - si * sub, r0 - si * sub + rb)
            z = z_ref.at[si % 2]
            xp_ref[h_rows + r0:h_rows + r0 + rb, :] = z[loc, cg0:cg0 + B_WIDTH] * z[loc, hh0:hh0 + B_WIDTH]
            w_rows = conv_rows[:-1] + [_ordered_after(conv_rows[-1], after)]
            conv = _conv_taps(xp_ref, cw_ref, r0, allc, B_CONV, stride, rb, w_rows=w_rows)
            ycat_ref[r0:r0 + rb, A_WIDTH:A_WIDTH + B_WIDTH] = (z[loc, bg0:bg0 + B_WIDTH] * conv).astype(_BF16)
        return emit

    def ln_block(si, r0):
        def emit(after):
            gain = _ordered_after(lng, after)
            for q0 in range(r0, r0 + rb, NORM_BLOCK):
                rows = slice(q0, q0 + NORM_BLOCK)
                loc = slice(q0 - si * sub, q0 - si * sub + NORM_BLOCK)
                v = _ln_rows(jax.nn.gelu(z_ref[si % 2, loc, v0:v0 + A_WIDTH]), gain)
                if v_ref is not None:
                    v_ref[rows, :] = v
                else:
                    vb_ref[rows, :] = v.astype(_BF16)
        return emit

    def gate_items(si):
        s0 = si * sub
        z = z_ref.at[si % 2]
        items = []
        if stride == 1:
            for c0 in range(s0, s0 + sub, 2 * CHUNK):
                for h in range(A_HEADS):
                    def emit(after, c0=c0, h=h):
                        ra = slice(c0, c0 + CHUNK)
                        rb2 = slice(c0 + CHUNK, c0 + 2 * CHUNK)
                        cols = slice(h * CHUNK, (h + 1) * CHUNK)
                        vpair = jnp.concatenate([vb_ref[ra, cols], vb_ref[rb2, cols]], axis=1)
                        mixed = _dot(wts[h], vpair)
                        bias = gb_ref[:, h:h + 1]
                        for half, rws in enumerate((ra, rb2)):
                            m = mixed[:, half * CHUNK:(half + 1) * CHUNK] + bias
                            u = z[rws.start - s0:rws.stop - s0, u0 + h * CHUNK:u0 + (h + 1) * CHUNK]
                            ycat_ref[rws, cols] = (jax.nn.gelu(u) * m).astype(_BF16)
                    items.append(emit)
        else:
            steps = tm // stride
            gw_off = gate_layer * A_HEADS * steps * steps
            gb_off = gate_layer * A_HEADS * steps
            for t in range(s0 // stride, (s0 + sub) // stride):
                def emit(after, t=t):
                    for n0 in range(0, stride, rb):
                        rows = slice(t * stride + n0, t * stride + n0 + rb)
                        for h in range(A_HEADS):
                            cols = slice(h * CHUNK, (h + 1) * CHUNK)
                            mixed = None
                            for s in range(t + 1):
                                w = gw_ref[gw_off + (h * steps + t) * steps + s]
                                term = w * v_ref[s * stride + n0:s * stride + n0 + rb, cols]
                                mixed = term if mixed is None else mixed + term
                            mixed = mixed + gb_ref[gb_off + h * steps + t]
                            u = z[rows.start - s0:rows.stop - s0, u0 + h * CHUNK:u0 + (h + 1) * CHUNK]
                            ycat_ref[rows, cols] = (jax.nn.gelu(u) * mixed).astype(_BF16)
                items.append(emit)
        return items

    def out_proj_pieces(si):
        srows = slice(si * sub, (si + 1) * sub)

        def piece(c0):
            def emit():
                y = _dot(ycat_ref[srows, :], wout_ref[:, c0:c0 + mc])
                out_ref[srows, c0:c0 + mc] = x_ref[srows, c0:c0 + mc] + y
                return y[0:1, 0:LANES]
            return emit

        return [piece(c0) for c0 in range(0, D_MODEL, mc)]

    _norm_to_bf16(x_ref, g_ref, xn_ref, 0, sub)
    for emit in in_proj_pieces(0):
        emit()
    for si in range(n_sub):
        s0 = si * sub
        side = []
        if si + 1 < n_sub:
            _norm_to_bf16(x_ref, g_ref, xn_ref, s0 + sub, sub)
            side += in_proj_pieces(si + 1)
        if si >= 1:
            side += out_proj_pieces(si - 1)
        blocks = ([b_block(si, r0) for r0 in range(s0, s0 + sub, rb)]
                  + [ln_block(si, r0) for r0 in range(s0, s0 + sub, rb)] + gate_items(si))
        after = None
        for emit in blocks:
            emit(after)
            after = side.pop(0)() if side else None
        for emit in side:
            emit()
    for emit in out_proj_pieces(n_sub - 1):
        emit()
    _emit_history(xp_ref, newb_ref, hist, tm, carry=not has_state)
    finish_cast()


def _pool_block(xp_ref, r0, gi, win, stride, rb, pos0, row_pos_static):
    cols = slice(gi * D_GROUP_DIM, (gi + 1) * D_GROUP_DIM)
    hist = POOL_BUF * stride
    h_rows = _round_up(hist, SUBLANES)
    if stride % SUBLANES == 0:
        s = None
        for j in range(win):
            off = h_rows + r0 - j * stride
            term = xp_ref[off:off + rb, cols]
            s = term if s is None else s + term
    else:
        halo = 2 * SUBLANES
        ext = xp_ref[h_rows + r0 - halo:h_rows + r0 + rb, cols]
        shift = 1
        while shift < win:
            ext = ext + pltpu.roll(ext, shift, 0)
            shift *= 2
        s = ext[halo:]
    if row_pos_static is not None:
        return s / float(min(row_pos_static + 1, win))
    pos = pos0 + r0 + lax.broadcasted_iota(jnp.int32, (rb, D_GROUP_DIM), 0)
    cnt = jnp.minimum(pos + 1, win).astype(_F32)
    return s / cnt


def _odd_kernel(*refs, tm, stride, has_state, start_pos, cast_layers, grid_shape):
    ins, srcs, outs, dsts, scr, cast_scr = _split_refs(refs, 12 if has_state else 10, 3, 5, len(cast_layers))
    x_ref, g_ref, win_ref, wout_ref, ccw_ref, ccb_ref, clg_ref, clb_ref, dproj_ref, dscale_ref = ins[:10]
    statec_ref, stated_ref = (ins[10], ins[11]) if has_state else (None, None)
    out_ref, newc_ref, newd_ref = outs
    xn_ref, xpc_ref, xpd_ref, ycat_ref, zb_ref = scr
    finish_cast = _maybe_cast_weights(cast_layers, grid_shape, srcs, dsts, cast_scr)

    rb = SHIFT_ROW_BLOCK if stride == 1 else ROW_BLOCK
    hist_c = (C_CONV - 1) * stride
    hc_rows = _round_up(hist_c, SUBLANES)
    hist_d = POOL_BUF * stride
    hd_rows = _round_up(hist_d, SUBLANES)
    first = pl.program_id(1) == 0
    _init_history(xpc_ref, statec_ref, hist_c, first)
    _init_history(xpd_ref, stated_ref, hist_d, first)
    ccb = ccb_ref[...]
    clg = clg_ref[...]
    clb = clb_ref[...]
    pos0 = start_pos + pl.program_id(1) * tm
    mc = MXU_COLS
    cuts = list(range(0, tm + 1, min(tm, ODD_SUB_TILE)))
    tiles = list(zip(cuts[:-1], cuts[1:]))
    n_sub = len(tiles)


    def in_proj_pieces(s0, s1):
        srows = slice(s0, s1)

        def glu_piece(c0):
            def emit():
                xn = xn_ref[srows, :]
                xpc_ref[hc_rows + s0:hc_rows + s1, c0:c0 + mc] = _dot(xn, win_ref[:, c0:c0 + mc])
                zb = _dot(xn, win_ref[:, C_WIDTH + c0:C_WIDTH + c0 + mc])
                zb_ref[srows, c0:c0 + mc] = zb
                return zb[0:1, 0:LANES]
            return emit

        def pool_in_piece(c0):
            def emit():
                zp = _dot(xn_ref[srows, :], win_ref[:, 2 * C_WIDTH + c0:2 * C_WIDTH + c0 + mc])
                xpd_ref[hd_rows + s0:hd_rows + s1, c0:c0 + mc] = zp
                return zp[0:1, 0:LANES]
            return emit

        return ([glu_piece(c0) for c0 in range(0, C_WIDTH, mc)]
                + [pool_in_piece(c0) for c0 in range(0, D_WIDTH, mc)])

    gated = {}

    def conv_block(s0, s1, r0):
        def emit(after):
            lo = gated.get(s0, s0)
            hi = min(r0 + rb + SUBLANES, s1)
            for g0 in range(lo, hi, rb):
                grows = slice(hc_rows + g0, hc_rows + min(g0 + rb, hi))
                xpc_ref[grows, :] = xpc_ref[grows, :] * jax.nn.sigmoid(zb_ref[g0:min(g0 + rb, hi), :])
            gated[s0] = hi
            parts = []
            for c0 in range(0, C_WIDTH, LANES):
                cols = slice(c0, c0 + LANES)
                if stride % SUBLANES == 0:
                    parts.append(_conv_taps(xpc_ref, ccw_ref, r0, cols, C_CONV, stride, rb))
                else:
                    parts.append(_conv_long_stride1(xpc_ref, ccw_ref, r0, cols, C_CONV, rb))
            cc = jnp.concatenate(parts, axis=-1) + _ordered_after(ccb, after)
            ycat_ref[r0:r0 + rb, 0:C_WIDTH] = jax.nn.silu(_ln_rows(cc, clg, clb)).astype(_BF16)
        return emit

    def pool_block(r0):
        def emit(after):
            for gi, win in enumerate(POOL_WINDOWS):
                cols = slice(gi * D_GROUP_DIM, (gi + 1) * D_GROUP_DIM)
                if stride == 1:
                    static_pos = None if r0 < POOL_BUF else POOL_BUF
                else:
                    static_pos = start_pos + r0 // stride
                pooled = _pool_block(xpd_ref, r0, gi, win, stride, rb, pos0, static_pos)
                diff = (pooled - xpd_ref[hd_rows + r0:hd_rows + r0 + rb, cols]).astype(_BF16)
                scale = dscale_ref[:, cols]
                if gi == 0:
                    scale = _ordered_after(scale, after)
                yd = _dot(diff, dproj_ref[gi]) * scale
                ycat_ref[r0:r0 + rb, C_WIDTH + gi * D_GROUP_DIM:C_WIDTH + (gi + 1) * D_GROUP_DIM] = yd.astype(_BF16)
        return emit

    def out_proj_pieces(s0, s1):
        srows = slice(s0, s1)

        def piece(c0):
            def emit():
                y = _dot(ycat_ref[srows, :], wout_ref[:, c0:c0 + mc])
                out_ref[srows, c0:c0 + mc] = x_ref[srows, c0:c0 + mc] + y
                return y[0:1, 0:LANES]
            return emit

        return [piece(c0) for c0 in range(0, D_MODEL, mc)]

    _norm_to_bf16(x_ref, g_ref, xn_ref, 0, tiles[0][1])
    for emit in in_proj_pieces(*tiles[0]):
        emit()
    for si, (s0, s1) in enumerate(tiles):
        xpc_ref[hc_rows + s1:hc_rows + s1 + SUBLANES, :] = jnp.zeros((SUBLANES, C_WIDTH), _F32)
        side = []
        if si + 1 < n_sub:
            n0, n1 = tiles[si + 1]
            _norm_to_bf16(x_ref, g_ref, xn_ref, n0, n1 - n0)
            side += in_proj_pieces(n0, n1)
        if si >= 1:
            side += out_proj_pieces(*tiles[si - 1])
        blocks = [conv_block(s0, s1, r0) for r0 in range(s0, s1, rb)] + [pool_block(r0) for r0 in range(s0, s1, rb)]
        after = None
        for emit in blocks:
            emit(after)
            after = side.pop(0)() if side else None
        for emit in side:
            emit()
    for emit in out_proj_pieces(*tiles[-1]):
        emit()
    _emit_history(xpc_ref, newc_ref, hist_c, tm, carry=not has_state)
    _emit_history(xpd_ref, newd_ref, hist_d, tm, carry=not has_state)
    finish_cast()


def _ffn_kernel(*refs, stride, has_state, final_norm):
    refs = list(refs)
    x_ref, g_ref, win_ref, wout_ref, cw_ref = refs[:5]
    del refs[:5]
    state_ref = refs.pop(0) if has_state else None
    gfin_ref = refs.pop(0) if final_norm else None
    out_ref, newf_ref, xn_ref, gp_ref, hist_ref, h_ref = refs
    rb = SHIFT_ROW_BLOCK if stride == 1 else ROW_BLOCK
    hist = (FFN_CONV - 1) * stride
    h_rows = _round_up(hist, SUBLANES)
    _init_history(hist_ref, state_ref, hist, pl.program_id(1) == 0)
    n_sub, sub, _ = x_ref.shape

    def sub_tile(i, carry):
        xs = x_ref.at[i]
        os = out_ref.at[i]
        _norm_to_bf16(xs, g_ref, xn_ref, 0, sub)
        xn = xn_ref[...]
        for j, c0 in enumerate(range(0, D_FF, FF_CHUNK)):
            cols = slice(c0, c0 + FF_CHUNK)
            slot = slice((j % 2) * FF_CHUNK, (j % 2 + 1) * FF_CHUNK)
            gp_ref[0:h_rows, slot] = hist_ref[:, cols]
            zg = _dot(xn, win_ref[:, c0:c0 + FF_CHUNK])
            zu = _dot(xn, win_ref[:, D_FF + c0:D_FF + c0 + FF_CHUNK])
            for r0 in range(0, sub, rb):
                rows = slice(r0, r0 + rb)
                gp_ref[h_rows + r0:h_rows + r0 + rb, slot] = zg[rows]
                conv = _conv_taps(gp_ref, cw_ref, r0, cols, FFN_CONV, stride, rb, xcols=slot)
                h_ref[rows, cols] = (jax.nn.gelu(conv) * zu[rows]).astype(_BF16)
            hist_ref[:, cols] = gp_ref[sub:sub + h_rows, slot]

        y = xs[...] + _dot(h_ref[...], wout_ref[...])
        if final_norm:
            gfin = gfin_ref[...]
            os[...] = y
            for r0 in range(0, sub, NORM_BLOCK):
                rows = slice(r0, r0 + NORM_BLOCK)
                os[rows, :] = _rms_rows(os[rows, :], gfin)
        else:
            os[...] = y
        return carry

    lax.fori_loop(0, n_sub, sub_tile, 0)
    newf_ref[0] = hist_ref[h_rows - hist:h_rows, :]


def _const_spec(shape):
    nd = len(shape)
    return pl.BlockSpec(shape, lambda b, t: (0,) * nd, pipeline_mode=pl.Buffered(1))


def _layer_spec(arr, layer):
    nd = arr.ndim - 1
    return pl.BlockSpec((None,) + arr.shape[1:], lambda b, t: (layer,) + (0,) * nd, pipeline_mode=pl.Buffered(1))


def _smem_spec():
    return pl.BlockSpec(memory_space=pltpu.SMEM)


def _hbm_spec():
    return pl.BlockSpec(memory_space=pl.ANY)


def _tile_spec(tm, nt, width):
    return pl.BlockSpec((tm, width), lambda b, t: (b * nt + t, 0))


def _state_spec(layer, rows, width):
    return pl.BlockSpec((None, 1, rows, width), lambda b, t: (layer, b, 0, 0))


def _new_state_spec(rows, width):
    return pl.BlockSpec((1, rows, width), lambda b, t: (b, 0, 0))


def _params():
    return pltpu.CompilerParams(dimension_semantics=("arbitrary", "arbitrary"), vmem_limit_bytes=VMEM_LIMIT)


def _cast_plumbing(cast_jobs, n_steps):
    in_specs, args, out_specs, out_shape, stage_in, stage_out = [], [], [], [], [], []
    for arr, _ in cast_jobs:
        _, r, c = arr.shape
        rc = r // n_steps
        assert rc * n_steps == r and rc % BF16_ROWS == 0, (arr.shape, n_steps)
        in_specs.append(_hbm_spec())
        args.append(arr)
        out_specs.append(_hbm_spec())
        out_shape.append(jax.ShapeDtypeStruct((r, c), _BF16))
        stage_in.append(pltpu.VMEM((2, rc, c), _F32))
        stage_out.append(pltpu.VMEM((2, rc, c), _BF16))
    scratch = stage_in + stage_out
    if cast_jobs:
        n = len(cast_jobs)
        scratch += [pltpu.SemaphoreType.DMA((n, 2)), pltpu.SemaphoreType.DMA((n, 2))]
    return in_specs, args, out_specs, out_shape, scratch, tuple(layer for _, layer in cast_jobs)


def _even_call(x2, nb, nt, stride, state, layer, win, wout, prm, cast_jobs, name):
    has_state = state is not None
    i = layer // 2
    tm = x2.shape[0] // (nb * nt)
    hist = (B_CONV - 1) * stride
    h_rows = _round_up(hist, SUBLANES)
    c_in, c_args, c_out_specs, c_out_shape, c_scratch, cast_layers = _cast_plumbing(cast_jobs, nb * nt)
    in_specs = [_tile_spec(tm, nt, D_MODEL), _layer_spec(prm['norm_mix_g'], layer), _const_spec(win.shape),
                _const_spec(wout.shape), _layer_spec(prm['a_ln_g'], i), _layer_spec(prm['b_conv_w'], i)]
    args = [x2, prm['norm_mix_g'], win, wout, prm['a_ln_g'], prm['b_conv_w']]
    if has_state:
        in_specs += [_smem_spec(), _smem_spec(), _state_spec(i, hist, B_WIDTH)]
        args += [prm['a_ws_steps'], prm['a_bs_steps'], state]
    else:
        in_specs += [_layer_spec(prm['a_ws'], i), _layer_spec(prm['a_bs_t'], i)]
        args += [prm['a_ws'], prm['a_bs_t']]
    out_shape = [jax.ShapeDtypeStruct(x2.shape, _F32), jax.ShapeDtypeStruct((nb, hist, B_WIDTH), _F32)]
    out_specs = [_tile_spec(tm, nt, D_MODEL), _new_state_spec(hist, B_WIDTH)]
    if has_state:
        out_shape.append(jax.ShapeDtypeStruct((x2.shape[0], A_WIDTH), _F32))
        out_specs.append(_tile_spec(tm, nt, A_WIDTH))
    sub = min(tm, SUB_TILE)
    scratch = [pltpu.VMEM((tm, D_MODEL), _BF16), pltpu.VMEM((h_rows + tm, B_WIDTH), _F32),
               pltpu.VMEM((tm, A_WIDTH + B_WIDTH), _BF16), pltpu.VMEM((tm, A_WIDTH), _BF16),
               pltpu.VMEM((min(tm // sub, 2), sub, win.shape[1]), _F32)]
    return pl.pallas_call(
        functools.partial(_even_kernel, tm=tm, stride=stride, has_state=has_state, gate_layer=i,
                          cast_layers=cast_layers, grid_shape=(nb, nt)),
        grid=(nb, nt), in_specs=in_specs + c_in, out_specs=out_specs + c_out_specs,
        out_shape=out_shape + c_out_shape, scratch_shapes=scratch + c_scratch,
        compiler_params=_params(), name=name)(*args, *c_args)


def _odd_call(x2, nb, nt, stride, state_c, state_d, start_pos, layer, win, wout, prm, cast_jobs, name):
    has_state = state_c is not None
    i = layer // 2
    hist_c = (C_CONV - 1) * stride
    hist_d = POOL_BUF * stride
    hc_rows = _round_up(hist_c, SUBLANES)
    hd_rows = _round_up(hist_d, SUBLANES)
    tm = x2.shape[0] // (nb * nt)
    c_in, c_args, c_out_specs, c_out_shape, c_scratch, cast_layers = _cast_plumbing(cast_jobs, nb * nt)
    names = ['c_conv_w', 'c_conv_b', 'c_ln_g', 'c_ln_b', 'd_proj', 'd_scale']
    in_specs = [_tile_spec(tm, nt, D_MODEL), _layer_spec(prm['norm_mix_g'], layer), _const_spec(win.shape),
                _const_spec(wout.shape)] + [_layer_spec(prm[k], i) for k in names]
    args = [x2, prm['norm_mix_g'], win, wout] + [prm[k] for k in names]
    if has_state:
        in_specs += [_state_spec(i, hist_c, C_WIDTH), _state_spec(i, hist_d, D_WIDTH)]
        args += [state_c, state_d]
    out_shape = [jax.ShapeDtypeStruct(x2.shape, _F32), jax.ShapeDtypeStruct((nb, hist_c, C_WIDTH), _F32),
                 jax.ShapeDtypeStruct((nb, hist_d, D_WIDTH), _F32)]
    out_specs = [_tile_spec(tm, nt, D_MODEL), _new_state_spec(hist_c, C_WIDTH), _new_state_spec(hist_d, D_WIDTH)]
    scratch = [pltpu.VMEM((tm, D_MODEL), _BF16), pltpu.VMEM((hc_rows + tm + SUBLANES, C_WIDTH), _F32),
               pltpu.VMEM((hd_rows + tm, D_WIDTH), _F32), pltpu.VMEM((tm, C_WIDTH + D_WIDTH), _BF16),
               pltpu.VMEM((tm, C_WIDTH), _F32)]
    return pl.pallas_call(
        functools.partial(_odd_kernel, tm=tm, stride=stride, has_state=has_state, start_pos=start_pos,
                          cast_layers=cast_layers, grid_shape=(nb, nt)),
        grid=(nb, nt), in_specs=in_specs + c_in, out_specs=out_specs + c_out_specs,
        out_shape=out_shape + c_out_shape, scratch_shapes=scratch + c_scratch,
        compiler_params=_params(), name=name)(*args, *c_args)


def _ffn_call(x2, nb, nt, stride, state, layer, final_norm, win, wout, prm, name):
    has_state = state is not None
    hist = (FFN_CONV - 1) * stride
    h_rows = _round_up(hist, SUBLANES)
    tm = x2.shape[0] // (nb * nt)
    sub = min(tm, SUB_TILE)
    n_sub = tm // sub
    x3 = x2.reshape(-1, sub, D_MODEL)
    x_spec = pl.BlockSpec((n_sub, sub, D_MODEL), lambda b, t: (b * nt + t, 0, 0))
    in_specs = [x_spec, _layer_spec(prm['norm_ffn_g'], layer), _const_spec(win.shape), _const_spec(wout.shape),
                _layer_spec(prm['ffn_conv_w'], layer)]
    args = [x3, prm['norm_ffn_g'], win, wout, prm['ffn_conv_w']]
    if has_state:
        in_specs.append(_state_spec(layer, hist, D_FF))
        args.append(state)
    if final_norm:
        in_specs.append(_const_spec((1, D_MODEL)))
        args.append(prm['norm_final_g'])
    out_shape = [jax.ShapeDtypeStruct(x3.shape, _F32), jax.ShapeDtypeStruct((nb, hist, D_FF), _F32)]
    out_specs = [x_spec, _new_state_spec(hist, D_FF)]
    scratch = [pltpu.VMEM((sub, D_MODEL), _BF16), pltpu.VMEM((h_rows + sub, 2 * FF_CHUNK), _F32),
               pltpu.VMEM((h_rows, D_FF), _F32), pltpu.VMEM((sub, D_FF), _BF16)]
    y3, newf = pl.pallas_call(
        functools.partial(_ffn_kernel, stride=stride, has_state=has_state, final_norm=final_norm),
        grid=(nb, nt), in_specs=in_specs, out_specs=out_specs, out_shape=out_shape,
        scratch_shapes=scratch, compiler_params=_params(), name=name)(*args)
    return y3.reshape(x2.shape), newf


def _to_tiles(a, ns):
    *lead, n, l, c = a.shape
    k = len(lead)
    a = a.reshape(*lead, n // ns, ns, l, c)
    a = a.transpose(*range(k), k, k + 2, k + 1, k + 3)
    return a.reshape(*lead, n // ns, l * ns, c)


def _from_tiles(a, ns, l):
    *lead, nb, _, c = a.shape
    k = len(lead)
    a = a.reshape(*lead, nb, l, ns, c)
    a = a.transpose(*range(k), k, k + 2, k + 1, k + 3)
    return a.reshape(*lead, nb * ns, l, c)


def _trunk(x2, nb, nt, stride, states, start_pos, prm, wb, raw, tag):
    new_a, new_b, new_c, new_d, new_f = [], [], [], [], []
    depth = prm['ffn_conv_w'].shape[0]
    for layer in range(depth):
        cast_keys, cast_jobs = [], []
        if raw is not None:
            cast_keys.append(('ffn', layer))
            cast_jobs += [(raw['w_ffn_in'], layer), (raw['w_ffn_out'], layer)]
            if layer + 1 < depth:
                nxt = 'even' if (layer + 1) % 2 == 0 else 'odd'
                cast_keys.append(('mix', layer + 1))
                cast_jobs += [(raw['w_in_' + nxt], (layer + 1) // 2), (raw['w_out_' + nxt], (layer + 1) // 2)]
        win, wout = wb[('mix', layer)]
        if layer % 2 == 0:
            st = None if states is None else states['b']
            res = _even_call(x2, nb, nt, stride, st, layer, win, wout, prm, cast_jobs, name=f"{tag}_mix{layer}")
            n_own = 2 if states is None else 3
            x2 = res[0]
            new_b.append(res[1])
            if states is not None:
                new_a.append(res[2])
        else:
            st_c = None if states is None else states['c']
            st_d = None if states is None else states['d']
            res = _odd_call(x2, nb, nt, stride, st_c, st_d, start_pos, layer, win, wout, prm, cast_jobs,
                            name=f"{tag}_mix{layer}")
            n_own = 3
            x2 = res[0]
            new_c.append(res[1])
            new_d.append(res[2])
        for k, key in enumerate(cast_keys):
            wb[key] = (res[n_own + 2 * k], res[n_own + 2 * k + 1])
        st_f = None if states is None else states['f']
        win, wout = wb[('ffn', layer)]
        x2, nf = _ffn_call(x2, nb, nt, stride, st_f, layer, layer == depth - 1, win, wout, prm,
                           name=f"{tag}_ffn{layer}")
        new_f.append(nf)
    return x2, new_a, new_b, new_c, new_d, new_f


def kernel(x_prompt, x_sample, state_b_conv, state_c_conv, state_d_pool, state_ffn_conv, norm_mix_g, norm_ffn_g, norm_final_g, w_in_even, w_out_even, a_ln_g, a_ws, a_bs, b_conv_w, w_in_odd, w_out_odd, c_conv_w, c_conv_b, c_ln_g, c_ln_b, d_proj, d_scale, w_ffn_in, ffn_conv_w, w_ffn_out):
    batch, seq, d = x_prompt.shape
    dec_batch, dec_seq, _ = x_sample.shape
    ns = SAMPLE_TILE // dec_seq

    def rows(p):
        return p.reshape(p.shape[0], 1, p.shape[1])

    prm = dict(norm_mix_g=rows(norm_mix_g), norm_ffn_g=rows(norm_ffn_g), norm_final_g=norm_final_g.reshape(1, -1),
               a_ln_g=rows(a_ln_g), a_ws=a_ws, a_bs_t=jnp.swapaxes(a_bs, 1, 2),
               a_ws_steps=a_ws[:, :, :dec_seq, :dec_seq].reshape(-1), a_bs_steps=a_bs[:, :, :dec_seq].reshape(-1),
               b_conv_w=b_conv_w,
               c_conv_w=jnp.pad(c_conv_w, ((0, 0), (0, _round_up(C_CONV, SUBLANES) - C_CONV), (0, 0))),
               c_conv_b=rows(c_conv_b), c_ln_g=rows(c_ln_g), c_ln_b=rows(c_ln_b), d_proj=d_proj.astype(_BF16),
               d_scale=rows(d_scale), ffn_conv_w=ffn_conv_w)
    raw = dict(w_in_even=w_in_even, w_out_even=w_out_even, w_in_odd=w_in_odd, w_out_odd=w_out_odd,
               w_ffn_in=w_ffn_in, w_ffn_out=w_ffn_out)
    wb = {('mix', 0): (w_in_even[0].astype(_BF16), w_out_even[0].astype(_BF16))}

    yp, _, b_p, c_p, d_p, f_p = _trunk(x_prompt.reshape(batch * seq, d), batch, seq // PROMPT_TILE, 1, None, 0,
                                       prm, wb, raw, "p")

    states = dict(b=_to_tiles(state_b_conv, ns), c=_to_tiles(state_c_conv, ns), d=_to_tiles(state_d_pool, ns),
                  f=_to_tiles(state_ffn_conv, ns))
    xs = _to_tiles(x_sample, ns).reshape(dec_batch * dec_seq, d)
    ys, a_s, b_s, c_s, d_s, f_s = _trunk(xs, dec_batch // ns, 1, ns, states, PAST_LEN, prm, wb, None, "s")

    def untile(parts, l):
        return _from_tiles(jnp.stack(parts), ns, l)

    y_prompt = yp.reshape(batch, seq, d)
    y_sample = _from_tiles(ys.reshape(dec_batch // ns, dec_seq * ns, d), ns, dec_seq)
    new_a = untile([a.reshape(dec_batch // ns, dec_seq * ns, A_WIDTH) for a in a_s], dec_seq)
    return (y_prompt, y_sample, new_a, jnp.stack(b_p), untile(b_s, B_CONV - 1), jnp.stack(c_p),
            untile(c_s, C_CONV - 1), jnp.stack(d_p), untile(d_s, POOL_BUF), jnp.stack(f_p),
            untile(f_s, FFN_CONV - 1))
```

```python
import functools

import jax
import jax.numpy as jnp
from jax import lax
from jax.experimental import pallas as pl
from jax.experimental.pallas import tpu as pltpu

D_MODEL = 1024
CHUNK = 128
A_HEADS = 4
A_WIDTH = 512
B_WIDTH = 512
B_CONV = 3
C_WIDTH = 512
C_CONV = 31
D_WIDTH = 512
POOL_WINDOWS = (2, 4, 8, 16)
D_GROUP_DIM = 128
POOL_BUF = 15
D_FF = 2816
FFN_CONV = 3
PAST_LEN = 16384
EPS = 1e-6

SUBLANES = 8
LANES = 128
BF16_ROWS = 16
MXU_COLS = 256
PROMPT_TILE = 1024
SAMPLE_TILE = 512
SUB_TILE = 512
ODD_SUB_TILE = 512
ROW_BLOCK = 64
SHIFT_ROW_BLOCK = 128
FFN_ROW_BLOCK = 256
NORM_BLOCK = 32
FF_CHUNK = 256
VMEM_LIMIT = 56 * 1024 * 1024

_BF16 = jnp.bfloat16
_F32 = jnp.float32


def _round_up(n, m):
    return (n + m - 1) // m * m


def _dot(a, b):
    return jnp.dot(a, b, preferred_element_type=_F32)


def _rms_rows(x, g):
    y = x * lax.rsqrt(jnp.mean(x * x, axis=-1, keepdims=True) + EPS)
    return y * g


def _ln_rows(x, g, b=None):
    mu = jnp.mean(x, axis=-1, keepdims=True)
    xc = x - mu
    y = xc * lax.rsqrt(jnp.mean(xc * xc, axis=-1, keepdims=True) + EPS) * g
    if b is not None:
        y = y + b
    return y


def _norm_to_bf16(x_ref, g_ref, xn_ref, s0, sub):
    g = g_ref[...]
    for r0 in range(s0, s0 + sub, NORM_BLOCK):
        rows = slice(r0, r0 + NORM_BLOCK)
        xn_ref[rows, :] = _rms_rows(x_ref[rows, :], g).astype(_BF16)


def _init_history(xp_ref, state_ref, hist, first_tile):
    h_rows = _round_up(hist, SUBLANES)
    if state_ref is None:
        @pl.when(first_tile)
        def _():
            xp_ref[0:h_rows, :] = jnp.zeros((h_rows, xp_ref.shape[1]), _F32)
    else:
        xp_ref[h_rows - hist:h_rows, :] = state_ref[0]


def _emit_history(xp_ref, new_ref, hist, tm, carry):
    h_rows = _round_up(hist, SUBLANES)
    tail = xp_ref[h_rows + tm - hist:h_rows + tm, :]
    new_ref[0] = tail
    if carry:
        xp_ref[h_rows - hist:h_rows, :] = tail


def _conv_taps(xp_ref, w_ref, r0, cols, ktaps, stride, rb, xcols=None, w_rows=None):
    xcols = cols if xcols is None else xcols
    hist = (ktaps - 1) * stride
    base = _round_up(hist, SUBLANES) - hist + r0
    acc = None
    for k in range(ktaps):
        w_k = w_ref[k:k + 1, cols] if w_rows is None else w_rows[k]
        term = w_k * xp_ref[base + k * stride:base + k * stride + rb, xcols]
        acc = term if acc is None else acc + term
    return acc


def _conv_long_stride1(xp_ref, w_ref, r0, cols, ktaps, rb):
    hist = ktaps - 1
    base = _round_up(hist, SUBLANES) - hist
    n = rb + SUBLANES
    p = None
    for r in range(SUBLANES - 1, -1, -1):
        z = None
        for k in range(ktaps):
            if (base + k) % SUBLANES != r:
                continue
            off = r0 + (base + k) - r
            term = w_ref[k:k + 1, cols] * xp_ref[off:off + n, cols]
            z = term if z is None else z + term
        if p is not None:
            p = pltpu.roll(p, n - 1, 0)
            p = p if z is None else p + z
        else:
            p = z
    return p[0:rb]


def _ordered_after(row, token):
    if token is None:
        return row
    half = jnp.uint32(16)
    bits = lax.bitcast_convert_type(token, jnp.uint32)
    zero = lax.shift_right_logical(lax.shift_right_logical(bits, half), half)
    head = lax.bitcast_convert_type(lax.bitcast_convert_type(row[:, 0:LANES], jnp.uint32) + zero, _F32)
    if row.shape[1] == LANES:
        return head
    return jnp.concatenate([head, row[:, LANES:]], axis=1)


def _cast_weights_step(step, n_steps, layers, srcs, dsts, stage_in, stage_out, sem_in, sem_out):
    n_jobs = len(layers)

    def in_copy(j, s, slot):
        rc = stage_in[j].shape[1]
        return pltpu.make_async_copy(srcs[j].at[layers[j], pl.ds(s * rc, rc), :], stage_in[j].at[slot],
                                     sem_in.at[j, slot])

    def out_copy(j, s, slot):
        rc = stage_out[j].shape[1]
        return pltpu.make_async_copy(stage_out[j].at[slot], dsts[j].at[pl.ds(s * rc, rc), :], sem_out.at[j, slot])

    slot = lax.rem(step, 2)

    @pl.when(step == 0)
    def _():
        for j in range(n_jobs):
            in_copy(j, 0, 0).start()

    for j in range(n_jobs):
        in_copy(j, step, slot).wait()

    @pl.when(step + 1 < n_steps)
    def _():
        for j in range(n_jobs):
            in_copy(j, step + 1, 1 - slot).start()

    @pl.when(step >= 2)
    def _():
        for j in range(n_jobs):
            out_copy(j, step - 2, slot).wait()

    for j in range(n_jobs):
        src = stage_in[j].at[slot]
        dst = stage_out[j].at[slot]
        for r0 in range(0, src.shape[0], BF16_ROWS):
            dst[r0:r0 + BF16_ROWS, :] = src[r0:r0 + BF16_ROWS, :].astype(_BF16)
        out_copy(j, step, slot).start()

    def finish():
        @pl.when(step == n_steps - 1)
        def _():
            for j in range(n_jobs):
                if n_steps > 1:
                    out_copy(j, step - 1, 1 - slot).wait()
                out_copy(j, step, slot).wait()

    return finish


def _split_refs(refs, n_in, n_out, n_scratch, n_jobs):
    refs = list(refs)
    a = n_in
    b = a + n_jobs
    c = b + n_out
    d = c + n_jobs
    e = d + n_scratch
    return refs[:a], refs[a:b], refs[b:c], refs[c:d], refs[d:e], refs[e:]


def _maybe_cast_weights(cast_layers, grid_shape, srcs, dsts, cast_scratch):
    n_jobs = len(cast_layers)
    if n_jobs == 0:
        return lambda: None
    stage_in = cast_scratch[:n_jobs]
    stage_out = cast_scratch[n_jobs:2 * n_jobs]
    sem_in, sem_out = cast_scratch[2 * n_jobs:]
    nb, nt = grid_shape
    step = pl.program_id(0) * nt + pl.program_id(1)
    return _cast_weights_step(step, nb * nt, cast_layers, srcs, dsts, stage_in, stage_out, sem_in, sem_out)


def _even_kernel(*refs, tm, stride, has_state, gate_layer, cast_layers, grid_shape):
    ins, srcs, outs, dsts, scr, cast_scr = _split_refs(
        refs, 9 if has_state else 8, 3 if has_state else 2, 5, len(cast_layers))
    x_ref, g_ref, win_ref, wout_ref, lng_ref, cw_ref, gw_ref, gb_ref = ins[:8]
    state_ref = ins[8] if has_state else None
    out_ref, newb_ref = outs[:2]
    v_ref = outs[2] if has_state else None
    xn_ref, xp_ref, ycat_ref, vb_ref, z_ref = scr
    finish_cast = _maybe_cast_weights(cast_layers, grid_shape, srcs, dsts, cast_scr)

    rb = SHIFT_ROW_BLOCK if stride == 1 else ROW_BLOCK
    mc = MXU_COLS
    hist = (B_CONV - 1) * stride
    h_rows = _round_up(hist, SUBLANES)
    _init_history(xp_ref, state_ref, hist, pl.program_id(1) == 0)
    lng = lng_ref[...]
    allc = slice(0, B_WIDTH)
    conv_rows = [cw_ref[k:k + 1, :] for k in range(B_CONV)]
    if stride == 1:
        row_i = lax.broadcasted_iota(jnp.int32, (CHUNK, CHUNK), 0)
        col_i = lax.broadcasted_iota(jnp.int32, (CHUNK, CHUNK), 1)
        wts = [jnp.where(col_i <= row_i, gw_ref[h], 0.0).astype(_BF16) for h in range(A_HEADS)]
    sub = min(tm, SUB_TILE)
    n_sub = tm // sub
    u0, v0, bg0, cg0, hh0 = (k * A_WIDTH for k in range(5))


    def in_proj_pieces(si):
        srows = slice(si * sub, (si + 1) * sub)

        def piece(c0):
            def emit():
                z = _dot(xn_ref[srows, :], win_ref[:, c0:c0 + mc])
                z_ref[si % 2, :, c0:c0 + mc] = z
                return z[0:1, 0:LANES]
            return emit

        order = list(range(bg0, hh0 + B_WIDTH, mc)) + list(range(u0, bg0, mc))
        return [piece(c0) for c0 in order]

    def b_block(si, r0):
        def emit(after):
            loc = slice(r0 - si * sub, r0 - si * sub + rb)
            z = z_ref.at[si % 2]
            xp_ref[h_rows + r0:h_rows + r0 + rb, :] = z[loc, cg0:cg0 + B_WIDTH] * z[loc, hh0:hh0 + B_WIDTH]
            w_rows = conv_rows[:-1] + [_ordered_after(conv_rows[-1], after)]
            conv = _conv_taps(xp_ref, cw_ref, r0, allc, B_CONV, stride, rb, w_rows=w_rows)
            ycat_ref[r0:r0 + rb, A_WIDTH:A_WIDTH + B_WIDTH] = (z[loc, bg0:bg0 + B_WIDTH] * conv).astype(_BF16)
        return emit

    def ln_block(si, r0):
        def emit(after):
            gain = _ordered_after(lng, after)
            for q0 in range(r0, r0 + rb, NORM_BLOCK):
                rows = slice(q0, q0 + NORM_BLOCK)
                loc = slice(q0 - si * sub, q0 - si * sub + NORM_BLOCK)
                v = _ln_rows(jax.nn.gelu(z_ref[si % 2, loc, v0:v0 + A_WIDTH]), gain)
                if v_ref is not None:
                    v_ref[rows, :] = v
                else:
                    vb_ref[rows, :] = v.astype(_BF16)
        return emit

    def gate_items(si):
        s0 = si * sub
        z = z_ref.at[si % 2]
        items = []
        if stride == 1:
            for c0 in range(s0, s0 + sub, 2 * CHUNK):
                for h in range(A_HEADS):
                    def emit(after, c0=c0, h=h):
                        ra = slice(c0, c0 + CHUNK)
                        rb2 = slice(c0 + CHUNK, c0 + 2 * CHUNK)
                        cols = slice(h * CHUNK, (h + 1) * CHUNK)
                        vpair = jnp.concatenate([vb_ref[ra, cols], vb_ref[rb2, cols]], axis=1)
                        mixed = _dot(wts[h], vpair)
                        bias = gb_ref[:, h:h + 1]
                        for half, rws in enumerate((ra, rb2)):
                            m = mixed[:, half * CHUNK:(half + 1) * CHUNK] + bias
                            u = z[rws.start - s0:rws.stop - s0, u0 + h * CHUNK:u0 + (h + 1) * CHUNK]
                            ycat_ref[rws, cols] = (jax.nn.gelu(u) * m).astype(_BF16)
                    items.append(emit)
        else:
            steps = tm // stride
            gw_off = gate_layer * A_HEADS * steps * steps
            gb_off = gate_layer * A_HEADS * steps
            for t in range(s0 // stride, (s0 + sub) // stride):
                def emit(after, t=t):
                    for n0 in range(0, stride, rb):
                        rows = slice(t * stride + n0, t * stride + n0 + rb)
                        for h in range(A_HEADS):
                            cols = slice(h * CHUNK, (h + 1) * CHUNK)
                            mixed = None
                            for s in range(t + 1):
                                w = gw_ref[gw_off + (h * steps + t) * steps + s]
                                term = w * v_ref[s * stride + n0:s * stride + n0 + rb, cols]
                                mixed = term if mixed is None else mixed + term
                            mixed = mixed + gb_ref[gb_off + h * steps + t]
                            u = z[rows.start - s0:rows.stop - s0, u0 + h * CHUNK:u0 + (h + 1) * CHUNK]
                            ycat_ref[rows, cols] = (jax.nn.gelu(u) * mixed).astype(_BF16)
                items.append(emit)
        return items

    def out_proj_pieces(si):
        srows = slice(si * sub, (si + 1) * sub)

        def piece(c0):
            def emit():
                y = _dot(ycat_ref[srows, :], wout_ref[:, c0:c0 + mc])
                out_ref[srows, c0:c0 + mc] = x_ref[srows, c0:c0 + mc] + y
                return y[0:1, 0:LANES]
            return emit

        return [piece(c0) for c0 in range(0, D_MODEL, mc)]

    _norm_to_bf16(x_ref, g_ref, xn_ref, 0, sub)
    for emit in in_proj_pieces(0):
        emit()
    for si in range(n_sub):
        s0 = si * sub
        side = []
        if si + 1 < n_sub:
            _norm_to_bf16(x_ref, g_ref, xn_ref, s0 + sub, sub)
            side += in_proj_pieces(si + 1)
        if si >= 1:
            side += out_proj_pieces(si - 1)
        blocks = ([b_block(si, r0) for r0 in range(s0, s0 + sub, rb)]
                  + [ln_block(si, r0) for r0 in range(s0, s0 + sub, rb)] + gate_items(si))
        after = None
        for emit in blocks:
            emit(after)
            after = side.pop(0)() if side else None
        for emit in side:
            emit()
    for emit in out_proj_pieces(n_sub - 1):
        emit()
    _emit_history(xp_ref, newb_ref, hist, tm, carry=not has_state)
    finish_cast()


def _pool_block(xp_ref, r0, gi, win, stride, rb, pos0, row_pos_static):
    cols = slice(gi * D_GROUP_DIM, (gi + 1) * D_GROUP_DIM)
    hist = POOL_BUF * stride
    h_rows = _round_up(hist, SUBLANES)
    if stride % SUBLANES == 0:
        s = None
        for j in range(win):
            off = h_rows + r0 - j * stride
            term = xp_ref[off:off + rb, cols]
            s = term if s is None else s + term
    else:
        halo = 2 * SUBLANES
        ext = xp_ref[h_rows + r0 - halo:h_rows + r0 + rb, cols]
        shift = 1
        while shift < win:
            ext = ext + pltpu.roll(ext, shift, 0)
            shift *= 2
        s = ext[halo:]
    if row_pos_static is not None:
        return s / float(min(row_pos_static + 1, win))
    pos = pos0 + r0 + lax.broadcasted_iota(jnp.int32, (rb, D_GROUP_DIM), 0)
    cnt = jnp.minimum(pos + 1, win).astype(_F32)
    return s / cnt


def _odd_kernel(*refs, tm, stride, has_state, start_pos, cast_layers, grid_shape):
    ins, srcs, outs, dsts, scr, cast_scr = _split_refs(refs, 12 if has_state else 10, 3, 5, len(cast_layers))
    x_ref, g_ref, win_ref, wout_ref, ccw_ref, ccb_ref, clg_ref, clb_ref, dproj_ref, dscale_ref = ins[:10]
    statec_ref, stated_ref = (ins[10], ins[11]) if has_state else (None, None)
    out_ref, newc_ref, newd_ref = outs
    xn_ref, xpc_ref, xpd_ref, ycat_ref, zb_ref = scr
    finish_cast = _maybe_cast_weights(cast_layers, grid_shape, srcs, dsts, cast_scr)

    rb = SHIFT_ROW_BLOCK if stride == 1 else ROW_BLOCK
    hist_c = (C_CONV - 1) * stride
    hc_rows = _round_up(hist_c, SUBLANES)
    hist_d = POOL_BUF * stride
    hd_rows = _round_up(hist_d, SUBLANES)
    first = pl.program_id(1) == 0
    _init_history(xpc_ref, statec_ref, hist_c, first)
    _init_history(xpd_ref, stated_ref, hist_d, first)
    ccb = ccb_ref[...]
    clg = clg_ref[...]
    clb = clb_ref[...]
    pos0 = start_pos + pl.program_id(1) * tm
    mc = MXU_COLS
    cuts = list(range(0, tm + 1, min(tm, ODD_SUB_TILE)))
    tiles = list(zip(cuts[:-1], cuts[1:]))
    n_sub = len(tiles)


    def in_proj_pieces(s0, s1):
        srows = slice(s0, s1)

        def glu_piece(c0):
            def emit():
                xn = xn_ref[srows, :]
                xpc_ref[hc_rows + s0:hc_rows + s1, c0:c0 + mc] = _dot(xn, win_ref[:, c0:c0 + mc])
                zb = _dot(xn, win_ref[:, C_WIDTH + c0:C_WIDTH + c0 + mc])
                zb_ref[srows, c0:c0 + mc] = zb
                return zb[0:1, 0:LANES]
            return emit

        def pool_in_piece(c0):
            def emit():
                zp = _dot(xn_ref[srows, :], win_ref[:, 2 * C_WIDTH + c0:2 * C_WIDTH + c0 + mc])
                xpd_ref[hd_rows + s0:hd_rows + s1, c0:c0 + mc] = zp
                return zp[0:1, 0:LANES]
            return emit

        return ([glu_piece(c0) for c0 in range(0, C_WIDTH, mc)]
                + [pool_in_piece(c0) for c0 in range(0, D_WIDTH, mc)])

    gated = {}

    def conv_block(s0, s1, r0):
        def emit(after):
            lo = gated.get(s0, s0)
            hi = min(r0 + rb + SUBLANES, s1)
            for g0 in range(lo, hi, rb):
                grows = slice(hc_rows + g0, hc_rows + min(g0 + rb, hi))
                xpc_ref[grows, :] = xpc_ref[grows, :] * jax.nn.sigmoid(zb_ref[g0:min(g0 + rb, hi), :])
            gated[s0] = hi
            parts = []
            for c0 in range(0, C_WIDTH, LANES):
                cols = slice(c0, c0 + LANES)
                if stride % SUBLANES == 0:
                    parts.append(_conv_taps(xpc_ref, ccw_ref, r0, cols, C_CONV, stride, rb))
                else:
                    parts.append(_conv_long_stride1(xpc_ref, ccw_ref, r0, cols, C_CONV, rb))
            cc = jnp.concatenate(parts, axis=-1) + _ordered_after(ccb, after)
            ycat_ref[r0:r0 + rb, 0:C_WIDTH] = jax.nn.silu(_ln_rows(cc, clg, clb)).astype(_BF16)
        return emit

    def pool_block(r0):
        def emit(after):
            for gi, win in enumerate(POOL_WINDOWS):
                cols = slice(gi * D_GROUP_DIM, (gi + 1) * D_GROUP_DIM)
                if stride == 1:
                    static_pos = None if r0 < POOL_BUF else POOL_BUF
                else:
                    static_pos = start_pos + r0 // stride
                pooled = _pool_block(xpd_ref, r0, gi, win, stride, rb, pos0, static_pos)
                diff = (pooled - xpd_ref[hd_rows + r0:hd_rows + r0 + rb, cols]).astype(_BF16)
                scale = dscale_ref[:, cols]
                if gi == 0:
                    scale = _ordered_after(scale, after)
                yd = _dot(diff, dproj_ref[gi]) * scale
                ycat_ref[r0:r0 + rb, C_WIDTH + gi * D_GROUP_DIM:C_WIDTH + (gi + 1) * D_GROUP_DIM] = yd.astype(_BF16)
        return emit

    def out_proj_pieces(s0, s1):
        srows = slice(s0, s1)

        def piece(c0):
            def emit():
                y = _dot(ycat_ref[srows, :], wout_ref[:, c0:c0 + mc])
                out_ref[srows, c0:c0 + mc] = x_ref[srows, c0:c0 + mc] + y
                return y[0:1, 0:LANES]
            return emit

        return [piece(c0) for c0 in range(0, D_MODEL, mc)]

    _norm_to_bf16(x_ref, g_ref, xn_ref, 0, tiles[0][1])
    for emit in in_proj_pieces(*tiles[0]):
        emit()
    for si, (s0, s1) in enumerate(tiles):
        xpc_ref[hc_rows + s1:hc_rows + s1 + SUBLANES, :] = jnp.zeros((SUBLANES, C_WIDTH), _F32)
        side = []
        if si + 1 < n_sub:
            n0, n1 = tiles[si + 1]
            _norm_to_bf16(x_ref, g_ref, xn_ref, n0, n1 - n0)
            side += in_proj_pieces(n0, n1)
        if si >= 1:
            side += out_proj_pieces(*tiles[si - 1])
        blocks = [conv_block(s0, s1, r0) for r0 in range(s0, s1, rb)] + [pool_block(r0) for r0 in range(s0, s1, rb)]
        after = None
        for emit in blocks:
            emit(after)
            after = side.pop(0)() if side else None
        for emit in side:
            emit()
    for emit in out_proj_pieces(*tiles[-1]):
        emit()
    _emit_history(xpc_ref, newc_ref, hist_c, tm, carry=not has_state)
    _emit_history(xpd_ref, newd_ref, hist_d, tm, carry=not has_state)
    finish_cast()


def _ffn_kernel(*refs, stride, has_state, final_norm):
    refs = list(refs)
    x_ref, g_ref, win_ref, wout_ref, cw_ref = refs[:5]
    del refs[:5]
    state_ref = refs.pop(0) if has_state else None
    gfin_ref = refs.pop(0) if final_norm else None
    out_ref, newf_ref, xn_ref, gp_ref, hist_ref, h_ref = refs
    rb = FFN_ROW_BLOCK if stride == 1 else ROW_BLOCK
    hist = (FFN_CONV - 1) * stride
    h_rows = _round_up(hist, SUBLANES)
    _init_history(hist_ref, state_ref, hist, pl.program_id(1) == 0)
    n_sub, sub, _ = x_ref.shape

    def sub_tile(i, carry):
        xs = x_ref.at[i]
        os = out_ref.at[i]
        _norm_to_bf16(xs, g_ref, xn_ref, 0, sub)
        xn = xn_ref[...]
        for j, c0 in enumerate(range(0, D_FF, FF_CHUNK)):
            cols = slice(c0, c0 + FF_CHUNK)
            slot = slice((j % 2) * FF_CHUNK, (j % 2 + 1) * FF_CHUNK)
            gp_ref[0:h_rows, slot] = hist_ref[:, cols]
            zg = _dot(xn, win_ref[:, c0:c0 + FF_CHUNK])
            zu = _dot(xn, win_ref[:, D_FF + c0:D_FF + c0 + FF_CHUNK])
            for r0 in range(0, sub, rb):
                rows = slice(r0, r0 + rb)
                gp_ref[h_rows + r0:h_rows + r0 + rb, slot] = zg[rows]
                conv = _conv_taps(gp_ref, cw_ref, r0, cols, FFN_CONV, stride, rb, xcols=slot)
                h_ref[rows, cols] = (jax.nn.gelu(conv) * zu[rows]).astype(_BF16)
            hist_ref[:, cols] = gp_ref[sub:sub + h_rows, slot]

        y = xs[...] + _dot(h_ref[...], wout_ref[...])
        if final_norm:
            gfin = gfin_ref[...]
            os[...] = y
            for r0 in range(0, sub, NORM_BLOCK):
                rows = slice(r0, r0 + NORM_BLOCK)
                os[rows, :] = _rms_rows(os[rows, :], gfin)
        else:
            os[...] = y
        return carry

    lax.fori_loop(0, n_sub, sub_tile, 0)
    newf_ref[0] = hist_ref[h_rows - hist:h_rows, :]


def _const_spec(shape):
    nd = len(shape)
    return pl.BlockSpec(shape, lambda b, t: (0,) * nd, pipeline_mode=pl.Buffered(1))


def _layer_spec(arr, layer):
    nd = arr.ndim - 1
    return pl.BlockSpec((None,) + arr.shape[1:], lambda b, t: (layer,) + (0,) * nd, pipeline_mode=pl.Buffered(1))


def _smem_spec():
    return pl.BlockSpec(memory_space=pltpu.SMEM)


def _hbm_spec():
    return pl.BlockSpec(memory_space=pl.ANY)


def _tile_spec(tm, nt, width):
    return pl.BlockSpec((tm, width), lambda b, t: (b * nt + t, 0))


def _state_spec(layer, rows, width):
    return pl.BlockSpec((None, 1, rows, width), lambda b, t: (layer, b, 0, 0))


def _new_state_spec(rows, width):
    return pl.BlockSpec((1, rows, width), lambda b, t: (b, 0, 0))


def _params():
    return pltpu.CompilerParams(dimension_semantics=("arbitrary", "arbitrary"), vmem_limit_bytes=VMEM_LIMIT)


def _cast_plumbing(cast_jobs, n_steps):
    in_specs, args, out_specs, out_shape, stage_in, stage_out = [], [], [], [], [], []
    for arr, _ in cast_jobs:
        _, r, c = arr.shape
        rc = r // n_steps
        assert rc * n_steps == r and rc % BF16_ROWS == 0, (arr.shape, n_steps)
        in_specs.append(_hbm_spec())
        args.append(arr)
        out_specs.append(_hbm_spec())
        out_shape.append(jax.ShapeDtypeStruct((r, c), _BF16))
        stage_in.append(pltpu.VMEM((2, rc, c), _F32))
        stage_out.append(pltpu.VMEM((2, rc, c), _BF16))
    scratch = stage_in + stage_out
    if cast_jobs:
        n = len(cast_jobs)
        scratch += [pltpu.SemaphoreType.DMA((n, 2)), pltpu.SemaphoreType.DMA((n, 2))]
    return in_specs, args, out_specs, out_shape, scratch, tuple(layer for _, layer in cast_jobs)


def _even_call(x2, nb, nt, stride, state, layer, win, wout, prm, cast_jobs, name):
    has_state = state is not None
    i = layer // 2
    tm = x2.shape[0] // (nb * nt)
    hist = (B_CONV - 1) * stride
    h_rows = _round_up(hist, SUBLANES)
    c_in, c_args, c_out_specs, c_out_shape, c_scratch, cast_layers = _cast_plumbing(cast_jobs, nb * nt)
    in_specs = [_tile_spec(tm, nt, D_MODEL), _layer_spec(prm['norm_mix_g'], layer), _const_spec(win.shape),
                _const_spec(wout.shape), _layer_spec(prm['a_ln_g'], i), _layer_spec(prm['b_conv_w'], i)]
    args = [x2, prm['norm_mix_g'], win, wout, prm['a_ln_g'], prm['b_conv_w']]
    if has_state:
        in_specs += [_smem_spec(), _smem_spec(), _state_spec(i, hist, B_WIDTH)]
        args += [prm['a_ws_steps'], prm['a_bs_steps'], state]
    else:
        in_specs += [_layer_spec(prm['a_ws'], i), _layer_spec(prm['a_bs_t'], i)]
        args += [prm['a_ws'], prm['a_bs_t']]
    out_shape = [jax.ShapeDtypeStruct(x2.shape, _F32), jax.ShapeDtypeStruct((nb, hist, B_WIDTH), _F32)]
    out_specs = [_tile_spec(tm, nt, D_MODEL), _new_state_spec(hist, B_WIDTH)]
    if has_state:
        out_shape.append(jax.ShapeDtypeStruct((x2.shape[0], A_WIDTH), _F32))
        out_specs.append(_tile_spec(tm, nt, A_WIDTH))
    sub = min(tm, SUB_TILE)
    scratch = [pltpu.VMEM((tm, D_MODEL), _BF16), pltpu.VMEM((h_rows + tm, B_WIDTH), _F32),
               pltpu.VMEM((tm, A_WIDTH + B_WIDTH), _BF16), pltpu.VMEM((tm, A_WIDTH), _BF16),
               pltpu.VMEM((min(tm // sub, 2), sub, win.shape[1]), _F32)]
    return pl.pallas_call(
        functools.partial(_even_kernel, tm=tm, stride=stride, has_state=has_state, gate_layer=i,
                          cast_layers=cast_layers, grid_shape=(nb, nt)),
        grid=(nb, nt), in_specs=in_specs + c_in, out_specs=out_specs + c_out_specs,
        out_shape=out_shape + c_out_shape, scratch_shapes=scratch + c_scratch,
        compiler_params=_params(), name=name)(*args, *c_args)


def _odd_call(x2, nb, nt, stride, state_c, state_d, start_pos, layer, win, wout, prm, cast_jobs, name):
    has_state = state_c is not None
    i = layer // 2
    hist_c = (C_CONV - 1) * stride
    hist_d = POOL_BUF * stride
    hc_rows = _round_up(hist_c, SUBLANES)
    hd_rows = _round_up(hist_d, SUBLANES)
    tm = x2.shape[0] // (nb * nt)
    c_in, c_args, c_out_specs, c_out_shape, c_scratch, cast_layers = _cast_plumbing(cast_jobs, nb * nt)
    names = ['c_conv_w', 'c_conv_b', 'c_ln_g', 'c_ln_b', 'd_proj', 'd_scale']
    in_specs = [_tile_spec(tm, nt, D_MODEL), _layer_spec(prm['norm_mix_g'], layer), _const_spec(win.shape),
                _const_spec(wout.shape)] + [_layer_spec(prm[k], i) for k in names]
    args = [x2, prm['norm_mix_g'], win, wout] + [prm[k] for k in names]
    if has_state:
        in_specs += [_state_spec(i, hist_c, C_WIDTH), _state_spec(i, hist_d, D_WIDTH)]
        args += [state_c, state_d]
    out_shape = [jax.ShapeDtypeStruct(x2.shape, _F32), jax.ShapeDtypeStruct((nb, hist_c, C_WIDTH), _F32),
                 jax.ShapeDtypeStruct((nb, hist_d, D_WIDTH), _F32)]
    out_specs = [_tile_spec(tm, nt, D_MODEL), _new_state_spec(hist_c, C_WIDTH), _new_state_spec(hist_d, D_WIDTH)]
    scratch = [pltpu.VMEM((tm, D_MODEL), _BF16), pltpu.VMEM((hc_rows + tm + SUBLANES, C_WIDTH), _F32),
               pltpu.VMEM((hd_rows + tm, D_WIDTH), _F32), pltpu.VMEM((tm, C_WIDTH + D_WIDTH), _BF16),
               pltpu.VMEM((tm, C_WIDTH), _F32)]
    return pl.pallas_call(
        functools.partial(_odd_kernel, tm=tm, stride=stride, has_state=has_state, start_pos=start_pos,
                          cast_layers=cast_layers, grid_shape=(nb, nt)),
        grid=(nb, nt), in_specs=in_specs + c_in, out_specs=out_specs + c_out_specs,
        out_shape=out_shape + c_out_shape, scratch_shapes=scratch + c_scratch,
        compiler_params=_params(), name=name)(*args, *c_args)


def _ffn_call(x2, nb, nt, stride, state, layer, final_norm, win, wout, prm, name):
    has_state = state is not None
    hist = (FFN_CONV - 1) * stride
    h_rows = _round_up(hist, SUBLANES)
    tm = x2.shape[0] // (nb * nt)
    sub = min(tm, SUB_TILE)
    n_sub = tm // sub
    x3 = x2.reshape(-1, sub, D_MODEL)
    x_spec = pl.BlockSpec((n_sub, sub, D_MODEL), lambda b, t: (b * nt + t, 0, 0))
    in_specs = [x_spec, _layer_spec(prm['norm_ffn_g'], layer), _const_spec(win.shape), _const_spec(wout.shape),
                _layer_spec(prm['ffn_conv_w'], layer)]
    args = [x3, prm['norm_ffn_g'], win, wout, prm['ffn_conv_w']]
    if has_state:
        in_specs.append(_state_spec(layer, hist, D_FF))
        args.append(state)
    if final_norm:
        in_specs.append(_const_spec((1, D_MODEL)))
        args.append(prm['norm_final_g'])
    out_shape = [jax.ShapeDtypeStruct(x3.shape, _F32), jax.ShapeDtypeStruct((nb, hist, D_FF), _F32)]
    out_specs = [x_spec, _new_state_spec(hist, D_FF)]
    scratch = [pltpu.VMEM((sub, D_MODEL), _BF16), pltpu.VMEM((h_rows + sub, 2 * FF_CHUNK), _F32),
               pltpu.VMEM((h_rows, D_FF), _F32), pltpu.VMEM((sub, D_FF), _BF16)]
    y3, newf = pl.pallas_call(
        functools.partial(_ffn_kernel, stride=stride, has_state=has_state, final_norm=final_norm),
        grid=(nb, nt), in_specs=in_specs, out_specs=out_specs, out_shape=out_shape,
        scratch_shapes=scratch, compiler_params=_params(), name=name)(*args)
    return y3.reshape(x2.shape), newf


def _to_tiles(a, ns):
    *lead, n, l, c = a.shape
    k = len(lead)
    a = a.reshape(*lead, n // ns, ns, l, c)
    a = a.transpose(*range(k), k, k + 2, k + 1, k + 3)
    return a.reshape(*lead, n // ns, l * ns, c)


def _from_tiles(a, ns, l):
    *lead, nb, _, c = a.shape
    k = len(lead)
    a = a.reshape(*lead, nb, l, ns, c)
    a = a.transpose(*range(k), k, k + 2, k + 1, k + 3)
    return a.reshape(*lead, nb * ns, l, c)


def _trunk(x2, nb, nt, stride, states, start_pos, prm, wb, raw, tag):
    new_a, new_b, new_c, new_d, new_f = [], [], [], [], []
    depth = prm['ffn_conv_w'].shape[0]
    for layer in range(depth):
        cast_keys, cast_jobs = [], []
        if raw is not None:
            cast_keys.append(('ffn', layer))
            cast_jobs += [(raw['w_ffn_in'], layer), (raw['w_ffn_out'], layer)]
            if layer + 1 < depth:
                nxt = 'even' if (layer + 1) % 2 == 0 else 'odd'
                cast_keys.append(('mix', layer + 1))
                cast_jobs += [(raw['w_in_' + nxt], (layer + 1) // 2), (raw['w_out_' + nxt], (layer + 1) // 2)]
        win, wout = wb[('mix', layer)]
        if layer % 2 == 0:
            st = None if states is None else states['b']
            res = _even_call(x2, nb, nt, stride, st, layer, win, wout, prm, cast_jobs, name=f"{tag}_mix{layer}")
            n_own = 2 if states is None else 3
            x2 = res[0]
            new_b.append(res[1])
            if states is not None:
                new_a.append(res[2])
        else:
            st_c = None if states is None else states['c']
            st_d = None if states is None else states['d']
            res = _odd_call(x2, nb, nt, stride, st_c, st_d, start_pos, layer, win, wout, prm, cast_jobs,
                            name=f"{tag}_mix{layer}")
            n_own = 3
            x2 = res[0]
            new_c.append(res[1])
            new_d.append(res[2])
        for k, key in enumerate(cast_keys):
            wb[key] = (res[n_own + 2 * k], res[n_own + 2 * k + 1])
        st_f = None if states is None else states['f']
        win, wout = wb[('ffn', layer)]
        x2, nf = _ffn_call(x2, nb, nt, stride, st_f, layer, layer == depth - 1, win, wout, prm,
                           name=f"{tag}_ffn{layer}")
        new_f.append(nf)
    return x2, new_a, new_b, new_c, new_d, new_f


def kernel(x_prompt, x_sample, state_b_conv, state_c_conv, state_d_pool, state_ffn_conv, norm_mix_g, norm_ffn_g, norm_final_g, w_in_even, w_out_even, a_ln_g, a_ws, a_bs, b_conv_w, w_in_odd, w_out_odd, c_conv_w, c_conv_b, c_ln_g, c_ln_b, d_proj, d_scale, w_ffn_in, ffn_conv_w, w_ffn_out):
    batch, seq, d = x_prompt.shape
    dec_batch, dec_seq, _ = x_sample.shape
    ns = SAMPLE_TILE // dec_seq

    def rows(p):
        return p.reshape(p.shape[0], 1, p.shape[1])

    prm = dict(norm_mix_g=rows(norm_mix_g), norm_ffn_g=rows(norm_ffn_g), norm_final_g=norm_final_g.reshape(1, -1),
               a_ln_g=rows(a_ln_g), a_ws=a_ws, a_bs_t=jnp.swapaxes(a_bs, 1, 2),
               a_ws_steps=a_ws[:, :, :dec_seq, :dec_seq].reshape(-1), a_bs_steps=a_bs[:, :, :dec_seq].reshape(-1),
               b_conv_w=b_conv_w,
               c_conv_w=jnp.pad(c_conv_w, ((0, 0), (0, _round_up(C_CONV, SUBLANES) - C_CONV), (0, 0))),
               c_conv_b=rows(c_conv_b), c_ln_g=rows(c_ln_g), c_ln_b=rows(c_ln_b), d_proj=d_proj.astype(_BF16),
               d_scale=rows(d_scale), ffn_conv_w=ffn_conv_w)
    raw = dict(w_in_even=w_in_even, w_out_even=w_out_even, w_in_odd=w_in_odd, w_out_odd=w_out_odd,
               w_ffn_in=w_ffn_in, w_ffn_out=w_ffn_out)
    wb = {('mix', 0): (w_in_even[0].astype(_BF16), w_out_even[0].astype(_BF16))}

    yp, _, b_p, c_p, d_p, f_p = _trunk(x_prompt.reshape(batch * seq, d), batch, seq // PROMPT_TILE, 1, None, 0,
                                       prm, wb, raw, "p")

    states = dict(b=_to_tiles(state_b_conv, ns), c=_to_tiles(state_c_conv, ns), d=_to_tiles(state_d_pool, ns),
                  f=_to_tiles(state_ffn_conv, ns))
    xs = _to_tiles(x_sample, ns).reshape(dec_batch * dec_seq, d)
    ys, a_s, b_s, c_s, d_s, f_s = _trunk(xs, dec_batch // ns, 1, ns, states, PAST_LEN, prm, wb, None, "s")

    def untile(parts, l):
        return _from_tiles(jnp.stack(parts), ns, l)

    y_prompt = yp.reshape(batch, seq, d)
    y_sample = _from_tiles(ys.reshape(dec_batch // ns, dec_seq * ns, d), ns, dec_seq)
    new_a = untile([a.reshape(dec_batch // ns, dec_seq * ns, A_WIDTH) for a in a_s], dec_seq)
    return (y_prompt, y_sample, new_a, jnp.stack(b_p), untile(b_s, B_CONV - 1), jnp.stack(c_p),
            untile(c_s, C_CONV - 1), jnp.stack(d_p), untile(d_s, POOL_BUF), jnp.stack(f_p),
            untile(f_s, FFN_CONV - 1))
```

```python
import functools

import jax
import jax.numpy as jnp
from jax import lax
from jax.experimental import pallas as pl
from jax.experimental.pallas import tpu as pltpu

D_MODEL = 1024
CHUNK = 128
A_HEADS = 4
A_WIDTH = 512
B_WIDTH = 512
B_CONV = 3
C_WIDTH = 512
C_CONV = 31
D_WIDTH = 512
POOL_WINDOWS = (2, 4, 8, 16)
D_GROUP_DIM = 128
POOL_BUF = 15
D_FF = 2816
FFN_CONV = 3
PAST_LEN = 16384
EPS = 1e-6

SUBLANES = 8
LANES = 128
BF16_ROWS = 16
MXU_COLS = 256
PROMPT_TILE = 1024
SAMPLE_TILE = 512
SUB_TILE = 512
ODD_SUB_TILE = 512
ROW_BLOCK = 64
SHIFT_ROW_BLOCK = 128
FFN_ROW_BLOCK = 256
NORM_BLOCK = 32
FF_CHUNK = 256
VMEM_LIMIT = 56 * 1024 * 1024

_BF16 = jnp.bfloat16
_F32 = jnp.float32


def _round_up(n, m):
    return (n + m - 1) // m * m


def _dot(a, b):
    return jnp.dot(a, b, preferred_element_type=_F32)


def _rms_rows(x, g):
    y = x * lax.rsqrt(jnp.mean(x * x, axis=-1, keepdims=True) + EPS)
    return y * g


def _ln_rows(x, g, b=None):
    mu = jnp.mean(x, axis=-1, keepdims=True)
    xc = x - mu
    y = xc * lax.rsqrt(jnp.mean(xc * xc, axis=-1, keepdims=True) + EPS) * g
    if b is not None:
        y = y + b
    return y


def _norm_to_bf16(x_ref, g_ref, xn_ref, s0, sub):
    g = g_ref[...]
    for r0 in range(s0, s0 + sub, NORM_BLOCK):
        rows = slice(r0, r0 + NORM_BLOCK)
        xn_ref[rows, :] = _rms_rows(x_ref[rows, :], g).astype(_BF16)


def _init_history(xp_ref, state_ref, hist, first_tile):
    h_rows = _round_up(hist, SUBLANES)
    if state_ref is None:
        @pl.when(first_tile)
        def _():
            xp_ref[0:h_rows, :] = jnp.zeros((h_rows, xp_ref.shape[1]), _F32)
    else:
        xp_ref[h_rows - hist:h_rows, :] = state_ref[0]


def _emit_history(xp_ref, new_ref, hist, tm, carry):
    h_rows = _round_up(hist, SUBLANES)
    tail = xp_ref[h_rows + tm - hist:h_rows + tm, :]
    new_ref[0] = tail
    if carry:
        xp_ref[h_rows - hist:h_rows, :] = tail


def _conv_taps(xp_ref, w_ref, r0, cols, ktaps, stride, rb, xcols=None, w_rows=None):
    xcols = cols if xcols is None else xcols
    hist = (ktaps - 1) * stride
    base = _round_up(hist, SUBLANES) - hist + r0
    acc = None
    for k in range(ktaps):
        w_k = w_ref[k:k + 1, cols] if w_rows is None else w_rows[k]
        term = w_k * xp_ref[base + k * stride:base + k * stride + rb, xcols]
        acc = term if acc is None else acc + term
    return acc


def _conv_long_stride1(xp_ref, w_ref, r0, cols, ktaps, rb):
    hist = ktaps - 1
    base = _round_up(hist, SUBLANES) - hist
    n = rb + SUBLANES
    p = None
    for r in range(SUBLANES - 1, -1, -1):
        z = None
        for k in range(ktaps):
            if (base + k) % SUBLANES != r:
                continue
            off = r0 + (base + k) - r
            term = w_ref[k:k + 1, cols] * xp_ref[off:off + n, cols]
            z = term if z is None else z + term
        if p is not None:
            p = pltpu.roll(p, n - 1, 0)
            p = p if z is None else p + z
        else:
            p = z
    return p[0:rb]


def _ordered_after(row, token):
    if token is None:
        return row
    half = jnp.uint32(16)
    bits = lax.bitcast_convert_type(token, jnp.uint32)
    zero = lax.shift_right_logical(lax.shift_right_logical(bits, half), half)
    head = lax.bitcast_convert_type(lax.bitcast_convert_type(row[:, 0:LANES], jnp.uint32) + zero, _F32)
    if row.shape[1] == LANES:
        return head
    return jnp.concatenate([head, row[:, LANES:]], axis=1)


def _cast_weights_step(step, n_steps, layers, srcs, dsts, stage_in, stage_out, sem_in, sem_out):
    n_jobs = len(layers)

    def in_copy(j, s, slot):
        rc = stage_in[j].shape[1]
        return pltpu.make_async_copy(srcs[j].at[layers[j], pl.ds(s * rc, rc), :], stage_in[j].at[slot],
                                     sem_in.at[j, slot])

    def out_copy(j, s, slot):
        rc = stage_out[j].shape[1]
        return pltpu.make_async_copy(stage_out[j].at[slot], dsts[j].at[pl.ds(s * rc, rc), :], sem_out.at[j, slot])

    slot = lax.rem(step, 2)

    @pl.when(step == 0)
    def _():
        for j in range(n_jobs):
            in_copy(j, 0, 0).start()

    for j in range(n_jobs):
        in_copy(j, step, slot).wait()

    @pl.when(step + 1 < n_steps)
    def _():
        for j in range(n_jobs):
            in_copy(j, step + 1, 1 - slot).start()

    @pl.when(step >= 2)
    def _():
        for j in range(n_jobs):
            out_copy(j, step - 2, slot).wait()

    for j in range(n_jobs):
        src = stage_in[j].at[slot]
        dst = stage_out[j].at[slot]
        for r0 in range(0, src.shape[0], BF16_ROWS):
            dst[r0:r0 + BF16_ROWS, :] = src[r0:r0 + BF16_ROWS, :].astype(_BF16)
        out_copy(j, step, slot).start()

    def finish():
        @pl.when(step == n_steps - 1)
        def _():
            for j in range(n_jobs):
                if n_steps > 1:
                    out_copy(j, step - 1, 1 - slot).wait()
                out_copy(j, step, slot).wait()

    return finish


def _split_refs(refs, n_in, n_out, n_scratch, n_jobs):
    refs = list(refs)
    a = n_in
    b = a + n_jobs
    c = b + n_out
    d = c + n_jobs
    e = d + n_scratch
    return refs[:a], refs[a:b], refs[b:c], refs[c:d], refs[d:e], refs[e:]


def _maybe_cast_weights(cast_layers, grid_shape, srcs, dsts, cast_scratch):
    n_jobs = len(cast_layers)
    if n_jobs == 0:
        return lambda: None
    stage_in = cast_scratch[:n_jobs]
    stage_out = cast_scratch[n_jobs:2 * n_jobs]
    sem_in, sem_out = cast_scratch[2 * n_jobs:]
    nb, nt = grid_shape
    step = pl.program_id(0) * nt + pl.program_id(1)
    return _cast_weights_step(step, nb * nt, cast_layers, srcs, dsts, stage_in, stage_out, sem_in, sem_out)


def _even_kernel(*refs, tm, stride, has_state, gate_layer, cast_layers, grid_shape):
    ins, srcs, outs, dsts, scr, cast_scr = _split_refs(
        refs, 9 if has_state else 8, 3 if has_state else 2, 5, len(cast_layers))
    x_ref, g_ref, win_ref, wout_ref, lng_ref, cw_ref, gw_ref, gb_ref = ins[:8]
    state_ref = ins[8] if has_state else None
    out_ref, newb_ref = outs[:2]
    v_ref = outs[2] if has_state else None
    xn_ref, xp_ref, ycat_ref, vb_ref, z_ref = scr
    finish_cast = _maybe_cast_weights(cast_layers, grid_shape, srcs, dsts, cast_scr)

    rb = SHIFT_ROW_BLOCK if stride == 1 else ROW_BLOCK
    mc = MXU_COLS
    hist = (B_CONV - 1) * stride
    h_rows = _round_up(hist, SUBLANES)
    _init_history(xp_ref, state_ref, hist, pl.program_id(1) == 0)
    lng = lng_ref[...]
    allc = slice(0, B_WIDTH)
    conv_rows = [cw_ref[k:k + 1, :] for k in range(B_CONV)]
    if stride == 1:
        row_i = lax.broadcasted_iota(jnp.int32, (CHUNK, CHUNK), 0)
        col_i = lax.broadcasted_iota(jnp.int32, (CHUNK, CHUNK), 1)
        wts = [jnp.where(col_i <= row_i, gw_ref[h], 0.0).astype(_BF16) for h in range(A_HEADS)]
    sub = min(tm, SUB_TILE)
    n_sub = tm // sub
    u0, v0, bg0, cg0, hh0 = (k * A_WIDTH for k in range(5))


    def in_proj_pieces(si):
        srows = slice(si * sub, (si + 1) * sub)

        def piece(c0):
            def emit():
                z = _dot(xn_ref[srows, :], win_ref[:, c0:c0 + mc])
                z_ref[si % 2, :, c0:c0 + mc] = z
                return z[0:1, 0:LANES]
            return emit

        order = list(range(bg0, hh0 + B_WIDTH, mc)) + list(range(u0, bg0, mc))
        return [piece(c0) for c0 in order]

    def b_block(si, r0):
        def emit(after):
            loc = slice(r0 - si * sub, r0 - si * sub + rb)
            z = z_ref.at[si % 2]
            xp_ref[h_rows + r0:h_rows + r0 + rb, :] = z[loc, cg0:cg0 + B_WIDTH] * z[loc, hh0:hh0 + B_WIDTH]
            w_rows = conv_rows[:-1] + [_ordered_after(conv_rows[-1], after)]
            conv = _conv_taps(xp_ref, cw_ref, r0, allc, B_CONV, stride, rb, w_rows=w_rows)
            ycat_ref[r0:r0 + rb, A_WIDTH:A_WIDTH + B_WIDTH] = (z[loc, bg0:bg0 + B_WIDTH] * conv).astype(_BF16)
        return emit

    def ln_block(si, r0):
        def emit(after):
            gain = _ordered_after(lng, after)
            for q0 in range(r0, r0 + rb, NORM_BLOCK):
                rows = slice(q0, q0 + NORM_BLOCK)
                loc = slice(q0 - si * sub, q0 - si * sub + NORM_BLOCK)
                v = _ln_rows(jax.nn.gelu(z_ref[si % 2, loc, v0:v0 + A_WIDTH]), gain)
                if v_ref is not None:
                    v_ref[rows, :] = v
                else:
                    vb_ref[rows, :] = v.astype(_BF16)
        return emit

    def gate_items(si):
        s0 = si * sub
        z = z_ref.at[si % 2]
        items = []
        if stride == 1:
            for c0 in range(s0, s0 + sub, 2 * CHUNK):
                for h in range(A_HEADS):
                    def emit(after, c0=c0, h=h):
                        ra = slice(c0, c0 + CHUNK)
                        rb2 = slice(c0 + CHUNK, c0 + 2 * CHUNK)
                        cols = slice(h * CHUNK, (h + 1) * CHUNK)
                        vpair = jnp.concatenate([vb_ref[ra, cols], vb_ref[rb2, cols]], axis=1)
                        mixed = _dot(wts[h], vpair)
                        bias = gb_ref[:, h:h + 1]
                        for half, rws in enumerate((ra, rb2)):
                            m = mixed[:, half * CHUNK:(half + 1) * CHUNK] + bias
                            u = z[rws.start - s0:rws.stop - s0, u0 + h * CHUNK:u0 + (h + 1) * CHUNK]
                            ycat_ref[rws, cols] = (jax.nn.gelu(u) * m).astype(_BF16)
                    items.append(emit)
        else:
            steps = tm // stride
            gw_off = gate_layer * A_HEADS * steps * steps
            gb_off = gate_layer * A_HEADS * steps
            for t in range(s0 // stride, (s0 + sub) // stride):
                def emit(after, t=t):
                    for n0 in range(0, stride, rb):
                        rows = slice(t * stride + n0, t * stride + n0 + rb)
                        for h in range(A_HEADS):
                            cols = slice(h * CHUNK, (h + 1) * CHUNK)
                            mixed = None
                            for s in range(t + 1):
                                w = gw_ref[gw_off + (h * steps + t) * steps + s]
                                term = w * v_ref[s * stride + n0:s * stride + n0 + rb, cols]
                                mixed = term if mixed is None else mixed + term
                            mixed = mixed + gb_ref[gb_off + h * steps + t]
                            u = z[rows.start - s0:rows.stop - s0, u0 + h * CHUNK:u0 + (h + 1) * CHUNK]
                            ycat_ref[rows, cols] = (jax.nn.gelu(u) * mixed).astype(_BF16)
                items.append(emit)
        return items

    def out_proj_pieces(si):
        srows = slice(si * sub, (si + 1) * sub)

        def piece(c0):
            def emit():
                y = _dot(ycat_ref[srows, :], wout_ref[:, c0:c0 + mc])
                out_ref[srows, c0:c0 + mc] = x_ref[srows, c0:c0 + mc] + y
                return y[0:1, 0:LANES]
            return emit

        return [piece(c0) for c0 in range(0, D_MODEL, mc)]

    _norm_to_bf16(x_ref, g_ref, xn_ref, 0, sub)
    for emit in in_proj_pieces(0):
        emit()
    for si in range(n_sub):
        s0 = si * sub
        side = []
        if si + 1 < n_sub:
            _norm_to_bf16(x_ref, g_ref, xn_ref, s0 + sub, sub)
            side += in_proj_pieces(si + 1)
        if si >= 1:
            side += out_proj_pieces(si - 1)
        blocks = ([b_block(si, r0) for r0 in range(s0, s0 + sub, rb)]
                  + [ln_block(si, r0) for r0 in range(s0, s0 + sub, rb)] + gate_items(si))
        after = None
        for emit in blocks:
            emit(after)
            after = side.pop(0)() if side else None
        for emit in side:
            emit()
    for emit in out_proj_pieces(n_sub - 1):
        emit()
    _emit_history(xp_ref, newb_ref, hist, tm, carry=not has_state)
    finish_cast()


def _pool_block(xp_ref, r0, gi, win, stride, rb, pos0, row_pos_static):
    cols = slice(gi * D_GROUP_DIM, (gi + 1) * D_GROUP_DIM)
    hist = POOL_BUF * stride
    h_rows = _round_up(hist, SUBLANES)
    if stride % SUBLANES == 0:
        s = None
        for j in range(win):
            off = h_rows + r0 - j * stride
            term = xp_ref[off:off + rb, cols]
            s = term if s is None else s + term
    else:
        halo = 2 * SUBLANES
        ext = xp_ref[h_rows + r0 - halo:h_rows + r0 + rb, cols]
        shift = 1
        while shift < win:
            ext = ext + pltpu.roll(ext, shift, 0)
            shift *= 2
        s = ext[halo:]
    if row_pos_static is not None:
        return s / float(min(row_pos_static + 1, win))
    pos = pos0 + r0 + lax.broadcasted_iota(jnp.int32, (rb, D_GROUP_DIM), 0)
    cnt = jnp.minimum(pos + 1, win).astype(_F32)
    return s / cnt


def _odd_kernel(*refs, tm, stride, has_state, start_pos, cast_layers, grid_shape):
    ins, srcs, outs, dsts, scr, cast_scr = _split_refs(refs, 12 if has_state else 10, 3, 5, len(cast_layers))
    x_ref, g_ref, win_ref, wout_ref, ccw_ref, ccb_ref, clg_ref, clb_ref, dproj_ref, dscale_ref = ins[:10]
    statec_ref, stated_ref = (ins[10], ins[11]) if has_state else (None, None)
    out_ref, newc_ref, newd_ref = outs
    xn_ref, xpc_ref, xpd_ref, ycat_ref, zb_ref = scr
    finish_cast = _maybe_cast_weights(cast_layers, grid_shape, srcs, dsts, cast_scr)

    rb = SHIFT_ROW_BLOCK if stride == 1 else ROW_BLOCK
    hist_c = (C_CONV - 1) * stride
    hc_rows = _round_up(hist_c, SUBLANES)
    hist_d = POOL_BUF * stride
    hd_rows = _round_up(hist_d, SUBLANES)
    first = pl.program_id(1) == 0
    _init_history(xpc_ref, statec_ref, hist_c, first)
    _init_history(xpd_ref, stated_ref, hist_d, first)
    ccb = ccb_ref[...]
    clg = clg_ref[...]
    clb = clb_ref[...]
    pos0 = start_pos + pl.program_id(1) * tm
    mc = MXU_COLS
    cuts = list(range(0, tm + 1, min(tm, ODD_SUB_TILE)))
    tiles = list(zip(cuts[:-1], cuts[1:]))
    n_sub = len(tiles)


    def in_proj_pieces(s0, s1):
        srows = slice(s0, s1)

        def glu_piece(c0):
            def emit():
                xn = xn_ref[srows, :]
                xpc_ref[hc_rows + s0:hc_rows + s1, c0:c0 + mc] = _dot(xn, win_ref[:, c0:c0 + mc])
                zb = _dot(xn, win_ref[:, C_WIDTH + c0:C_WIDTH + c0 + mc])
                zb_ref[srows, c0:c0 + mc] = zb
                return zb[0:1, 0:LANES]
            return emit

        def pool_in_piece(c0):
            def emit():
                zp = _dot(xn_ref[srows, :], win_ref[:, 2 * C_WIDTH + c0:2 * C_WIDTH + c0 + mc])
                xpd_ref[hd_rows + s0:hd_rows + s1, c0:c0 + mc] = zp
                return zp[0:1, 0:LANES]
            return emit

        return ([glu_piece(c0) for c0 in range(0, C_WIDTH, mc)]
                + [pool_in_piece(c0) for c0 in range(0, D_WIDTH, mc)])

    gated = {}

    def conv_block(s0, s1, r0):
        def emit(after):
            lo = gated.get(s0, s0)
            hi = min(r0 + rb + SUBLANES, s1)
            for g0 in range(lo, hi, rb):
                grows = slice(hc_rows + g0, hc_rows + min(g0 + rb, hi))
                xpc_ref[grows, :] = xpc_ref[grows, :] * jax.nn.sigmoid(zb_ref[g0:min(g0 + rb, hi), :])
            gated[s0] = hi
            parts = []
            for c0 in range(0, C_WIDTH, LANES):
                cols = slice(c0, c0 + LANES)
                if stride % SUBLANES == 0:
                    parts.append(_conv_taps(xpc_ref, ccw_ref, r0, cols, C_CONV, stride, rb))
                else:
                    parts.append(_conv_long_stride1(xpc_ref, ccw_ref, r0, cols, C_CONV, rb))
            cc = jnp.concatenate(parts, axis=-1) + _ordered_after(ccb, after)
            ycat_ref[r0:r0 + rb, 0:C_WIDTH] = jax.nn.silu(_ln_rows(cc, clg, clb)).astype(_BF16)
        return emit

    def pool_block(r0):
        def emit(after):
            for gi, win in enumerate(POOL_WINDOWS):
                cols = slice(gi * D_GROUP_DIM, (gi + 1) * D_GROUP_DIM)
                if stride == 1:
                    static_pos = None if r0 < POOL_BUF else POOL_BUF
                else:
                    static_pos = start_pos + r0 // stride
                pooled = _pool_block(xpd_ref, r0, gi, win, stride, rb, pos0, static_pos)
                diff = (pooled - xpd_ref[hd_rows + r0:hd_rows + r0 + rb, cols]).astype(_BF16)
                scale = dscale_ref[:, cols]
                if gi == 0:
                    scale = _ordered_after(scale, after)
                yd = _dot(diff, dproj_ref[gi]) * scale
                ycat_ref[r0:r0 + rb, C_WIDTH + gi * D_GROUP_DIM:C_WIDTH + (gi + 1) * D_GROUP_DIM] = yd.astype(_BF16)
        return emit

    def out_proj_pieces(s0, s1):
        srows = slice(s0, s1)

        def piece(c0):
            def emit():
                y = _dot(ycat_ref[srows, :], wout_ref[:, c0:c0 + mc])
                out_ref[srows, c0:c0 + mc] = x_ref[srows, c0:c0 + mc] + y
                return y[0:1, 0:LANES]
            return emit

        return [piece(c0) for c0 in range(0, D_MODEL, mc)]

    _norm_to_bf16(x_ref, g_ref, xn_ref, 0, tiles[0][1])
    for emit in in_proj_pieces(*tiles[0]):
        emit()
    for si, (s0, s1) in enumerate(tiles):
        xpc_ref[hc_rows + s1:hc_rows + s1 + SUBLANES, :] = jnp.zeros((SUBLANES, C_WIDTH), _F32)
        side = []
        if si + 1 < n_sub:
            n0, n1 = tiles[si + 1]
            _norm_to_bf16(x_ref, g_ref, xn_ref, n0, n1 - n0)
            side += in_proj_pieces(n0, n1)
        if si >= 1:
            side += out_proj_pieces(*tiles[si - 1])
        blocks = [conv_block(s0, s1, r0) for r0 in range(s0, s1, rb)] + [pool_block(r0) for r0 in range(s0, s1, rb)]
        after = None
        for emit in blocks:
            emit(after)
            after = side.pop(0)() if side else None
        for emit in side:
            emit()
    for emit in out_proj_pieces(*tiles[-1]):
        emit()
    _emit_history(xpc_ref, newc_ref, hist_c, tm, carry=not has_state)
    _emit_history(xpd_ref, newd_ref, hist_d, tm, carry=not has_state)
    finish_cast()


def _ffn_kernel(*refs, stride, has_state, final_norm):
    refs = list(refs)
    x_ref, g_ref, win_ref, wout_ref, cw_ref = refs[:5]
    del refs[:5]
    state_ref = refs.pop(0) if has_state else None
    gfin_ref = refs.pop(0) if final_norm else None
    out_ref, newf_ref, xn_ref, gp_ref, hist_ref, h_ref = refs
    rb = FFN_ROW_BLOCK if stride == 1 else ROW_BLOCK
    hist = (FFN_CONV - 1) * stride
    h_rows = _round_up(hist, SUBLANES)
    _init_history(hist_ref, state_ref, hist, pl.program_id(1) == 0)
    n_sub, sub, _ = x_ref.shape

    def sub_tile(i, carry):
        xs = x_ref.at[i]
        os = out_ref.at[i]
        _norm_to_bf16(xs, g_ref, xn_ref, 0, sub)
        xn = xn_ref[...]
        for j, c0 in enumerate(range(0, D_FF, FF_CHUNK)):
            cols = slice(c0, c0 + FF_CHUNK)
            slot = slice((j % 2) * FF_CHUNK, (j % 2 + 1) * FF_CHUNK)
            gp_ref[0:h_rows, slot] = hist_ref[:, cols]
            zg = _dot(xn, win_ref[:, c0:c0 + FF_CHUNK])
            zu = _dot(xn, win_ref[:, D_FF + c0:D_FF + c0 + FF_CHUNK])
            for r0 in range(0, sub, rb):
                rows = slice(r0, r0 + rb)
                gp_ref[h_rows + r0:h_rows + r0 + rb, slot] = zg[rows]
                conv = _conv_taps(gp_ref, cw_ref, r0, cols, FFN_CONV, stride, rb, xcols=slot)
                h_ref[rows, cols] = (jax.nn.gelu(conv) * zu[rows]).astype(_BF16)
            hist_ref[:, cols] = gp_ref[sub:sub + h_rows, slot]

        y = xs[...] + _dot(h_ref[...], wout_ref[...])
        if final_norm:
            gfin = gfin_ref[...]
            for r0 in range(0, sub, NORM_BLOCK):
                rows = slice(r0, r0 + NORM_BLOCK)
                os[rows, :] = _rms_rows(y[rows], gfin)
        else:
            os[...] = y
        return carry

    lax.fori_loop(0, n_sub, sub_tile, 0)
    newf_ref[0] = hist_ref[h_rows - hist:h_rows, :]


def _const_spec(shape):
    nd = len(shape)
    return pl.BlockSpec(shape, lambda b, t: (0,) * nd, pipeline_mode=pl.Buffered(1))


def _layer_spec(arr, layer):
    nd = arr.ndim - 1
    return pl.BlockSpec((None,) + arr.shape[1:], lambda b, t: (layer,) + (0,) * nd, pipeline_mode=pl.Buffered(1))


def _smem_spec():
    return pl.BlockSpec(memory_space=pltpu.SMEM)


def _hbm_spec():
    return pl.BlockSpec(memory_space=pl.ANY)


def _tile_spec(tm, nt, width):
    return pl.BlockSpec((tm, width), lambda b, t: (b * nt + t, 0))


def _state_spec(layer, rows, width):
    return pl.BlockSpec((None, 1, rows, width), lambda b, t: (layer, b, 0, 0))


def _new_state_spec(rows, width):
    return pl.BlockSpec((1, rows, width), lambda b, t: (b, 0, 0))


def _params():
    return pltpu.CompilerParams(dimension_semantics=("arbitrary", "arbitrary"), vmem_limit_bytes=VMEM_LIMIT)


def _cast_plumbing(cast_jobs, n_steps):
    in_specs, args, out_specs, out_shape, stage_in, stage_out = [], [], [], [], [], []
    for arr, _ in cast_jobs:
        _, r, c = arr.shape
        rc = r // n_steps
        assert rc * n_steps == r and rc % BF16_ROWS == 0, (arr.shape, n_steps)
        in_specs.append(_hbm_spec())
        args.append(arr)
        out_specs.append(_hbm_spec())
        out_shape.append(jax.ShapeDtypeStruct((r, c), _BF16))
        stage_in.append(pltpu.VMEM((2, rc, c), _F32))
        stage_out.append(pltpu.VMEM((2, rc, c), _BF16))
    scratch = stage_in + stage_out
    if cast_jobs:
        n = len(cast_jobs)
        scratch += [pltpu.SemaphoreType.DMA((n, 2)), pltpu.SemaphoreType.DMA((n, 2))]
    return in_specs, args, out_specs, out_shape, scratch, tuple(layer for _, layer in cast_jobs)


def _even_call(x2, nb, nt, stride, state, layer, win, wout, prm, cast_jobs, name):
    has_state = state is not None
    i = layer // 2
    tm = x2.shape[0] // (nb * nt)
    hist = (B_CONV - 1) * stride
    h_rows = _round_up(hist, SUBLANES)
    c_in, c_args, c_out_specs, c_out_shape, c_scratch, cast_layers = _cast_plumbing(cast_jobs, nb * nt)
    in_specs = [_tile_spec(tm, nt, D_MODEL), _layer_spec(prm['norm_mix_g'], layer), _const_spec(win.shape),
                _const_spec(wout.shape), _layer_spec(prm['a_ln_g'], i), _layer_spec(prm['b_conv_w'], i)]
    args = [x2, prm['norm_mix_g'], win, wout, prm['a_ln_g'], prm['b_conv_w']]
    if has_state:
        in_specs += [_smem_spec(), _smem_spec(), _state_spec(i, hist, B_WIDTH)]
        args += [prm['a_ws_steps'], prm['a_bs_steps'], state]
    else:
        in_specs += [_layer_spec(prm['a_ws'], i), _layer_spec(prm['a_bs_t'], i)]
        args += [prm['a_ws'], prm['a_bs_t']]
    out_shape = [jax.ShapeDtypeStruct(x2.shape, _F32), jax.ShapeDtypeStruct((nb, hist, B_WIDTH), _F32)]
    out_specs = [_tile_spec(tm, nt, D_MODEL), _new_state_spec(hist, B_WIDTH)]
    if has_state:
        out_shape.append(jax.ShapeDtypeStruct((x2.shape[0], A_WIDTH), _F32))
        out_specs.append(_tile_spec(tm, nt, A_WIDTH))
    sub = min(tm, SUB_TILE)
    scratch = [pltpu.VMEM((tm, D_MODEL), _BF16), pltpu.VMEM((h_rows + tm, B_WIDTH), _F32),
               pltpu.VMEM((tm, A_WIDTH + B_WIDTH), _BF16), pltpu.VMEM((tm, A_WIDTH), _BF16),
               pltpu.VMEM((min(tm // sub, 2), sub, win.shape[1]), _F32)]
    return pl.pallas_call(
        functools.partial(_even_kernel, tm=tm, stride=stride, has_state=has_state, gate_layer=i,
                          cast_layers=cast_layers, grid_shape=(nb, nt)),
        grid=(nb, nt), in_specs=in_specs + c_in, out_specs=out_specs + c_out_specs,
        out_shape=out_shape + c_out_shape, scratch_shapes=scratch + c_scratch,
        compiler_params=_params(), name=name)(*args, *c_args)


def _odd_call(x2, nb, nt, stride, state_c, state_d, start_pos, layer, win, wout, prm, cast_jobs, name):
    has_state = state_c is not None
    i = layer // 2
    hist_c = (C_CONV - 1) * stride
    hist_d = POOL_BUF * stride
    hc_rows = _round_up(hist_c, SUBLANES)
    hd_rows = _round_up(hist_d, SUBLANES)
    tm = x2.shape[0] // (nb * nt)
    c_in, c_args, c_out_specs, c_out_shape, c_scratch, cast_layers = _cast_plumbing(cast_jobs, nb * nt)
    names = ['c_conv_w', 'c_conv_b', 'c_ln_g', 'c_ln_b', 'd_proj', 'd_scale']
    in_specs = [_tile_spec(tm, nt, D_MODEL), _layer_spec(prm['norm_mix_g'], layer), _const_spec(win.shape),
                _const_spec(wout.shape)] + [_layer_spec(prm[k], i) for k in names]
    args = [x2, prm['norm_mix_g'], win, wout] + [prm[k] for k in names]
    if has_state:
        in_specs += [_state_spec(i, hist_c, C_WIDTH), _state_spec(i, hist_d, D_WIDTH)]
        args += [state_c, state_d]
    out_shape = [jax.ShapeDtypeStruct(x2.shape, _F32), jax.ShapeDtypeStruct((nb, hist_c, C_WIDTH), _F32),
                 jax.ShapeDtypeStruct((nb, hist_d, D_WIDTH), _F32)]
    out_specs = [_tile_spec(tm, nt, D_MODEL), _new_state_spec(hist_c, C_WIDTH), _new_state_spec(hist_d, D_WIDTH)]
    scratch = [pltpu.VMEM((tm, D_MODEL), _BF16), pltpu.VMEM((hc_rows + tm + SUBLANES, C_WIDTH), _F32),
               pltpu.VMEM((hd_rows + tm, D_WIDTH), _F32), pltpu.VMEM((tm, C_WIDTH + D_WIDTH), _BF16),
               pltpu.VMEM((tm, C_WIDTH), _F32)]
    return pl.pallas_call(
        functools.partial(_odd_kernel, tm=tm, stride=stride, has_state=has_state, start_pos=start_pos,
                          cast_layers=cast_layers, grid_shape=(nb, nt)),
        grid=(nb, nt), in_specs=in_specs + c_in, out_specs=out_specs + c_out_specs,
        out_shape=out_shape + c_out_shape, scratch_shapes=scratch + c_scratch,
        compiler_params=_params(), name=name)(*args, *c_args)


def _ffn_call(x2, nb, nt, stride, state, layer, final_norm, win, wout, prm, name):
    has_state = state is not None
    hist = (FFN_CONV - 1) * stride
    h_rows = _round_up(hist, SUBLANES)
    tm = x2.shape[0] // (nb * nt)
    sub = min(tm, SUB_TILE)
    n_sub = tm // sub
    x3 = x2.reshape(-1, sub, D_MODEL)
    x_spec = pl.BlockSpec((n_sub, sub, D_MODEL), lambda b, t: (b * nt + t, 0, 0))
    in_specs = [x_spec, _layer_spec(prm['norm_ffn_g'], layer), _const_spec(win.shape), _const_spec(wout.shape),
                _layer_spec(prm['ffn_conv_w'], layer)]
    args = [x3, prm['norm_ffn_g'], win, wout, prm['ffn_conv_w']]
    if has_state:
        in_specs.append(_state_spec(layer, hist, D_FF))
        args.append(state)
    if final_norm:
        in_specs.append(_const_spec((1, D_MODEL)))
        args.append(prm['norm_final_g'])
    out_shape = [jax.ShapeDtypeStruct(x3.shape, _F32), jax.ShapeDtypeStruct((nb, hist, D_FF), _F32)]
    out_specs = [x_spec, _new_state_spec(hist, D_FF)]
    scratch = [pltpu.VMEM((sub, D_MODEL), _BF16), pltpu.VMEM((h_rows + sub, 2 * FF_CHUNK), _F32),
               pltpu.VMEM((h_rows, D_FF), _F32), pltpu.VMEM((sub, D_FF), _BF16)]
    y3, newf = pl.pallas_call(
        functools.partial(_ffn_kernel, stride=stride, has_state=has_state, final_norm=final_norm),
        grid=(nb, nt), in_specs=in_specs, out_specs=out_specs, out_shape=out_shape,
        scratch_shapes=scratch, compiler_params=_params(), name=name)(*args)
    return y3.reshape(x2.shape), newf


def _to_tiles(a, ns):
    *lead, n, l, c = a.shape
    k = len(lead)
    a = a.reshape(*lead, n // ns, ns, l, c)
    a = a.transpose(*range(k), k, k + 2, k + 1, k + 3)
    return a.reshape(*lead, n // ns, l * ns, c)


def _from_tiles(a, ns, l):
    *lead, nb, _, c = a.shape
    k = len(lead)
    a = a.reshape(*lead, nb, l, ns, c)
    a = a.transpose(*range(k), k, k + 2, k + 1, k + 3)
    return a.reshape(*lead, nb * ns, l, c)


def _trunk(x2, nb, nt, stride, states, start_pos, prm, wb, raw, tag):
    new_a, new_b, new_c, new_d, new_f = [], [], [], [], []
    depth = prm['ffn_conv_w'].shape[0]
    for layer in range(depth):
        cast_keys, cast_jobs = [], []
        if raw is not None:
            cast_keys.append(('ffn', layer))
            cast_jobs += [(raw['w_ffn_in'], layer), (raw['w_ffn_out'], layer)]
            if layer + 1 < depth:
                nxt = 'even' if (layer + 1) % 2 == 0 else 'odd'
                cast_keys.append(('mix', layer + 1))
                cast_jobs += [(raw['w_in_' + nxt], (layer + 1) // 2), (raw['w_out_' + nxt], (layer + 1) // 2)]
        win, wout = wb[('mix', layer)]
        if layer % 2 == 0:
            st = None if states is None else states['b']
            res = _even_call(x2, nb, nt, stride, st, layer, win, wout, prm, cast_jobs, name=f"{tag}_mix{layer}")
            n_own = 2 if states is None else 3
            x2 = res[0]
            new_b.append(res[1])
            if states is not None:
                new_a.append(res[2])
        else:
            st_c = None if states is None else states['c']
            st_d = None if states is None else states['d']
            res = _odd_call(x2, nb, nt, stride, st_c, st_d, start_pos, layer, win, wout, prm, cast_jobs,
                            name=f"{tag}_mix{layer}")
            n_own = 3
            x2 = res[0]
            new_c.append(res[1])
            new_d.append(res[2])
        for k, key in enumerate(cast_keys):
            wb[key] = (res[n_own + 2 * k], res[n_own + 2 * k + 1])
        st_f = None if states is None else states['f']
        win, wout = wb[('ffn', layer)]
        x2, nf = _ffn_call(x2, nb, nt, stride, st_f, layer, layer == depth - 1, win, wout, prm,
                           name=f"{tag}_ffn{layer}")
        new_f.append(nf)
    return x2, new_a, new_b, new_c, new_d, new_f


def kernel(x_prompt, x_sample, state_b_conv, state_c_conv, state_d_pool, state_ffn_conv, norm_mix_g, norm_ffn_g, norm_final_g, w_in_even, w_out_even, a_ln_g, a_ws, a_bs, b_conv_w, w_in_odd, w_out_odd, c_conv_w, c_conv_b, c_ln_g, c_ln_b, d_proj, d_scale, w_ffn_in, ffn_conv_w, w_ffn_out):
    batch, seq, d = x_prompt.shape
    dec_batch, dec_seq, _ = x_sample.shape
    ns = SAMPLE_TILE // dec_seq

    def rows(p):
        return p.reshape(p.shape[0], 1, p.shape[1])

    prm = dict(norm_mix_g=rows(norm_mix_g), norm_ffn_g=rows(norm_ffn_g), norm_final_g=norm_final_g.reshape(1, -1),
               a_ln_g=rows(a_ln_g), a_ws=a_ws, a_bs_t=jnp.swapaxes(a_bs, 1, 2),
               a_ws_steps=a_ws[:, :, :dec_seq, :dec_seq].reshape(-1), a_bs_steps=a_bs[:, :, :dec_seq].reshape(-1),
               b_conv_w=b_conv_w,
               c_conv_w=jnp.pad(c_conv_w, ((0, 0), (0, _round_up(C_CONV, SUBLANES) - C_CONV), (0, 0))),
               c_conv_b=rows(c_conv_b), c_ln_g=rows(c_ln_g), c_ln_b=rows(c_ln_b), d_proj=d_proj.astype(_BF16),
               d_scale=rows(d_scale), ffn_conv_w=ffn_conv_w)
    raw = dict(w_in_even=w_in_even, w_out_even=w_out_even, w_in_odd=w_in_odd, w_out_odd=w_out_odd,
               w_ffn_in=w_ffn_in, w_ffn_out=w_ffn_out)
    wb = {('mix', 0): (w_in_even[0].astype(_BF16), w_out_even[0].astype(_BF16))}

    yp, _, b_p, c_p, d_p, f_p = _trunk(x_prompt.reshape(batch * seq, d), batch, seq // PROMPT_TILE, 1, None, 0,
                                       prm, wb, raw, "p")

    states = dict(b=_to_tiles(state_b_conv, ns), c=_to_tiles(state_c_conv, ns), d=_to_tiles(state_d_pool, ns),
                  f=_to_tiles(state_ffn_conv, ns))
    xs = _to_tiles(x_sample, ns).reshape(dec_batch * dec_seq, d)
    ys, a_s, b_s, c_s, d_s, f_s = _trunk(xs, dec_batch // ns, 1, ns, states, PAST_LEN, prm, wb, None, "s")

    def untile(parts, l):
        return _from_tiles(jnp.stack(parts), ns, l)

    y_prompt = yp.reshape(batch, seq, d)
    y_sample = _from_tiles(ys.reshape(dec_batch // ns, dec_seq * ns, d), ns, dec_seq)
    new_a = untile([a.reshape(dec_batch // ns, dec_seq * ns, A_WIDTH) for a in a_s], dec_seq)
    return (y_prompt, y_sample, new_a, jnp.stack(b_p), untile(b_s, B_CONV - 1), jnp.stack(c_p),
            untile(c_s, C_CONV - 1), jnp.stack(d_p), untile(d_s, POOL_BUF), jnp.stack(f_p),
            untile(f_s, FFN_CONV - 1))
```

```python
import functools

import jax
import jax.numpy as jnp
from jax import lax
from jax.experimental import pallas as pl
from jax.experimental.pallas import tpu as pltpu

D_MODEL = 1024
CHUNK = 128
A_HEADS = 4
A_WIDTH = 512
B_WIDTH = 512
B_CONV = 3
C_WIDTH = 512
C_CONV = 31
D_WIDTH = 512
POOL_WINDOWS = (2, 4, 8, 16)
D_GROUP_DIM = 128
POOL_BUF = 15
D_FF = 2816
FFN_CONV = 3
PAST_LEN = 16384
EPS = 1e-6

SUBLANES = 8
LANES = 128
BF16_ROWS = 16
MXU_COLS = 256
PROMPT_TILE = 1024
SAMPLE_TILE = 512
SUB_TILE = 512
ODD_SUB_TILE = 512
ROW_BLOCK = 64
SHIFT_ROW_BLOCK = 128
FFN_ROW_BLOCK = 256
NORM_BLOCK = 32
FF_CHUNK = 256
VMEM_LIMIT = 56 * 1024 * 1024

_BF16 = jnp.bfloat16
_F32 = jnp.float32


def _round_up(n, m):
    return (n + m - 1) // m * m


def _dot(a, b):
    return jnp.dot(a, b, preferred_element_type=_F32)


def _rms_rows(x, g):
    y = x * lax.rsqrt(jnp.mean(x * x, axis=-1, keepdims=True) + EPS)
    return y * g


def _ln_rows(x, g, b=None):
    mu = jnp.mean(x, axis=-1, keepdims=True)
    xc = x - mu
    y = xc * lax.rsqrt(jnp.mean(xc * xc, axis=-1, keepdims=True) + EPS) * g
    if b is not None:
        y = y + b
    return y


def _norm_to_bf16(x_ref, g_ref, xn_ref, s0, sub):
    g = g_ref[...]
    for r0 in range(s0, s0 + sub, NORM_BLOCK):
        rows = slice(r0, r0 + NORM_BLOCK)
        xn_ref[rows, :] = _rms_rows(x_ref[rows, :], g).astype(_BF16)


def _init_history(xp_ref, state_ref, hist, first_tile):
    h_rows = _round_up(hist, SUBLANES)
    if state_ref is None:
        @pl.when(first_tile)
        def _():
            xp_ref[0:h_rows, :] = jnp.zeros((h_rows, xp_ref.shape[1]), _F32)
    else:
        xp_ref[h_rows - hist:h_rows, :] = state_ref[0]


def _emit_history(xp_ref, new_ref, hist, tm, carry):
    h_rows = _round_up(hist, SUBLANES)
    tail = xp_ref[h_rows + tm - hist:h_rows + tm, :]
    new_ref[0] = tail
    if carry:
        xp_ref[h_rows - hist:h_rows, :] = tail


def _conv_taps(xp_ref, w_ref, r0, cols, ktaps, stride, rb, xcols=None, w_rows=None):
    xcols = cols if xcols is None else xcols
    hist = (ktaps - 1) * stride
    base = _round_up(hist, SUBLANES) - hist + r0
    acc = None
    for k in range(ktaps):
        w_k = w_ref[k:k + 1, cols] if w_rows is None else w_rows[k]
        term = w_k * xp_ref[base + k * stride:base + k * stride + rb, xcols]
        acc = term if acc is None else acc + term
    return acc


def _conv_long_stride1(xp_ref, w_ref, r0, cols, ktaps, rb):
    hist = ktaps - 1
    base = _round_up(hist, SUBLANES) - hist
    n = rb + SUBLANES
    p = None
    for r in range(SUBLANES - 1, -1, -1):
        z = None
        for k in range(ktaps):
            if (base + k) % SUBLANES != r:
                continue
            off = r0 + (base + k) - r
            term = w_ref[k:k + 1, cols] * xp_ref[off:off + n, cols]
            z = term if z is None else z + term
        if p is not None:
            p = pltpu.roll(p, n - 1, 0)
            p = p if z is None else p + z
        else:
            p = z
    return p[0:rb]


def _ordered_after(row, token):
    if token is None:
        return row
    half = jnp.uint32(jnp.iinfo(jnp.uint32).bits // 2)
    bits = lax.bitcast_convert_type(token, jnp.uint32)
    zero = lax.shift_right_logical(lax.shift_right_logical(bits, half), half)
    head = lax.bitcast_convert_type(lax.bitcast_convert_type(row[:, 0:LANES], jnp.uint32) + zero, _F32)
    if row.shape[1] == LANES:
        return head
    return jnp.concatenate([head, row[:, LANES:]], axis=1)


def _cast_weights_step(step, n_steps, layers, srcs, dsts, stage_in, stage_out, sem_in, sem_out):
    n_jobs = len(layers)

    def in_copy(j, s, slot):
        rc = stage_in[j].shape[1]
        return pltpu.make_async_copy(srcs[j].at[layers[j], pl.ds(s * rc, rc), :], stage_in[j].at[slot],
                                     sem_in.at[j, slot])

    def out_copy(j, s, slot):
        rc = stage_out[j].shape[1]
        return pltpu.make_async_copy(stage_out[j].at[slot], dsts[j].at[pl.ds(s * rc, rc), :], sem_out.at[j, slot])

    slot = lax.rem(step, 2)

    @pl.when(step == 0)
    def _():
        for j in range(n_jobs):
            in_copy(j, 0, 0).start()

    for j in range(n_jobs):
        in_copy(j, step, slot).wait()

    @pl.when(step + 1 < n_steps)
    def _():
        for j in range(n_jobs):
            in_copy(j, step + 1, 1 - slot).start()

    @pl.when(step >= 2)
    def _():
        for j in range(n_jobs):
            out_copy(j, step - 2, slot).wait()

    for j in range(n_jobs):
        src = stage_in[j].at[slot]
        dst = stage_out[j].at[slot]
        for r0 in range(0, src.shape[0], BF16_ROWS):
            dst[r0:r0 + BF16_ROWS, :] = src[r0:r0 + BF16_ROWS, :].astype(_BF16)
        out_copy(j, step, slot).start()

    def finish():
        @pl.when(step == n_steps - 1)
        def _():
            for j in range(n_jobs):
                if n_steps > 1:
                    out_copy(j, step - 1, 1 - slot).wait()
                out_copy(j, step, slot).wait()

    return finish


def _split_refs(refs, n_in, n_out, n_scratch, n_jobs):
    refs = list(refs)
    a = n_in
    b = a + n_jobs
    c = b + n_out
    d = c + n_jobs
    e = d + n_scratch
    return refs[:a], refs[a:b], refs[b:c], refs[c:d], refs[d:e], refs[e:]


def _maybe_cast_weights(cast_layers, grid_shape, srcs, dsts, cast_scratch):
    n_jobs = len(cast_layers)
    if n_jobs == 0:
        return lambda: None
    stage_in = cast_scratch[:n_jobs]
    stage_out = cast_scratch[n_jobs:2 * n_jobs]
    sem_in, sem_out = cast_scratch[2 * n_jobs:]
    nb, nt = grid_shape
    step = pl.program_id(0) * nt + pl.program_id(1)
    return _cast_weights_step(step, nb * nt, cast_layers, srcs, dsts, stage_in, stage_out, sem_in, sem_out)


def _even_kernel(*refs, tm, stride, has_state, gate_layer, cast_layers, grid_shape):
    ins, srcs, outs, dsts, scr, cast_scr = _split_refs(
        refs, 9 if has_state else 8, 3 if has_state else 2, 5, len(cast_layers))
    x_ref, g_ref, win_ref, wout_ref, lng_ref, cw_ref, gw_ref, gb_ref = ins[:8]
    state_ref = ins[8] if has_state else None
    out_ref, newb_ref = outs[:2]
    v_ref = outs[2] if has_state else None
    xn_ref, xp_ref, ycat_ref, vb_ref, z_ref = scr
    finish_cast = _maybe_cast_weights(cast_layers, grid_shape, srcs, dsts, cast_scr)

    rb = SHIFT_ROW_BLOCK if stride == 1 else ROW_BLOCK
    mc = MXU_COLS
    hist = (B_CONV - 1) * stride
    h_rows = _round_up(hist, SUBLANES)
    _init_history(xp_ref, state_ref, hist, pl.program_id(1) == 0)
    lng = lng_ref[...]
    allc = slice(0, B_WIDTH)
    conv_rows = [cw_ref[k:k + 1, :] for k in range(B_CONV)]
    if stride == 1:
        row_i = lax.broadcasted_iota(jnp.int32, (CHUNK, CHUNK), 0)
        col_i = lax.broadcasted_iota(jnp.int32, (CHUNK, CHUNK), 1)
        wts = [jnp.where(col_i <= row_i, gw_ref[h], 0.0).astype(_BF16) for h in range(A_HEADS)]
    sub = min(tm, SUB_TILE)
    n_sub = tm // sub
    u0, v0, bg0, cg0, hh0 = (k * A_WIDTH for k in range(5))


    def in_proj_pieces(si):
        srows = slice(si * sub, (si + 1) * sub)

        def piece(c0):
            def emit():
                z = _dot(xn_ref[srows, :], win_ref[:, c0:c0 + mc])
                z_ref[si % 2, :, c0:c0 + mc] = z
                return z[0:1, 0:LANES]
            return emit

        order = list(range(bg0, hh0 + B_WIDTH, mc)) + list(range(u0, bg0, mc))
        return [piece(c0) for c0 in order]

    def b_block(si, r0):
        def emit(after):
            loc = slice(r0 - si * sub, r0 - si * sub + rb)
            z = z_ref.at[si % 2]
            xp_ref[h_rows + r0:h_rows + r0 + rb, :] = z[loc, cg0:cg0 + B_WIDTH] * z[loc, hh0:hh0 + B_WIDTH]
            w_rows = conv_rows[:-1] + [_ordered_after(conv_rows[-1], after)]
            conv = _conv_taps(xp_ref, cw_ref, r0, allc, B_CONV, stride, rb, w_rows=w_rows)
            ycat_ref[r0:r0 + rb, A_WIDTH:A_WIDTH + B_WIDTH] = (z[loc, bg0:bg0 + B_WIDTH] * conv).astype(_BF16)
        return emit

    def ln_block(si, r0):
        def emit(after):
            gain = _ordered_after(lng, after)
            for q0 in range(r0, r0 + rb, NORM_BLOCK):
                rows = slice(q0, q0 + NORM_BLOCK)
                loc = slice(q0 - si * sub, q0 - si * sub + NORM_BLOCK)
                v = _ln_rows(jax.nn.gelu(z_ref[si % 2, loc, v0:v0 + A_WIDTH]), gain)
                if v_ref is not None:
                    v_ref[rows, :] = v
                else:
                    vb_ref[rows, :] = v.astype(_BF16)
        return emit

    def gate_items(si):
        s0 = si * sub
        z = z_ref.at[si % 2]
        items = []
        if stride == 1:
            for c0 in range(s0, s0 + sub, 2 * CHUNK):
                for h in range(A_HEADS):
                    def emit(after, c0=c0, h=h):
                        ra = slice(c0, c0 + CHUNK)
                        rb2 = slice(c0 + CHUNK, c0 + 2 * CHUNK)
                        cols = slice(h * CHUNK, (h + 1) * CHUNK)
                        vpair = jnp.concatenate([vb_ref[ra, cols], vb_ref[rb2, cols]], axis=1)
                        mixed = _dot(wts[h], vpair)
                        bias = gb_ref[:, h:h + 1]
                        for half, rws in enumerate((ra, rb2)):
                            m = mixed[:, half * CHUNK:(half + 1) * CHUNK] + bias
                            u = z[rws.start - s0:rws.stop - s0, u0 + h * CHUNK:u0 + (h + 1) * CHUNK]
                            ycat_ref[rws, cols] = (jax.nn.gelu(u) * m).astype(_BF16)
                    items.append(emit)
        else:
            steps = tm // stride
            gw_off = gate_layer * A_HEADS * steps * steps
            gb_off = gate_layer * A_HEADS * steps
            for t in range(s0 // stride, (s0 + sub) // stride):
                def emit(after, t=t):
                    for n0 in range(0, stride, rb):
                        rows = slice(t * stride + n0, t * stride + n0 + rb)
                        for h in range(A_HEADS):
                            cols = slice(h * CHUNK, (h + 1) * CHUNK)
                            mixed = None
                            for s in range(t + 1):
                                w = gw_ref[gw_off + (h * steps + t) * steps + s]
                                term = w * v_ref[s * stride + n0:s * stride + n0 + rb, cols]
                                mixed = term if mixed is None else mixed + term
                            mixed = mixed + gb_ref[gb_off + h * steps + t]
                            u = z[rows.start - s0:rows.stop - s0, u0 + h * CHUNK:u0 + (h + 1) * CHUNK]
                            ycat_ref[rows, cols] = (jax.nn.gelu(u) * mixed).astype(_BF16)
                items.append(emit)
        return items

    def out_proj_pieces(si):
        srows = slice(si * sub, (si + 1) * sub)

        def piece(c0):
            def emit():
                y = _dot(ycat_ref[srows, :], wout_ref[:, c0:c0 + mc])
                out_ref[srows, c0:c0 + mc] = x_ref[srows, c0:c0 + mc] + y
                return y[0:1, 0:LANES]
            return emit

        return [piece(c0) for c0 in range(0, D_MODEL, mc)]

    _norm_to_bf16(x_ref, g_ref, xn_ref, 0, sub)
    for emit in in_proj_pieces(0):
        emit()
    for si in range(n_sub):
        s0 = si * sub
        side = []
        if si + 1 < n_sub:
            _norm_to_bf16(x_ref, g_ref, xn_ref, s0 + sub, sub)
            side += in_proj_pieces(si + 1)
        if si >= 1:
            side += out_proj_pieces(si - 1)
        blocks = ([b_block(si, r0) for r0 in range(s0, s0 + sub, rb)]
                  + [ln_block(si, r0) for r0 in range(s0, s0 + sub, rb)] + gate_items(si))
        after = None
        for emit in blocks:
            emit(after)
            after = side.pop(0)() if side else None
        for emit in side:
            emit()
    for emit in out_proj_pieces(n_sub - 1):
        emit()
    _emit_history(xp_ref, newb_ref, hist, tm, carry=not has_state)
    finish_cast()


def _pool_block(xp_ref, r0, gi, win, stride, rb, pos0, row_pos_static):
    cols = slice(gi * D_GROUP_DIM, (gi + 1) * D_GROUP_DIM)
    hist = POOL_BUF * stride
    h_rows = _round_up(hist, SUBLANES)
    if stride % SUBLANES == 0:
        s = None
        for j in range(win):
            off = h_rows + r0 - j * stride
            term = xp_ref[off:off + rb, cols]
            s = term if s is None else s + term
    else:
        halo = 2 * SUBLANES
        ext = xp_ref[h_rows + r0 - halo:h_rows + r0 + rb, cols]
        shift = 1
        while shift < win:
            ext = ext + pltpu.roll(ext, shift, 0)
            shift *= 2
        s = ext[halo:]
    if row_pos_static is not None:
        return s / float(min(row_pos_static + 1, win))
    pos = pos0 + r0 + lax.broadcasted_iota(jnp.int32, (rb, D_GROUP_DIM), 0)
    cnt = jnp.minimum(pos + 1, win).astype(_F32)
    return s / cnt


def _odd_kernel(*refs, tm, stride, has_state, start_pos, cast_layers, grid_shape):
    ins, srcs, outs, dsts, scr, cast_scr = _split_refs(refs, 12 if has_state else 10, 3, 5, len(cast_layers))
    x_ref, g_ref, win_ref, wout_ref, ccw_ref, ccb_ref, clg_ref, clb_ref, dproj_ref, dscale_ref = ins[:10]
    statec_ref, stated_ref = (ins[10], ins[11]) if has_state else (None, None)
    out_ref, newc_ref, newd_ref = outs
    xn_ref, xpc_ref, xpd_ref, ycat_ref, zb_ref = scr
    finish_cast = _maybe_cast_weights(cast_layers, grid_shape, srcs, dsts, cast_scr)

    rb = SHIFT_ROW_BLOCK if stride == 1 else ROW_BLOCK
    hist_c = (C_CONV - 1) * stride
    hc_rows = _round_up(hist_c, SUBLANES)
    hist_d = POOL_BUF * stride
    hd_rows = _round_up(hist_d, SUBLANES)
    first = pl.program_id(1) == 0
    _init_history(xpc_ref, statec_ref, hist_c, first)
    _init_history(xpd_ref, stated_ref, hist_d, first)
    ccb = ccb_ref[...]
    clg = clg_ref[...]
    clb = clb_ref[...]
    pos0 = start_pos + pl.program_id(1) * tm
    mc = MXU_COLS
    cuts = list(range(0, tm + 1, min(tm, ODD_SUB_TILE)))
    tiles = list(zip(cuts[:-1], cuts[1:]))
    n_sub = len(tiles)


    def in_proj_pieces(s0, s1):
        srows = slice(s0, s1)

        def glu_piece(c0):
            def emit():
                xn = xn_ref[srows, :]
                xpc_ref[hc_rows + s0:hc_rows + s1, c0:c0 + mc] = _dot(xn, win_ref[:, c0:c0 + mc])
                zb = _dot(xn, win_ref[:, C_WIDTH + c0:C_WIDTH + c0 + mc])
                zb_ref[srows, c0:c0 + mc] = zb
                return zb[0:1, 0:LANES]
            return emit

        def pool_in_piece(c0):
            def emit():
                zp = _dot(xn_ref[srows, :], win_ref[:, 2 * C_WIDTH + c0:2 * C_WIDTH + c0 + mc])
                xpd_ref[hd_rows + s0:hd_rows + s1, c0:c0 + mc] = zp
                return zp[0:1, 0:LANES]
            return emit

        return ([glu_piece(c0) for c0 in range(0, C_WIDTH, mc)]
                + [pool_in_piece(c0) for c0 in range(0, D_WIDTH, mc)])

    gated = {}

    def conv_block(s0, s1, r0):
        def emit(after):
            lo = gated.get(s0, s0)
            hi = min(r0 + rb + SUBLANES, s1)
            for g0 in range(lo, hi, rb):
                grows = slice(hc_rows + g0, hc_rows + min(g0 + rb, hi))
                xpc_ref[grows, :] = xpc_ref[grows, :] * jax.nn.sigmoid(zb_ref[g0:min(g0 + rb, hi), :])
            gated[s0] = hi
            parts = []
            for c0 in range(0, C_WIDTH, LANES):
                cols = slice(c0, c0 + LANES)
                if stride % SUBLANES == 0:
                    parts.append(_conv_taps(xpc_ref, ccw_ref, r0, cols, C_CONV, stride, rb))
                else:
                    parts.append(_conv_long_stride1(xpc_ref, ccw_ref, r0, cols, C_CONV, rb))
            cc = jnp.concatenate(parts, axis=-1) + _ordered_after(ccb, after)
            ycat_ref[r0:r0 + rb, 0:C_WIDTH] = jax.nn.silu(_ln_rows(cc, clg, clb)).astype(_BF16)
        return emit

    def pool_block(r0):
        def emit(after):
            for gi, win in enumerate(POOL_WINDOWS):
                cols = slice(gi * D_GROUP_DIM, (gi + 1) * D_GROUP_DIM)
                if stride == 1:
                    static_pos = None if r0 < POOL_BUF else POOL_BUF
                else:
                    static_pos = start_pos + r0 // stride
                pooled = _pool_block(xpd_ref, r0, gi, win, stride, rb, pos0, static_pos)
                diff = (pooled - xpd_ref[hd_rows + r0:hd_rows + r0 + rb, cols]).astype(_BF16)
                scale = dscale_ref[:, cols]
                if gi == 0:
                    scale = _ordered_after(scale, after)
                yd = _dot(diff, dproj_ref[gi]) * scale
                ycat_ref[r0:r0 + rb, C_WIDTH + gi * D_GROUP_DIM:C_WIDTH + (gi + 1) * D_GROUP_DIM] = yd.astype(_BF16)
        return emit

    def out_proj_pieces(s0, s1):
        srows = slice(s0, s1)

        def piece(c0):
            def emit():
                y = _dot(ycat_ref[srows, :], wout_ref[:, c0:c0 + mc])
                out_ref[srows, c0:c0 + mc] = x_ref[srows, c0:c0 + mc] + y
                return y[0:1, 0:LANES]
            return emit

        return [piece(c0) for c0 in range(0, D_MODEL, mc)]

    _norm_to_bf16(x_ref, g_ref, xn_ref, 0, tiles[0][1])
    for emit in in_proj_pieces(*tiles[0]):
        emit()
    for si, (s0, s1) in enumerate(tiles):
        xpc_ref[hc_rows + s1:hc_rows + s1 + SUBLANES, :] = jnp.zeros((SUBLANES, C_WIDTH), _F32)
        side = []
        if si + 1 < n_sub:
            n0, n1 = tiles[si + 1]
            _norm_to_bf16(x_ref, g_ref, xn_ref, n0, n1 - n0)
            side += in_proj_pieces(n0, n1)
        if si >= 1:
            side += out_proj_pieces(*tiles[si - 1])
        blocks = [conv_block(s0, s1, r0) for r0 in range(s0, s1, rb)] + [pool_block(r0) for r0 in range(s0, s1, rb)]
        after = None
        for emit in blocks:
            emit(after)
            after = side.pop(0)() if side else None
        for emit in side:
            emit()
    for emit in out_proj_pieces(*tiles[-1]):
        emit()
    _emit_history(xpc_ref, newc_ref, hist_c, tm, carry=not has_state)
    _emit_history(xpd_ref, newd_ref, hist_d, tm, carry=not has_state)
    finish_cast()


def _ffn_kernel(*refs, stride, has_state, final_norm):
    refs = list(refs)
    x_ref, g_ref, win_ref, wout_ref, cw_ref = refs[:5]
    del refs[:5]
    state_ref = refs.pop(0) if has_state else None
    gfin_ref = refs.pop(0) if final_norm else None
    out_ref, newf_ref, xn_ref, gp_ref, hist_ref, h_ref = refs
    rb = FFN_ROW_BLOCK
    hist = (FFN_CONV - 1) * stride
    h_rows = _round_up(hist, SUBLANES)
    _init_history(hist_ref, state_ref, hist, pl.program_id(1) == 0)
    n_sub, sub, _ = x_ref.shape

    def sub_tile(i, carry):
        xs = x_ref.at[i]
        os = out_ref.at[i]
        _norm_to_bf16(xs, g_ref, xn_ref, 0, sub)
        xn = xn_ref[...]
        for j, c0 in enumerate(range(0, D_FF, FF_CHUNK)):
            cols = slice(c0, c0 + FF_CHUNK)
            slot = slice((j % 2) * FF_CHUNK, (j % 2 + 1) * FF_CHUNK)
            gp_ref[0:h_rows, slot] = hist_ref[:, cols]
            zg = _dot(xn, win_ref[:, c0:c0 + FF_CHUNK])
            zu = _dot(xn, win_ref[:, D_FF + c0:D_FF + c0 + FF_CHUNK])
            for r0 in range(0, sub, rb):
                rows = slice(r0, r0 + rb)
                gp_ref[h_rows + r0:h_rows + r0 + rb, slot] = zg[rows]
                conv = _conv_taps(gp_ref, cw_ref, r0, cols, FFN_CONV, stride, rb, xcols=slot)
                h_ref[rows, cols] = (jax.nn.gelu(conv) * zu[rows]).astype(_BF16)
            hist_ref[:, cols] = gp_ref[sub:sub + h_rows, slot]

        y = xs[...] + _dot(h_ref[...], wout_ref[...])
        if final_norm:
            gfin = gfin_ref[...]
            for r0 in range(0, sub, NORM_BLOCK):
                rows = slice(r0, r0 + NORM_BLOCK)
                os[rows, :] = _rms_rows(y[rows], gfin)
        else:
            os[...] = y
        return carry

    lax.fori_loop(0, n_sub, sub_tile, 0)
    newf_ref[0] = hist_ref[h_rows - hist:h_rows, :]


def _const_spec(shape):
    nd = len(shape)
    return pl.BlockSpec(shape, lambda b, t: (0,) * nd, pipeline_mode=pl.Buffered(1))


def _layer_spec(arr, layer):
    nd = arr.ndim - 1
    return pl.BlockSpec((None,) + arr.shape[1:], lambda b, t: (layer,) + (0,) * nd, pipeline_mode=pl.Buffered(1))


def _smem_spec():
    return pl.BlockSpec(memory_space=pltpu.SMEM)


def _hbm_spec():
    return pl.BlockSpec(memory_space=pl.ANY)


def _tile_spec(tm, nt, width):
    return pl.BlockSpec((tm, width), lambda b, t: (b * nt + t, 0))


def _state_spec(layer, rows, width):
    return pl.BlockSpec((None, 1, rows, width), lambda b, t: (layer, b, 0, 0))


def _new_state_spec(rows, width):
    return pl.BlockSpec((1, rows, width), lambda b, t: (b, 0, 0))


def _params():
    return pltpu.CompilerParams(dimension_semantics=("arbitrary", "arbitrary"), vmem_limit_bytes=VMEM_LIMIT)


def _cast_plumbing(cast_jobs, n_steps):
    in_specs, args, out_specs, out_shape, stage_in, stage_out = [], [], [], [], [], []
    for arr, _ in cast_jobs:
        _, r, c = arr.shape
        rc = r // n_steps
        assert rc * n_steps == r and rc % BF16_ROWS == 0, (arr.shape, n_steps)
        in_specs.append(_hbm_spec())
        args.append(arr)
        out_specs.append(_hbm_spec())
        out_shape.append(jax.ShapeDtypeStruct((r, c), _BF16))
        stage_in.append(pltpu.VMEM((2, rc, c), _F32))
        stage_out.append(pltpu.VMEM((2, rc, c), _BF16))
    scratch = stage_in + stage_out
    if cast_jobs:
        n = len(cast_jobs)
        scratch += [pltpu.SemaphoreType.DMA((n, 2)), pltpu.SemaphoreType.DMA((n, 2))]
    return in_specs, args, out_specs, out_shape, scratch, tuple(layer for _, layer in cast_jobs)


def _even_call(x2, nb, nt, stride, state, layer, win, wout, prm, cast_jobs, name):
    has_state = state is not None
    i = layer // 2
    tm = x2.shape[0] // (nb * nt)
    hist = (B_CONV - 1) * stride
    h_rows = _round_up(hist, SUBLANES)
    c_in, c_args, c_out_specs, c_out_shape, c_scratch, cast_layers = _cast_plumbing(cast_jobs, nb * nt)
    in_specs = [_tile_spec(tm, nt, D_MODEL), _layer_spec(prm['norm_mix_g'], layer), _const_spec(win.shape),
                _const_spec(wout.shape), _layer_spec(prm['a_ln_g'], i), _layer_spec(prm['b_conv_w'], i)]
    args = [x2, prm['norm_mix_g'], win, wout, prm['a_ln_g'], prm['b_conv_w']]
    if has_state:
        in_specs += [_smem_spec(), _smem_spec(), _state_spec(i, hist, B_WIDTH)]
        args += [prm['a_ws_steps'], prm['a_bs_steps'], state]
    else:
        in_specs += [_layer_spec(prm['a_ws'], i), _layer_spec(prm['a_bs_t'], i)]
        args += [prm['a_ws'], prm['a_bs_t']]
    out_shape = [jax.ShapeDtypeStruct(x2.shape, _F32), jax.ShapeDtypeStruct((nb, hist, B_WIDTH), _F32)]
    out_specs = [_tile_spec(tm, nt, D_MODEL), _new_state_spec(hist, B_WIDTH)]
    if has_state:
        out_shape.append(jax.ShapeDtypeStruct((x2.shape[0], A_WIDTH), _F32))
        out_specs.append(_tile_spec(tm, nt, A_WIDTH))
    sub = min(tm, SUB_TILE)
    scratch = [pltpu.VMEM((tm, D_MODEL), _BF16), pltpu.VMEM((h_rows + tm, B_WIDTH), _F32),
               pltpu.VMEM((tm, A_WIDTH + B_WIDTH), _BF16), pltpu.VMEM((tm, A_WIDTH), _BF16),
               pltpu.VMEM((min(tm // sub, 2), sub, win.shape[1]), _F32)]
    return pl.pallas_call(
        functools.partial(_even_kernel, tm=tm, stride=stride, has_state=has_state, gate_layer=i,
                          cast_layers=cast_layers, grid_shape=(nb, nt)),
        grid=(nb, nt), in_specs=in_specs + c_in, out_specs=out_specs + c_out_specs,
        out_shape=out_shape + c_out_shape, scratch_shapes=scratch + c_scratch,
        compiler_params=_params(), name=name)(*args, *c_args)


def _odd_call(x2, nb, nt, stride, state_c, state_d, start_pos, layer, win, wout, prm, cast_jobs, name):
    has_state = state_c is not None
    i = layer // 2
    hist_c = (C_CONV - 1) * stride
    hist_d = POOL_BUF * stride
    hc_rows = _round_up(hist_c, SUBLANES)
    hd_rows = _round_up(hist_d, SUBLANES)
    tm = x2.shape[0] // (nb * nt)
    c_in, c_args, c_out_specs, c_out_shape, c_scratch, cast_layers = _cast_plumbing(cast_jobs, nb * nt)
    names = ['c_conv_w', 'c_conv_b', 'c_ln_g', 'c_ln_b', 'd_proj', 'd_scale']
    in_specs = [_tile_spec(tm, nt, D_MODEL), _layer_spec(prm['norm_mix_g'], layer), _const_spec(win.shape),
                _const_spec(wout.shape)] + [_layer_spec(prm[k], i) for k in names]
    args = [x2, prm['norm_mix_g'], win, wout] + [prm[k] for k in names]
    if has_state:
        in_specs += [_state_spec(i, hist_c, C_WIDTH), _state_spec(i, hist_d, D_WIDTH)]
        args += [state_c, state_d]
    out_shape = [jax.ShapeDtypeStruct(x2.shape, _F32), jax.ShapeDtypeStruct((nb, hist_c, C_WIDTH), _F32),
                 jax.ShapeDtypeStruct((nb, hist_d, D_WIDTH), _F32)]
    out_specs = [_tile_spec(tm, nt, D_MODEL), _new_state_spec(hist_c, C_WIDTH), _new_state_spec(hist_d, D_WIDTH)]
    scratch = [pltpu.VMEM((tm, D_MODEL), _BF16), pltpu.VMEM((hc_rows + tm + SUBLANES, C_WIDTH), _F32),
               pltpu.VMEM((hd_rows + tm, D_WIDTH), _F32), pltpu.VMEM((tm, C_WIDTH + D_WIDTH), _BF16),
               pltpu.VMEM((tm, C_WIDTH), _F32)]
    return pl.pallas_call(
        functools.partial(_odd_kernel, tm=tm, stride=stride, has_state=has_state, start_pos=start_pos,
                          cast_layers=cast_layers, grid_shape=(nb, nt)),
        grid=(nb, nt), in_specs=in_specs + c_in, out_specs=out_specs + c_out_specs,
        out_shape=out_shape + c_out_shape, scratch_shapes=scratch + c_scratch,
        compiler_params=_params(), name=name)(*args, *c_args)


def _ffn_call(x2, nb, nt, stride, state, layer, final_norm, win, wout, prm, name):
    has_state = state is not None
    hist = (FFN_CONV - 1) * stride
    h_rows = _round_up(hist, SUBLANES)
    tm = x2.shape[0] // (nb * nt)
    sub = min(tm, SUB_TILE)
    n_sub = tm // sub
    x3 = x2.reshape(-1, sub, D_MODEL)
    x_spec = pl.BlockSpec((n_sub, sub, D_MODEL), lambda b, t: (b * nt + t, 0, 0))
    in_specs = [x_spec, _layer_spec(prm['norm_ffn_g'], layer), _const_spec(win.shape), _const_spec(wout.shape),
                _layer_spec(prm['ffn_conv_w'], layer)]
    args = [x3, prm['norm_ffn_g'], win, wout, prm['ffn_conv_w']]
    if has_state:
        in_specs.append(_state_spec(layer, hist, D_FF))
        args.append(state)
    if final_norm:
        in_specs.append(_const_spec((1, D_MODEL)))
        args.append(prm['norm_final_g'])
    out_shape = [jax.ShapeDtypeStruct(x3.shape, _F32), jax.ShapeDtypeStruct((nb, hist, D_FF), _F32)]
    out_specs = [x_spec, _new_state_spec(hist, D_FF)]
    scratch = [pltpu.VMEM((sub, D_MODEL), _BF16), pltpu.VMEM((h_rows + sub, 2 * FF_CHUNK), _F32),
               pltpu.VMEM((h_rows, D_FF), _F32), pltpu.VMEM((sub, D_FF), _BF16)]
    y3, newf = pl.pallas_call(
        functools.partial(_ffn_kernel, stride=stride, has_state=has_state, final_norm=final_norm),
        grid=(nb, nt), in_specs=in_specs, out_specs=out_specs, out_shape=out_shape,
        scratch_shapes=scratch, compiler_params=_params(), name=name)(*args)
    return y3.reshape(x2.shape), newf


def _to_tiles(a, ns):
    *lead, n, l, c = a.shape
    k = len(lead)
    a = a.reshape(*lead, n // ns, ns, l, c)
    a = a.transpose(*range(k), k, k + 2, k + 1, k + 3)
    return a.reshape(*lead, n // ns, l * ns, c)


def _from_tiles(a, ns, l):
    *lead, nb, _, c = a.shape
    k = len(lead)
    a = a.reshape(*lead, nb, l, ns, c)
    a = a.transpose(*range(k), k, k + 2, k + 1, k + 3)
    return a.reshape(*lead, nb * ns, l, c)


def _trunk(x2, nb, nt, stride, states, start_pos, prm, wb, raw, tag):
    new_a, new_b, new_c, new_d, new_f = [], [], [], [], []
    depth = prm['ffn_conv_w'].shape[0]
    for layer in range(depth):
        cast_keys, cast_jobs = [], []
        if raw is not None:
            cast_keys.append(('ffn', layer))
            cast_jobs += [(raw['w_ffn_in'], layer), (raw['w_ffn_out'], layer)]
            if layer + 1 < depth:
                nxt = 'even' if (layer + 1) % 2 == 0 else 'odd'
                cast_keys.append(('mix', layer + 1))
                cast_jobs += [(raw['w_in_' + nxt], (layer + 1) // 2), (raw['w_out_' + nxt], (layer + 1) // 2)]
        win, wout = wb[('mix', layer)]
        if layer % 2 == 0:
            st = None if states is None else states['b']
            res = _even_call(x2, nb, nt, stride, st, layer, win, wout, prm, cast_jobs, name=f"{tag}_mix{layer}")
            n_own = 2 if states is None else 3
            x2 = res[0]
            new_b.append(res[1])
            if states is not None:
                new_a.append(res[2])
        else:
            st_c = None if states is None else states['c']
            st_d = None if states is None else states['d']
            res = _odd_call(x2, nb, nt, stride, st_c, st_d, start_pos, layer, win, wout, prm, cast_jobs,
                            name=f"{tag}_mix{layer}")
            n_own = 3
            x2 = res[0]
            new_c.append(res[1])
            new_d.append(res[2])
        for k, key in enumerate(cast_keys):
            wb[key] = (res[n_own + 2 * k], res[n_own + 2 * k + 1])
        st_f = None if states is None else states['f']
        win, wout = wb[('ffn', layer)]
        x2, nf = _ffn_call(x2, nb, nt, stride, st_f, layer, layer == depth - 1, win, wout, prm,
                           name=f"{tag}_ffn{layer}")
        new_f.append(nf)
    return x2, new_a, new_b, new_c, new_d, new_f


def kernel(x_prompt, x_sample, state_b_conv, state_c_conv, state_d_pool, state_ffn_conv, norm_mix_g, norm_ffn_g, norm_final_g, w_in_even, w_out_even, a_ln_g, a_ws, a_bs, b_conv_w, w_in_odd, w_out_odd, c_conv_w, c_conv_b, c_ln_g, c_ln_b, d_proj, d_scale, w_ffn_in, ffn_conv_w, w_ffn_out):
    batch, seq, d = x_prompt.shape
    dec_batch, dec_seq, _ = x_sample.shape
    ns = SAMPLE_TILE // dec_seq

    def rows(p):
        return p.reshape(p.shape[0], 1, p.shape[1])

    prm = dict(norm_mix_g=rows(norm_mix_g), norm_ffn_g=rows(norm_ffn_g), norm_final_g=norm_final_g.reshape(1, -1),
               a_ln_g=rows(a_ln_g), a_ws=a_ws, a_bs_t=jnp.swapaxes(a_bs, 1, 2),
               a_ws_steps=a_ws[:, :, :dec_seq, :dec_seq].reshape(-1), a_bs_steps=a_bs[:, :, :dec_seq].reshape(-1),
               b_conv_w=b_conv_w,
               c_conv_w=jnp.pad(c_conv_w, ((0, 0), (0, _round_up(C_CONV, SUBLANES) - C_CONV), (0, 0))),
               c_conv_b=rows(c_conv_b), c_ln_g=rows(c_ln_g), c_ln_b=rows(c_ln_b), d_proj=d_proj.astype(_BF16),
               d_scale=rows(d_scale), ffn_conv_w=ffn_conv_w)
    raw = dict(w_in_even=w_in_even, w_out_even=w_out_even, w_in_odd=w_in_odd, w_out_odd=w_out_odd,
               w_ffn_in=w_ffn_in, w_ffn_out=w_ffn_out)
    wb = {('mix', 0): (w_in_even[0].astype(_BF16), w_out_even[0].astype(_BF16))}

    yp, _, b_p, c_p, d_p, f_p = _trunk(x_prompt.reshape(batch * seq, d), batch, seq // PROMPT_TILE, 1, None, 0,
                                       prm, wb, raw, "p")

    states = dict(b=_to_tiles(state_b_conv, ns), c=_to_tiles(state_c_conv, ns), d=_to_tiles(state_d_pool, ns),
                  f=_to_tiles(state_ffn_conv, ns))
    xs = _to_tiles(x_sample, ns).reshape(dec_batch * dec_seq, d)
    ys, a_s, b_s, c_s, d_s, f_s = _trunk(xs, dec_batch // ns, 1, ns, states, PAST_LEN, prm, wb, None, "s")

    def untile(parts, l):
        return _from_tiles(jnp.stack(parts), ns, l)

    y_prompt = yp.reshape(batch, seq, d)
    y_sample = _from_tiles(ys.reshape(dec_batch // ns, dec_seq * ns, d), ns, dec_seq)
    new_a = untile([a.reshape(dec_batch // ns, dec_seq * ns, A_WIDTH) for a in a_s], dec_seq)
    return (y_prompt, y_sample, new_a, jnp.stack(b_p), untile(b_s, B_CONV - 1), jnp.stack(c_p),
            untile(c_s, C_CONV - 1), jnp.stack(d_p), untile(d_s, POOL_BUF), jnp.stack(f_p),
            untile(f_s, FFN_CONV - 1))
```

```python
import functools

import jax
import jax.numpy as jnp
from jax import lax
from jax.experimental import pallas as pl
from jax.experimental.pallas import tpu as pltpu

D_MODEL = 1024
CHUNK = 128
A_HEADS = 4
A_WIDTH = 512
B_WIDTH = 512
B_CONV = 3
C_WIDTH = 512
C_CONV = 31
D_WIDTH = 512
POOL_WINDOWS = (2, 4, 8, 16)
D_GROUP_DIM = 128
POOL_BUF = 15
D_FF = 2816
FFN_CONV = 3
PAST_LEN = 16384
EPS = 1e-6

SUBLANES = 8
LANES = 128
BF16_ROWS = 16
MXU_COLS = 256
PROMPT_TILE = 1024
SAMPLE_TILE = 512
SUB_TILE = 512
ODD_SUB_TILE = 512
ROW_BLOCK = 64
SHIFT_ROW_BLOCK = 128
FFN_ROW_BLOCK = 256
NORM_BLOCK = 32
FF_CHUNK = 256
VMEM_LIMIT = 56 * 1024 * 1024

_BF16 = jnp.bfloat16
_F32 = jnp.float32


def _round_up(n, m):
    return (n + m - 1) // m * m


def _dot(a, b):
    return jnp.dot(a, b, preferred_element_type=_F32)


def _rms_rows(x, g):
    y = x * lax.rsqrt(jnp.mean(x * x, axis=-1, keepdims=True) + EPS)
    return y * g


def _ln_rows(x, g, b=None):
    mu = jnp.mean(x, axis=-1, keepdims=True)
    xc = x - mu
    y = xc * lax.rsqrt(jnp.mean(xc * xc, axis=-1, keepdims=True) + EPS) * g
    if b is not None:
        y = y + b
    return y


def _norm_to_bf16(x_ref, g_ref, xn_ref, s0, sub):
    g = g_ref[...]
    for r0 in range(s0, s0 + sub, NORM_BLOCK):
        rows = slice(r0, r0 + NORM_BLOCK)
        xn_ref[rows, :] = _rms_rows(x_ref[rows, :], g).astype(_BF16)


def _init_history(xp_ref, state_ref, hist, first_tile):
    h_rows = _round_up(hist, SUBLANES)
    if state_ref is None:
        @pl.when(first_tile)
        def _():
            xp_ref[0:h_rows, :] = jnp.zeros((h_rows, xp_ref.shape[1]), _F32)
    else:
        xp_ref[h_rows - hist:h_rows, :] = state_ref[0]


def _emit_history(xp_ref, new_ref, hist, tm, carry):
    h_rows = _round_up(hist, SUBLANES)
    tail = xp_ref[h_rows + tm - hist:h_rows + tm, :]
    new_ref[0] = tail
    if carry:
        xp_ref[h_rows - hist:h_rows, :] = tail


def _conv_taps(xp_ref, w_ref, r0, cols, ktaps, stride, rb, xcols=None, w_rows=None):
    xcols = cols if xcols is None else xcols
    hist = (ktaps - 1) * stride
    base = _round_up(hist, SUBLANES) - hist + r0
    acc = None
    for k in range(ktaps):
        w_k = w_ref[k:k + 1, cols] if w_rows is None else w_rows[k]
        term = w_k * xp_ref[base + k * stride:base + k * stride + rb, xcols]
        acc = term if acc is None else acc + term
    return acc


def _conv_long_stride1(xp_ref, w_ref, r0, cols, ktaps, rb):
    hist = ktaps - 1
    base = _round_up(hist, SUBLANES) - hist
    n = rb + SUBLANES
    p = None
    for r in range(SUBLANES - 1, -1, -1):
        z = None
        for k in range(ktaps):
            if (base + k) % SUBLANES != r:
                continue
            off = r0 + (base + k) - r
            term = w_ref[k:k + 1, cols] * xp_ref[off:off + n, cols]
            z = term if z is None else z + term
        if p is not None:
            p = pltpu.roll(p, n - 1, 0)
            p = p if z is None else p + z
        else:
            p = z
    return p[0:rb]


def _ordered_after(row, token):
    if token is None:
        return row
    half = jnp.uint32(jnp.iinfo(jnp.uint32).bits // 2)
    bits = lax.bitcast_convert_type(token, jnp.uint32)
    zero = lax.shift_right_logical(lax.shift_right_logical(bits, half), half)
    head = lax.bitcast_convert_type(lax.bitcast_convert_type(row[:, 0:LANES], jnp.uint32) + zero, _F32)
    if row.shape[1] == LANES:
        return head
    return jnp.concatenate([head, row[:, LANES:]], axis=1)


def _cast_weights_step(step, n_steps, layers, srcs, dsts, stage_in, stage_out, sem_in, sem_out):
    n_jobs = len(layers)

    def in_copy(j, s, slot):
        rc = stage_in[j].shape[1]
        return pltpu.make_async_copy(srcs[j].at[layers[j], pl.ds(s * rc, rc), :], stage_in[j].at[slot],
                                     sem_in.at[j, slot])

    def out_copy(j, s, slot):
        rc = stage_out[j].shape[1]
        return pltpu.make_async_copy(stage_out[j].at[slot], dsts[j].at[pl.ds(s * rc, rc), :], sem_out.at[j, slot])

    slot = lax.rem(step, 2)

    @pl.when(step == 0)
    def _():
        for j in range(n_jobs):
            in_copy(j, 0, 0).start()

    for j in range(n_jobs):
        in_copy(j, step, slot).wait()

    @pl.when(step + 1 < n_steps)
    def _():
        for j in range(n_jobs):
            in_copy(j, step + 1, 1 - slot).start()

    @pl.when(step >= 2)
    def _():
        for j in range(n_jobs):
            out_copy(j, step - 2, slot).wait()

    for j in range(n_jobs):
        src = stage_in[j].at[slot]
        dst = stage_out[j].at[slot]
        for r0 in range(0, src.shape[0], BF16_ROWS):
            dst[r0:r0 + BF16_ROWS, :] = src[r0:r0 + BF16_ROWS, :].astype(_BF16)
        out_copy(j, step, slot).start()

    def finish():
        @pl.when(step == n_steps - 1)
        def _():
            for j in range(n_jobs):
                if n_steps > 1:
                    out_copy(j, step - 1, 1 - slot).wait()
                out_copy(j, step, slot).wait()

    return finish


def _split_refs(refs, n_in, n_out, n_scratch, n_jobs):
    refs = list(refs)
    a = n_in
    b = a + n_jobs
    c = b + n_out
    d = c + n_jobs
    e = d + n_scratch
    return refs[:a], refs[a:b], refs[b:c], refs[c:d], refs[d:e], refs[e:]


def _maybe_cast_weights(cast_layers, grid_shape, srcs, dsts, cast_scratch):
    n_jobs = len(cast_layers)
    if n_jobs == 0:
        return lambda: None
    stage_in = cast_scratch[:n_jobs]
    stage_out = cast_scratch[n_jobs:2 * n_jobs]
    sem_in, sem_out = cast_scratch[2 * n_jobs:]
    nb, nt = grid_shape
    step = pl.program_id(0) * nt + pl.program_id(1)
    return _cast_weights_step(step, nb * nt, cast_layers, srcs, dsts, stage_in, stage_out, sem_in, sem_out)


def _even_kernel(*refs, tm, stride, has_state, gate_layer, cast_layers, grid_shape):
    ins, srcs, outs, dsts, scr, cast_scr = _split_refs(
        refs, 9 if has_state else 8, 3 if has_state else 2, 5, len(cast_layers))
    x_ref, g_ref, win_ref, wout_ref, lng_ref, cw_ref, gw_ref, gb_ref = ins[:8]
    state_ref = ins[8] if has_state else None
    out_ref, newb_ref = outs[:2]
    v_ref = outs[2] if has_state else None
    xn_ref, xp_ref, ycat_ref, vb_ref, z_ref = scr
    finish_cast = _maybe_cast_weights(cast_layers, grid_shape, srcs, dsts, cast_scr)

    rb = SHIFT_ROW_BLOCK if stride == 1 else ROW_BLOCK
    mc = MXU_COLS
    hist = (B_CONV - 1) * stride
    h_rows = _round_up(hist, SUBLANES)
    _init_history(xp_ref, state_ref, hist, pl.program_id(1) == 0)
    lng = lng_ref[...]
    allc = slice(0, B_WIDTH)
    conv_rows = [cw_ref[k:k + 1, :] for k in range(B_CONV)]
    if stride == 1:
        row_i = lax.broadcasted_iota(jnp.int32, (CHUNK, CHUNK), 0)
        col_i = lax.broadcasted_iota(jnp.int32, (CHUNK, CHUNK), 1)
        wts = [jnp.where(col_i <= row_i, gw_ref[h], 0.0).astype(_BF16) for h in range(A_HEADS)]
    sub = min(tm, SUB_TILE)
    n_sub = tm // sub
    u0, v0, bg0, cg0, hh0 = (k * A_WIDTH for k in range(5))


    def in_proj_pieces(si):
        srows = slice(si * sub, (si + 1) * sub)

        def piece(c0):
            def emit():
                z = _dot(xn_ref[srows, :], win_ref[:, c0:c0 + mc])
                z_ref[si % 2, :, c0:c0 + mc] = z
                return z[0:1, 0:LANES]
            return emit

        order = list(range(bg0, hh0 + B_WIDTH, mc)) + list(range(u0, bg0, mc))
        return [piece(c0) for c0 in order]

    def b_block(si, r0):
        def emit(after):
            loc = slice(r0 - si * sub, r0 - si * sub + rb)
            z = z_ref.at[si % 2]
            xp_ref[h_rows + r0:h_rows + r0 + rb, :] = z[loc, cg0:cg0 + B_WIDTH] * z[loc, hh0:hh0 + B_WIDTH]
            w_rows = conv_rows[:-1] + [_ordered_after(conv_rows[-1], after)]
            conv = _conv_taps(xp_ref, cw_ref, r0, allc, B_CONV, stride, rb, w_rows=w_rows)
            ycat_ref[r0:r0 + rb, A_WIDTH:A_WIDTH + B_WIDTH] = (z[loc, bg0:bg0 + B_WIDTH] * conv).astype(_BF16)
        return emit

    def ln_block(si, r0):
        def emit(after):
            gain = _ordered_after(lng, after)
            for q0 in range(r0, r0 + rb, NORM_BLOCK):
                rows = slice(q0, q0 + NORM_BLOCK)
                loc = slice(q0 - si * sub, q0 - si * sub + NORM_BLOCK)
                v = _ln_rows(jax.nn.gelu(z_ref[si % 2, loc, v0:v0 + A_WIDTH]), gain)
                if v_ref is not None:
                    v_ref[rows, :] = v
                else:
                    vb_ref[rows, :] = v.astype(_BF16)
        return emit

    def gate_items(si):
        s0 = si * sub
        z = z_ref.at[si % 2]
        items = []
        if stride == 1:
            for c0 in range(s0, s0 + sub, 2 * CHUNK):
                for h in range(A_HEADS):
                    def emit(after, c0=c0, h=h):
                        ra = slice(c0, c0 + CHUNK)
                        rb2 = slice(c0 + CHUNK, c0 + 2 * CHUNK)
                        cols = slice(h * CHUNK, (h + 1) * CHUNK)
                        vpair = jnp.concatenate([vb_ref[ra, cols], vb_ref[rb2, cols]], axis=1)
                        mixed = _dot(wts[h], vpair)
                        bias = gb_ref[:, h:h + 1]
                        for half, rws in enumerate((ra, rb2)):
                            m = mixed[:, half * CHUNK:(half + 1) * CHUNK] + bias
                            u = z[rws.start - s0:rws.stop - s0, u0 + h * CHUNK:u0 + (h + 1) * CHUNK]
                            ycat_ref[rws, cols] = (jax.nn.gelu(u) * m).astype(_BF16)
                    items.append(emit)
        else:
            steps = tm // stride
            gw_off = gate_layer * A_HEADS * steps * steps
            gb_off = gate_layer * A_HEADS * steps
            for t in range(s0 // stride, (s0 + sub) // stride):
                def emit(after, t=t):
                    for n0 in range(0, stride, rb):
                        rows = slice(t * stride + n0, t * stride + n0 + rb)
                        for h in range(A_HEADS):
                            cols = slice(h * CHUNK, (h + 1) * CHUNK)
                            mixed = None
                            for s in range(t + 1):
                                w = gw_ref[gw_off + (h * steps + t) * steps + s]
                                term = w * v_ref[s * stride + n0:s * stride + n0 + rb, cols]
                                mixed = term if mixed is None else mixed + term
                            mixed = mixed + gb_ref[gb_off + h * steps + t]
                            u = z[rows.start - s0:rows.stop - s0, u0 + h * CHUNK:u0 + (h + 1) * CHUNK]
                            ycat_ref[rows, cols] = (jax.nn.gelu(u) * mixed).astype(_BF16)
                items.append(emit)
        return items

    def out_proj_pieces(si):
        srows = slice(si * sub, (si + 1) * sub)

        def piece(c0):
            def emit():
                y = _dot(ycat_ref[srows, :], wout_ref[:, c0:c0 + mc])
                out_ref[srows, c0:c0 + mc] = x_ref[srows, c0:c0 + mc] + y
                return y[0:1, 0:LANES]
            return emit

        return [piece(c0) for c0 in range(0, D_MODEL, mc)]

    _norm_to_bf16(x_ref, g_ref, xn_ref, 0, sub)
    for emit in in_proj_pieces(0):
        emit()
    for si in range(n_sub):
        s0 = si * sub
        side = []
        if si + 1 < n_sub:
            _norm_to_bf16(x_ref, g_ref, xn_ref, s0 + sub, sub)
            side += in_proj_pieces(si + 1)
        if si >= 1:
            side += out_proj_pieces(si - 1)
        blocks = ([b_block(si, r0) for r0 in range(s0, s0 + sub, rb)]
                  + [ln_block(si, r0) for r0 in range(s0, s0 + sub, rb)] + gate_items(si))
        after = None
        for emit in blocks:
            emit(after)
            after = side.pop(0)() if side else None
        for emit in side:
            emit()
    for emit in out_proj_pieces(n_sub - 1):
        emit()
    _emit_history(xp_ref, newb_ref, hist, tm, carry=not has_state)
    finish_cast()


def _pool_block(xp_ref, r0, gi, win, stride, rb, pos0, row_pos_static):
    cols = slice(gi * D_GROUP_DIM, (gi + 1) * D_GROUP_DIM)
    hist = POOL_BUF * stride
    h_rows = _round_up(hist, SUBLANES)
    if stride % SUBLANES == 0:
        s = None
        for j in range(win):
            off = h_rows + r0 - j * stride
            term = xp_ref[off:off + rb, cols]
            s = term if s is None else s + term
    else:
        halo = 2 * SUBLANES
        ext = xp_ref[h_rows + r0 - halo:h_rows + r0 + rb, cols]
        shift = 1
        while shift < win:
            ext = ext + pltpu.roll(ext, shift, 0)
            shift *= 2
        s = ext[halo:]
    if row_pos_static is not None:
        return s / float(min(row_pos_static + 1, win))
    pos = pos0 + r0 + lax.broadcasted_iota(jnp.int32, (rb, D_GROUP_DIM), 0)
    cnt = jnp.minimum(pos + 1, win).astype(_F32)
    return s / cnt


def _odd_kernel(*refs, tm, stride, has_state, start_pos, cast_layers, grid_shape):
    ins, srcs, outs, dsts, scr, cast_scr = _split_refs(refs, 12 if has_state else 10, 3, 5, len(cast_layers))
    x_ref, g_ref, win_ref, wout_ref, ccw_ref, ccb_ref, clg_ref, clb_ref, dproj_ref, dscale_ref = ins[:10]
    statec_ref, stated_ref = (ins[10], ins[11]) if has_state else (None, None)
    out_ref, newc_ref, newd_ref = outs
    xn_ref, xpc_ref, xpd_ref, ycat_ref, zb_ref = scr
    finish_cast = _maybe_cast_weights(cast_layers, grid_shape, srcs, dsts, cast_scr)

    rb = SHIFT_ROW_BLOCK if stride == 1 else ROW_BLOCK
    hist_c = (C_CONV - 1) * stride
    hc_rows = _round_up(hist_c, SUBLANES)
    hist_d = POOL_BUF * stride
    hd_rows = _round_up(hist_d, SUBLANES)
    first = pl.program_id(1) == 0
    _init_history(xpc_ref, statec_ref, hist_c, first)
    _init_history(xpd_ref, stated_ref, hist_d, first)
    ccb = ccb_ref[...]
    clg = clg_ref[...]
    clb = clb_ref[...]
    pos0 = start_pos + pl.program_id(1) * tm
    mc = MXU_COLS
    cuts = list(range(0, tm + 1, min(tm, ODD_SUB_TILE)))
    tiles = list(zip(cuts[:-1], cuts[1:]))
    n_sub = len(tiles)


    def in_proj_pieces(s0, s1):
        srows = slice(s0, s1)

        def glu_piece(c0):
            def emit():
                xn = xn_ref[srows, :]
                xpc_ref[hc_rows + s0:hc_rows + s1, c0:c0 + mc] = _dot(xn, win_ref[:, c0:c0 + mc])
                zb = _dot(xn, win_ref[:, C_WIDTH + c0:C_WIDTH + c0 + mc])
                zb_ref[srows, c0:c0 + mc] = zb
                return zb[0:1, 0:LANES]
            return emit

        def pool_in_piece(c0):
            def emit():
                zp = _dot(xn_ref[srows, :], win_ref[:, 2 * C_WIDTH + c0:2 * C_WIDTH + c0 + mc])
                xpd_ref[hd_rows + s0:hd_rows + s1, c0:c0 + mc] = zp
                return zp[0:1, 0:LANES]
            return emit

        return ([glu_piece(c0) for c0 in range(0, C_WIDTH, mc)]
                + [pool_in_piece(c0) for c0 in range(0, D_WIDTH, mc)])

    gated = {}

    def conv_block(s0, s1, r0):
        def emit(after):
            lo = gated.get(s0, s0)
            hi = min(r0 + rb + SUBLANES, s1)
            for g0 in range(lo, hi, rb):
                grows = slice(hc_rows + g0, hc_rows + min(g0 + rb, hi))
                xpc_ref[grows, :] = xpc_ref[grows, :] * jax.nn.sigmoid(zb_ref[g0:min(g0 + rb, hi), :])
            gated[s0] = hi
            parts = []
            for c0 in range(0, C_WIDTH, LANES):
                cols = slice(c0, c0 + LANES)
                if stride % SUBLANES == 0:
                    parts.append(_conv_taps(xpc_ref, ccw_ref, r0, cols, C_CONV, stride, rb))
                else:
                    parts.append(_conv_long_stride1(xpc_ref, ccw_ref, r0, cols, C_CONV, rb))
            cc = jnp.concatenate(parts, axis=-1) + _ordered_after(ccb, after)
            ycat_ref[r0:r0 + rb, 0:C_WIDTH] = jax.nn.silu(_ln_rows(cc, clg, clb)).astype(_BF16)
        return emit

    def pool_block(r0):
        def emit(after):
            for gi, win in enumerate(POOL_WINDOWS):
                cols = slice(gi * D_GROUP_DIM, (gi + 1) * D_GROUP_DIM)
                if stride == 1:
                    static_pos = None if r0 < POOL_BUF else POOL_BUF
                else:
                    static_pos = start_pos + r0 // stride
                pooled = _pool_block(xpd_ref, r0, gi, win, stride, rb, pos0, static_pos)
                diff = (pooled - xpd_ref[hd_rows + r0:hd_rows + r0 + rb, cols]).astype(_BF16)
                scale = dscale_ref[:, cols]
                if gi == 0:
                    scale = _ordered_after(scale, after)
                yd = _dot(diff, dproj_ref[gi]) * scale
                ycat_ref[r0:r0 + rb, C_WIDTH + gi * D_GROUP_DIM:C_WIDTH + (gi + 1) * D_GROUP_DIM] = yd.astype(_BF16)
        return emit

    def out_proj_pieces(s0, s1):
        srows = slice(s0, s1)

        def piece(c0):
            def emit():
                y = _dot(ycat_ref[srows, :], wout_ref[:, c0:c0 + mc])
                out_ref[srows, c0:c0 + mc] = x_ref[srows, c0:c0 + mc] + y
                return y[0:1, 0:LANES]
            return emit

        return [piece(c0) for c0 in range(0, D_MODEL, mc)]

    _norm_to_bf16(x_ref, g_ref, xn_ref, 0, tiles[0][1])
    for emit in in_proj_pieces(*tiles[0]):
        emit()
    for si, (s0, s1) in enumerate(tiles):
        xpc_ref[hc_rows + s1:hc_rows + s1 + SUBLANES, :] = jnp.zeros((SUBLANES, C_WIDTH), _F32)
        side = []
        if si + 1 < n_sub:
            n0, n1 = tiles[si + 1]
            _norm_to_bf16(x_ref, g_ref, xn_ref, n0, n1 - n0)
            side += in_proj_pieces(n0, n1)
        if si >= 1:
            side += out_proj_pieces(*tiles[si - 1])
        blocks = [conv_block(s0, s1, r0) for r0 in range(s0, s1, rb)] + [pool_block(r0) for r0 in range(s0, s1, rb)]
        after = None
        for emit in blocks:
            emit(after)
            after = side.pop(0)() if side else None
        for emit in side:
            emit()
    for emit in out_proj_pieces(*tiles[-1]):
        emit()
    _emit_history(xpc_ref, newc_ref, hist_c, tm, carry=not has_state)
    _emit_history(xpd_ref, newd_ref, hist_d, tm, carry=not has_state)
    finish_cast()


def _ffn_kernel(*refs, stride, has_state, final_norm):
    refs = list(refs)
    x_ref, g_ref, win_ref, wout_ref, cw_ref = refs[:5]
    del refs[:5]
    state_ref = refs.pop(0) if has_state else None
    gfin_ref = refs.pop(0) if final_norm else None
    out_ref, newf_ref, xn_ref, gp_ref, hist_ref, h_ref = refs
    rb = FFN_ROW_BLOCK
    hist = (FFN_CONV - 1) * stride
    h_rows = _round_up(hist, SUBLANES)
    _init_history(hist_ref, state_ref, hist, pl.program_id(1) == 0)
    n_sub, sub, _ = x_ref.shape

    def sub_tile(i, carry):
        xs = x_ref.at[i]
        os = out_ref.at[i]
        _norm_to_bf16(xs, g_ref, xn_ref, 0, sub)
        xn = xn_ref[...]
        for j, c0 in enumerate(range(0, D_FF, FF_CHUNK)):
            cols = slice(c0, c0 + FF_CHUNK)
            slot = slice((j % 2) * FF_CHUNK, (j % 2 + 1) * FF_CHUNK)
            gp_ref[0:h_rows, slot] = hist_ref[:, cols]
            zg = _dot(xn, win_ref[:, c0:c0 + FF_CHUNK])
            zu = _dot(xn, win_ref[:, D_FF + c0:D_FF + c0 + FF_CHUNK])
            for r0 in range(0, sub, rb):
                rows = slice(r0, r0 + rb)
                gp_ref[h_rows + r0:h_rows + r0 + rb, slot] = zg[rows]
                conv = _conv_taps(gp_ref, cw_ref, r0, cols, FFN_CONV, stride, rb, xcols=slot)
                h_ref[rows, cols] = (jax.nn.gelu(conv) * zu[rows]).astype(_BF16)
            hist_ref[:, cols] = gp_ref[sub:sub + h_rows, slot]

        y = xs[...] + _dot(h_ref[...], wout_ref[...])
        if final_norm:
            gfin = gfin_ref[...]
            for r0 in range(0, sub, NORM_BLOCK):
                rows = slice(r0, r0 + NORM_BLOCK)
                os[rows, :] = _rms_rows(y[rows], gfin)
        else:
            os[...] = y
        return carry

    lax.fori_loop(0, n_sub, sub_tile, 0)
    newf_ref[0] = hist_ref[h_rows - hist:h_rows, :]


def _const_spec(shape):
    nd = len(shape)
    return pl.BlockSpec(shape, lambda b, t: (0,) * nd, pipeline_mode=pl.Buffered(1))


def _layer_spec(arr, layer):
    nd = arr.ndim - 1
    return pl.BlockSpec((None,) + arr.shape[1:], lambda b, t: (layer,) + (0,) * nd, pipeline_mode=pl.Buffered(1))


def _smem_spec():
    return pl.BlockSpec(memory_space=pltpu.SMEM)


def _hbm_spec():
    return pl.BlockSpec(memory_space=pl.ANY)


def _tile_spec(tm, nt, width):
    return pl.BlockSpec((tm, width), lambda b, t: (b * nt + t, 0))


def _state_spec(layer, rows, width):
    return pl.BlockSpec((None, 1, rows, width), lambda b, t: (layer, b, 0, 0))


def _new_state_spec(rows, width):
    return pl.BlockSpec((1, rows, width), lambda b, t: (b, 0, 0))


def _params():
    return pltpu.CompilerParams(dimension_semantics=("arbitrary", "arbitrary"), vmem_limit_bytes=VMEM_LIMIT)


def _cast_plumbing(cast_jobs, n_steps):
    in_specs, args, out_specs, out_shape, stage_in, stage_out = [], [], [], [], [], []
    for arr, _ in cast_jobs:
        _, r, c = arr.shape
        rc = r // n_steps
        assert rc * n_steps == r and rc % BF16_ROWS == 0, (arr.shape, n_steps)
        in_specs.append(_hbm_spec())
        args.append(arr)
        out_specs.append(_hbm_spec())
        out_shape.append(jax.ShapeDtypeStruct((r, c), _BF16))
        stage_in.append(pltpu.VMEM((2, rc, c), _F32))
        stage_out.append(pltpu.VMEM((2, rc, c), _BF16))
    scratch = stage_in + stage_out
    if cast_jobs:
        n = len(cast_jobs)
        scratch += [pltpu.SemaphoreType.DMA((n, 2)), pltpu.SemaphoreType.DMA((n, 2))]
    return in_specs, args, out_specs, out_shape, scratch, tuple(layer for _, layer in cast_jobs)


def _even_call(x2, nb, nt, stride, state, layer, win, wout, prm, cast_jobs, name):
    has_state = state is not None
    i = layer // 2
    tm = x2.shape[0] // (nb * nt)
    hist = (B_CONV - 1) * stride
    h_rows = _round_up(hist, SUBLANES)
    c_in, c_args, c_out_specs, c_out_shape, c_scratch, cast_layers = _cast_plumbing(cast_jobs, nb * nt)
    in_specs = [_tile_spec(tm, nt, D_MODEL), _layer_spec(prm['norm_mix_g'], layer), _const_spec(win.shape),
                _const_spec(wout.shape), _layer_spec(prm['a_ln_g'], i), _layer_spec(prm['b_conv_w'], i)]
    args = [x2, prm['norm_mix_g'], win, wout, prm['a_ln_g'], prm['b_conv_w']]
    if has_state:
        in_specs += [_smem_spec(), _smem_spec(), _state_spec(i, hist, B_WIDTH)]
        args += [prm['a_ws_steps'], prm['a_bs_steps'], state]
    else:
        in_specs += [_layer_spec(prm['a_ws'], i), _layer_spec(prm['a_bs_t'], i)]
        args += [prm['a_ws'], prm['a_bs_t']]
    out_shape = [jax.ShapeDtypeStruct(x2.shape, _F32), jax.ShapeDtypeStruct((nb, hist, B_WIDTH), _F32)]
    out_specs = [_tile_spec(tm, nt, D_MODEL), _new_state_spec(hist, B_WIDTH)]
    if has_state:
        out_shape.append(jax.ShapeDtypeStruct((x2.shape[0], A_WIDTH), _F32))
        out_specs.append(_tile_spec(tm, nt, A_WIDTH))
    sub = min(tm, SUB_TILE)
    scratch = [pltpu.VMEM((tm, D_MODEL), _BF16), pltpu.VMEM((h_rows + tm, B_WIDTH), _F32),
               pltpu.VMEM((tm, A_WIDTH + B_WIDTH), _BF16), pltpu.VMEM((tm, A_WIDTH), _BF16),
               pltpu.VMEM((min(tm // sub, 2), sub, win.shape[1]), _F32)]
    return pl.pallas_call(
        functools.partial(_even_kernel, tm=tm, stride=stride, has_state=has_state, gate_layer=i,
                          cast_layers=cast_layers, grid_shape=(nb, nt)),
        grid=(nb, nt), in_specs=in_specs + c_in, out_specs=out_specs + c_out_specs,
        out_shape=out_shape + c_out_shape, scratch_shapes=scratch + c_scratch,
        compiler_params=_params(), name=name)(*args, *c_args)


def _odd_call(x2, nb, nt, stride, state_c, state_d, start_pos, layer, win, wout, prm, cast_jobs, name):
    has_state = state_c is not None
    i = layer // 2
    hist_c = (C_CONV - 1) * stride
    hist_d = POOL_BUF * stride
    hc_rows = _round_up(hist_c, SUBLANES)
    hd_rows = _round_up(hist_d, SUBLANES)
    tm = x2.shape[0] // (nb * nt)
    c_in, c_args, c_out_specs, c_out_shape, c_scratch, cast_layers = _cast_plumbing(cast_jobs, nb * nt)
    names = ['c_conv_w', 'c_conv_b', 'c_ln_g', 'c_ln_b', 'd_proj', 'd_scale']
    in_specs = [_tile_spec(tm, nt, D_MODEL), _layer_spec(prm['norm_mix_g'], layer), _const_spec(win.shape),
                _const_spec(wout.shape)] + [_layer_spec(prm[k], i) for k in names]
    args = [x2, prm['norm_mix_g'], win, wout] + [prm[k] for k in names]
    if has_state:
        in_specs += [_state_spec(i, hist_c, C_WIDTH), _state_spec(i, hist_d, D_WIDTH)]
        args += [state_c, state_d]
    out_shape = [jax.ShapeDtypeStruct(x2.shape, _F32), jax.ShapeDtypeStruct((nb, hist_c, C_WIDTH), _F32),
                 jax.ShapeDtypeStruct((nb, hist_d, D_WIDTH), _F32)]
    out_specs = [_tile_spec(tm, nt, D_MODEL), _new_state_spec(hist_c, C_WIDTH), _new_state_spec(hist_d, D_WIDTH)]
    scratch = [pltpu.VMEM((tm, D_MODEL), _BF16), pltpu.VMEM((hc_rows + tm + SUBLANES, C_WIDTH), _F32),
               pltpu.VMEM((hd_rows + tm, D_WIDTH), _F32), pltpu.VMEM((tm, C_WIDTH + D_WIDTH), _BF16),
               pltpu.VMEM((tm, C_WIDTH), _F32)]
    return pl.pallas_call(
        functools.partial(_odd_kernel, tm=tm, stride=stride, has_state=has_state, start_pos=start_pos,
                          cast_layers=cast_layers, grid_shape=(nb, nt)),
        grid=(nb, nt), in_specs=in_specs + c_in, out_specs=out_specs + c_out_specs,
        out_shape=out_shape + c_out_shape, scratch_shapes=scratch + c_scratch,
        compiler_params=_params(), name=name)(*args, *c_args)


def _ffn_call(x2, nb, nt, stride, state, layer, final_norm, win, wout, prm, name):
    has_state = state is not None
    hist = (FFN_CONV - 1) * stride
    h_rows = _round_up(hist, SUBLANES)
    tm = x2.shape[0] // (nb * nt)
    sub = min(tm, SUB_TILE)
    n_sub = tm // sub
    x3 = x2.reshape(-1, sub, D_MODEL)
    x_spec = pl.BlockSpec((n_sub, sub, D_MODEL), lambda b, t: (b * nt + t, 0, 0))
    in_specs = [x_spec, _layer_spec(prm['norm_ffn_g'], layer), _const_spec(win.shape), _const_spec(wout.shape),
                _layer_spec(prm['ffn_conv_w'], layer)]
    args = [x3, prm['norm_ffn_g'], win, wout, prm['ffn_conv_w']]
    if has_state:
        in_specs.append(_state_spec(layer, hist, D_FF))
        args.append(state)
    if final_norm:
        in_specs.append(_const_spec((1, D_MODEL)))
        args.append(prm['norm_final_g'])
    out_shape = [jax.ShapeDtypeStruct(x3.shape, _F32), jax.ShapeDtypeStruct((nb, hist, D_FF), _F32)]
    out_specs = [x_spec, _new_state_spec(hist, D_FF)]
    scratch = [pltpu.VMEM((sub, D_MODEL), _BF16), pltpu.VMEM((h_rows + sub, 2 * FF_CHUNK), _F32),
               pltpu.VMEM((h_rows, D_FF), _F32), pltpu.VMEM((sub, D_FF), _BF16)]
    y3, newf = pl.pallas_call(
        functools.partial(_ffn_kernel, stride=stride, has_state=has_state, final_norm=final_norm),
        grid=(nb, nt), in_specs=in_specs, out_specs=out_specs, out_shape=out_shape,
        scratch_shapes=scratch, compiler_params=_params(), name=name)(*args)
    return y3.reshape(x2.shape), newf


def _to_tiles(a, ns):
    *lead, n, l, c = a.shape
    k = len(lead)
    a = a.reshape(*lead, n // ns, ns, l, c)
    a = a.transpose(*range(k), k, k + 2, k + 1, k + 3)
    return a.reshape(*lead, n // ns, l * ns, c)


def _from_tiles(a, ns, l):
    *lead, nb, _, c = a.shape
    k = len(lead)
    a = a.reshape(*lead, nb, l, ns, c)
    a = a.transpose(*range(k), k, k + 2, k + 1, k + 3)
    return a.reshape(*lead, nb * ns, l, c)


def _trunk(x2, nb, nt, stride, states, start_pos, prm, wb, raw, tag):
    new_a, new_b, new_c, new_d, new_f = [], [], [], [], []
    depth = prm['ffn_conv_w'].shape[0]
    for layer in range(depth):
        cast_keys, cast_jobs = [], []
        if raw is not None:
            cast_keys.append(('ffn', layer))
            cast_jobs += [(raw['w_ffn_in'], layer), (raw['w_ffn_out'], layer)]
            if layer + 1 < depth:
                nxt = 'even' if (layer + 1) % 2 == 0 else 'odd'
                cast_keys.append(('mix', layer + 1))
                cast_jobs += [(raw['w_in_' + nxt], (layer + 1) // 2), (raw['w_out_' + nxt], (layer + 1) // 2)]
        win, wout = wb[('mix', layer)]
        if layer % 2 == 0:
            st = None if states is None else states['b']
            res = _even_call(x2, nb, nt, stride, st, layer, win, wout, prm, cast_jobs, name=f"{tag}_mix{layer}")
            n_own = 2 if states is None else 3
            x2 = res[0]
            new_b.append(res[1])
            if states is not None:
                new_a.append(res[2])
        else:
            st_c = None if states is None else states['c']
            st_d = None if states is None else states['d']
            res = _odd_call(x2, nb, nt, stride, st_c, st_d, start_pos, layer, win, wout, prm, cast_jobs,
                            name=f"{tag}_mix{layer}")
            n_own = 3
            x2 = res[0]
            new_c.append(res[1])
            new_d.append(res[2])
        for k, key in enumerate(cast_keys):
            wb[key] = (res[n_own + 2 * k], res[n_own + 2 * k + 1])
        st_f = None if states is None else states['f']
        win, wout = wb[('ffn', layer)]
        ffn_nt = 1 if states is None else nt
        x2, nf = _ffn_call(x2, nb, ffn_nt, stride, st_f, layer, layer == depth - 1, win, wout, prm,
                           name=f"{tag}_ffn{layer}")
        new_f.append(nf)
    return x2, new_a, new_b, new_c, new_d, new_f


def kernel(x_prompt, x_sample, state_b_conv, state_c_conv, state_d_pool, state_ffn_conv, norm_mix_g, norm_ffn_g, norm_final_g, w_in_even, w_out_even, a_ln_g, a_ws, a_bs, b_conv_w, w_in_odd, w_out_odd, c_conv_w, c_conv_b, c_ln_g, c_ln_b, d_proj, d_scale, w_ffn_in, ffn_conv_w, w_ffn_out):
    batch, seq, d = x_prompt.shape
    dec_batch, dec_seq, _ = x_sample.shape
    ns = SAMPLE_TILE // dec_seq

    def rows(p):
        return p.reshape(p.shape[0], 1, p.shape[1])

    prm = dict(norm_mix_g=rows(norm_mix_g), norm_ffn_g=rows(norm_ffn_g), norm_final_g=norm_final_g.reshape(1, -1),
               a_ln_g=rows(a_ln_g), a_ws=a_ws, a_bs_t=jnp.swapaxes(a_bs, 1, 2),
               a_ws_steps=a_ws[:, :, :dec_seq, :dec_seq].reshape(-1), a_bs_steps=a_bs[:, :, :dec_seq].reshape(-1),
               b_conv_w=b_conv_w,
               c_conv_w=jnp.pad(c_conv_w, ((0, 0), (0, _round_up(C_CONV, SUBLANES) - C_CONV), (0, 0))),
               c_conv_b=rows(c_conv_b), c_ln_g=rows(c_ln_g), c_ln_b=rows(c_ln_b), d_proj=d_proj.astype(_BF16),
               d_scale=rows(d_scale), ffn_conv_w=ffn_conv_w)
    raw = dict(w_in_even=w_in_even, w_out_even=w_out_even, w_in_odd=w_in_odd, w_out_odd=w_out_odd,
               w_ffn_in=w_ffn_in, w_ffn_out=w_ffn_out)
    wb = {('mix', 0): (w_in_even[0].astype(_BF16), w_out_even[0].astype(_BF16))}

    yp, _, b_p, c_p, d_p, f_p = _trunk(x_prompt.reshape(batch * seq, d), batch, seq // PROMPT_TILE, 1, None, 0,
                                       prm, wb, raw, "p")

    states = dict(b=_to_tiles(state_b_conv, ns), c=_to_tiles(state_c_conv, ns), d=_to_tiles(state_d_pool, ns),
                  f=_to_tiles(state_ffn_conv, ns))
    xs = _to_tiles(x_sample, ns).reshape(dec_batch * dec_seq, d)
    ys, a_s, b_s, c_s, d_s, f_s = _trunk(xs, dec_batch // ns, 1, ns, states, PAST_LEN, prm, wb, None, "s")

    def untile(parts, l):
        return _from_tiles(jnp.stack(parts), ns, l)

    y_prompt = yp.reshape(batch, seq, d)
    y_sample = _from_tiles(ys.reshape(dec_batch // ns, dec_seq * ns, d), ns, dec_seq)
    new_a = untile([a.reshape(dec_batch // ns, dec_seq * ns, A_WIDTH) for a in a_s], dec_seq)
    return (y_prompt, y_sample, new_a, jnp.stack(b_p), untile(b_s, B_CONV - 1), jnp.stack(c_p),
            untile(c_s, C_CONV - 1), jnp.stack(d_p), untile(d_s, POOL_BUF), jnp.stack(f_p),
            untile(f_s, FFN_CONV - 1))
```

```python
import functools

import jax
import jax.numpy as jnp
from jax import lax
from jax.experimental import pallas as pl
from jax.experimental.pallas import tpu as pltpu

D_MODEL = 1024
CHUNK = 128
A_HEADS = 4
A_WIDTH = 512
B_WIDTH = 512
B_CONV = 3
C_WIDTH = 512
C_CONV = 31
D_WIDTH = 512
POOL_WINDOWS = (2, 4, 8, 16)
D_GROUP_DIM = 128
POOL_BUF = 15
D_FF = 2816
FFN_CONV = 3
PAST_LEN = 16384
EPS = 1e-6

SUBLANES = 8
LANES = 128
BF16_ROWS = 16
MXU_COLS = 256
PROMPT_TILE = 1024
SAMPLE_TILE = 512
SUB_TILE = 512
ODD_SUB_TILE = 512
ROW_BLOCK = 64
SHIFT_ROW_BLOCK = 128
FFN_ROW_BLOCK = 256
NORM_BLOCK = 32
FF_CHUNK = 256
CAST_DMA_PRIORITY = 1
VMEM_LIMIT = 56 * 1024 * 1024

_BF16 = jnp.bfloat16
_F32 = jnp.float32


def _round_up(n, m):
    return (n + m - 1) // m * m


def _dot(a, b):
    return jnp.dot(a, b, preferred_element_type=_F32)


def _rms_rows(x, g):
    y = x * lax.rsqrt(jnp.mean(x * x, axis=-1, keepdims=True) + EPS)
    return y * g


def _ln_rows(x, g, b=None):
    mu = jnp.mean(x, axis=-1, keepdims=True)
    xc = x - mu
    y = xc * lax.rsqrt(jnp.mean(xc * xc, axis=-1, keepdims=True) + EPS) * g
    if b is not None:
        y = y + b
    return y


def _norm_to_bf16(x_ref, g_ref, xn_ref, s0, sub):
    g = g_ref[...]
    for r0 in range(s0, s0 + sub, NORM_BLOCK):
        rows = slice(r0, r0 + NORM_BLOCK)
        xn_ref[rows, :] = _rms_rows(x_ref[rows, :], g).astype(_BF16)


def _init_history(xp_ref, state_ref, hist, first_tile):
    h_rows = _round_up(hist, SUBLANES)
    if state_ref is None:
        @pl.when(first_tile)
        def _():
            xp_ref[0:h_rows, :] = jnp.zeros((h_rows, xp_ref.shape[1]), _F32)
    else:
        xp_ref[h_rows - hist:h_rows, :] = state_ref[0]


def _emit_history(xp_ref, new_ref, hist, tm, carry):
    h_rows = _round_up(hist, SUBLANES)
    tail = xp_ref[h_rows + tm - hist:h_rows + tm, :]
    new_ref[0] = tail
    if carry:
        xp_ref[h_rows - hist:h_rows, :] = tail


def _conv_taps(xp_ref, w_ref, r0, cols, ktaps, stride, rb, xcols=None, w_rows=None):
    xcols = cols if xcols is None else xcols
    hist = (ktaps - 1) * stride
    base = _round_up(hist, SUBLANES) - hist + r0
    acc = None
    for k in range(ktaps):
        w_k = w_ref[k:k + 1, cols] if w_rows is None else w_rows[k]
        term = w_k * xp_ref[base + k * stride:base + k * stride + rb, xcols]
        acc = term if acc is None else acc + term
    return acc


def _conv_long_stride1(xp_ref, w_ref, r0, cols, ktaps, rb):
    hist = ktaps - 1
    base = _round_up(hist, SUBLANES) - hist
    n = rb + SUBLANES
    p = None
    for r in range(SUBLANES - 1, -1, -1):
        z = None
        for k in range(ktaps):
            if (base + k) % SUBLANES != r:
                continue
            off = r0 + (base + k) - r
            term = w_ref[k:k + 1, cols] * xp_ref[off:off + n, cols]
            z = term if z is None else z + term
        if p is not None:
            p = pltpu.roll(p, n - 1, 0)
            p = p if z is None else p + z
        else:
            p = z
    return p[0:rb]


def _ordered_after(row, token):
    if token is None:
        return row
    half = jnp.uint32(jnp.iinfo(jnp.uint32).bits // 2)
    bits = lax.bitcast_convert_type(token, jnp.uint32)
    zero = lax.shift_right_logical(lax.shift_right_logical(bits, half), half)
    head = lax.bitcast_convert_type(lax.bitcast_convert_type(row[:, 0:LANES], jnp.uint32) + zero, _F32)
    if row.shape[1] == LANES:
        return head
    return jnp.concatenate([head, row[:, LANES:]], axis=1)


def _cast_weights_step(step, n_steps, layers, srcs, dsts, stage_in, stage_out, sem_in, sem_out):
    n_jobs = len(layers)

    def in_copy(j, s, slot):
        rc = stage_in[j].shape[1]
        return pltpu.make_async_copy(srcs[j].at[layers[j], pl.ds(s * rc, rc), :], stage_in[j].at[slot],
                                     sem_in.at[j, slot])

    def out_copy(j, s, slot):
        rc = stage_out[j].shape[1]
        return pltpu.make_async_copy(stage_out[j].at[slot], dsts[j].at[pl.ds(s * rc, rc), :], sem_out.at[j, slot])

    slot = lax.rem(step, 2)

    @pl.when(step == 0)
    def _():
        for j in range(n_jobs):
            in_copy(j, 0, 0).start(priority=CAST_DMA_PRIORITY)

    for j in range(n_jobs):
        in_copy(j, step, slot).wait()

    @pl.when(step + 1 < n_steps)
    def _():
        for j in range(n_jobs):
            in_copy(j, step + 1, 1 - slot).start(priority=CAST_DMA_PRIORITY)

    @pl.when(step >= 2)
    def _():
        for j in range(n_jobs):
            out_copy(j, step - 2, slot).wait()

    for j in range(n_jobs):
        src = stage_in[j].at[slot]
        dst = stage_out[j].at[slot]
        for r0 in range(0, src.shape[0], BF16_ROWS):
            dst[r0:r0 + BF16_ROWS, :] = src[r0:r0 + BF16_ROWS, :].astype(_BF16)
        out_copy(j, step, slot).start(priority=CAST_DMA_PRIORITY)

    def finish():
        @pl.when(step == n_steps - 1)
        def _():
            for j in range(n_jobs):
                if n_steps > 1:
                    out_copy(j, step - 1, 1 - slot).wait()
                out_copy(j, step, slot).wait()

    return finish


def _split_refs(refs, n_in, n_out, n_scratch, n_jobs):
    refs = list(refs)
    a = n_in
    b = a + n_jobs
    c = b + n_out
    d = c + n_jobs
    e = d + n_scratch
    return refs[:a], refs[a:b], refs[b:c], refs[c:d], refs[d:e], refs[e:]


def _maybe_cast_weights(cast_layers, grid_shape, srcs, dsts, cast_scratch):
    n_jobs = len(cast_layers)
    if n_jobs == 0:
        return lambda: None
    stage_in = cast_scratch[:n_jobs]
    stage_out = cast_scratch[n_jobs:2 * n_jobs]
    sem_in, sem_out = cast_scratch[2 * n_jobs:]
    nb, nt = grid_shape
    step = pl.program_id(0) * nt + pl.program_id(1)
    return _cast_weights_step(step, nb * nt, cast_layers, srcs, dsts, stage_in, stage_out, sem_in, sem_out)


def _even_kernel(*refs, tm, stride, has_state, gate_layer, cast_layers, grid_shape):
    ins, srcs, outs, dsts, scr, cast_scr = _split_refs(
        refs, 9 if has_state else 8, 3 if has_state else 2, 5, len(cast_layers))
    x_ref, g_ref, win_ref, wout_ref, lng_ref, cw_ref, gw_ref, gb_ref = ins[:8]
    state_ref = ins[8] if has_state else None
    out_ref, newb_ref = outs[:2]
    v_ref = outs[2] if has_state else None
    xn_ref, xp_ref, ycat_ref, vb_ref, z_ref = scr
    finish_cast = _maybe_cast_weights(cast_layers, grid_shape, srcs, dsts, cast_scr)

    rb = SHIFT_ROW_BLOCK if stride == 1 else ROW_BLOCK
    mc = MXU_COLS
    hist = (B_CONV - 1) * stride
    h_rows = _round_up(hist, SUBLANES)
    _init_history(xp_ref, state_ref, hist, pl.program_id(1) == 0)
    lng = lng_ref[...]
    allc = slice(0, B_WIDTH)
    conv_rows = [cw_ref[k:k + 1, :] for k in range(B_CONV)]
    if stride == 1:
        row_i = lax.broadcasted_iota(jnp.int32, (CHUNK, CHUNK), 0)
        col_i = lax.broadcasted_iota(jnp.int32, (CHUNK, CHUNK), 1)
        wts = [jnp.where(col_i <= row_i, gw_ref[h], 0.0).astype(_BF16) for h in range(A_HEADS)]
    sub = min(tm, SUB_TILE)
    n_sub = tm // sub
    u0, v0, bg0, cg0, hh0 = (k * A_WIDTH for k in range(5))


    def in_proj_pieces(si):
        srows = slice(si * sub, (si + 1) * sub)

        def piece(c0):
            def emit():
                z = _dot(xn_ref[srows, :], win_ref[:, c0:c0 + mc])
                z_ref[si % 2, :, c0:c0 + mc] = z
                return z[0:1, 0:LANES]
            return emit

        order = list(range(bg0, hh0 + B_WIDTH, mc)) + list(range(u0, bg0, mc))
        return [piece(c0) for c0 in order]

    def b_block(si, r0):
        def emit(after):
            loc = slice(r0 - si * sub, r0 - si * sub + rb)
            z = z_ref.at[si % 2]
            xp_ref[h_rows + r0:h_rows + r0 + rb, :] = z[loc, cg0:cg0 + B_WIDTH] * z[loc, hh0:hh0 + B_WIDTH]
            w_rows = conv_rows[:-1] + [_ordered_after(conv_rows[-1], after)]
            conv = _conv_taps(xp_ref, cw_ref, r0, allc, B_CONV, stride, rb, w_rows=w_rows)
            ycat_ref[r0:r0 + rb, A_WIDTH:A_WIDTH + B_WIDTH] = (z[loc, bg0:bg0 + B_WIDTH] * conv).astype(_BF16)
        return emit

    def ln_block(si, r0):
        def emit(after):
            gain = _ordered_after(lng, after)
            for q0 in range(r0, r0 + rb, NORM_BLOCK):
                rows = slice(q0, q0 + NORM_BLOCK)
                loc = slice(q0 - si * sub, q0 - si * sub + NORM_BLOCK)
                v = _ln_rows(jax.nn.gelu(z_ref[si % 2, loc, v0:v0 + A_WIDTH]), gain)
                if v_ref is not None:
                    v_ref[rows, :] = v
                else:
                    vb_ref[rows, :] = v.astype(_BF16)
        return emit

    def gate_items(si):
        s0 = si * sub
        z = z_ref.at[si % 2]
        items = []
        if stride == 1:
            for c0 in range(s0, s0 + sub, 2 * CHUNK):
                for h in range(A_HEADS):
                    def emit(after, c0=c0, h=h):
                        ra = slice(c0, c0 + CHUNK)
                        rb2 = slice(c0 + CHUNK, c0 + 2 * CHUNK)
                        cols = slice(h * CHUNK, (h + 1) * CHUNK)
                        vpair = jnp.concatenate([vb_ref[ra, cols], vb_ref[rb2, cols]], axis=1)
                        mixed = _dot(wts[h], vpair)
                        bias = gb_ref[:, h:h + 1]
                        for half, rws in enumerate((ra, rb2)):
                            m = mixed[:, half * CHUNK:(half + 1) * CHUNK] + bias
                            u = z[rws.start - s0:rws.stop - s0, u0 + h * CHUNK:u0 + (h + 1) * CHUNK]
                            ycat_ref[rws, cols] = (jax.nn.gelu(u) * m).astype(_BF16)
                    items.append(emit)
        else:
            steps = tm // stride
            gw_off = gate_layer * A_HEADS * steps * steps
            gb_off = gate_layer * A_HEADS * steps
            for t in range(s0 // stride, (s0 + sub) // stride):
                def emit(after, t=t):
                    for n0 in range(0, stride, rb):
                        rows = slice(t * stride + n0, t * stride + n0 + rb)
                        for h in range(A_HEADS):
                            cols = slice(h * CHUNK, (h + 1) * CHUNK)
                            mixed = None
                            for s in range(t + 1):
                                w = gw_ref[gw_off + (h * steps + t) * steps + s]
                                term = w * v_ref[s * stride + n0:s * stride + n0 + rb, cols]
                                mixed = term if mixed is None else mixed + term
                            mixed = mixed + gb_ref[gb_off + h * steps + t]
                            u = z[rows.start - s0:rows.stop - s0, u0 + h * CHUNK:u0 + (h + 1) * CHUNK]
                            ycat_ref[rows, cols] = (jax.nn.gelu(u) * mixed).astype(_BF16)
                items.append(emit)
        return items

    def out_proj_pieces(si):
        srows = slice(si * sub, (si + 1) * sub)

        def piece(c0):
            def emit():
                y = _dot(ycat_ref[srows, :], wout_ref[:, c0:c0 + mc])
                out_ref[srows, c0:c0 + mc] = x_ref[srows, c0:c0 + mc] + y
                return y[0:1, 0:LANES]
            return emit

        return [piece(c0) for c0 in range(0, D_MODEL, mc)]

    _norm_to_bf16(x_ref, g_ref, xn_ref, 0, sub)
    for emit in in_proj_pieces(0):
        emit()
    for si in range(n_sub):
        s0 = si * sub
        side = []
        if si + 1 < n_sub:
            _norm_to_bf16(x_ref, g_ref, xn_ref, s0 + sub, sub)
            side += in_proj_pieces(si + 1)
        if si >= 1:
            side += out_proj_pieces(si - 1)
        blocks = ([b_block(si, r0) for r0 in range(s0, s0 + sub, rb)]
                  + [ln_block(si, r0) for r0 in range(s0, s0 + sub, rb)] + gate_items(si))
        after = None
        for emit in blocks:
            emit(after)
            after = side.pop(0)() if side else None
        for emit in side:
            emit()
    for emit in out_proj_pieces(n_sub - 1):
        emit()
    _emit_history(xp_ref, newb_ref, hist, tm, carry=not has_state)
    finish_cast()


def _pool_block(xp_ref, r0, gi, win, stride, rb, pos0, row_pos_static):
    cols = slice(gi * D_GROUP_DIM, (gi + 1) * D_GROUP_DIM)
    hist = POOL_BUF * stride
    h_rows = _round_up(hist, SUBLANES)
    if stride % SUBLANES == 0:
        s = None
        for j in range(win):
            off = h_rows + r0 - j * stride
            term = xp_ref[off:off + rb, cols]
            s = term if s is None else s + term
    else:
        halo = 2 * SUBLANES
        ext = xp_ref[h_rows + r0 - halo:h_rows + r0 + rb, cols]
        shift = 1
        while shift < win:
            ext = ext + pltpu.roll(ext, shift, 0)
            shift *= 2
        s = ext[halo:]
    if row_pos_static is not None:
        return s / float(min(row_pos_static + 1, win))
    pos = pos0 + r0 + lax.broadcasted_iota(jnp.int32, (rb, D_GROUP_DIM), 0)
    cnt = jnp.minimum(pos + 1, win).astype(_F32)
    return s / cnt


def _odd_kernel(*refs, tm, stride, has_state, start_pos, cast_layers, grid_shape):
    ins, srcs, outs, dsts, scr, cast_scr = _split_refs(refs, 12 if has_state else 10, 3, 5, len(cast_layers))
    x_ref, g_ref, win_ref, wout_ref, ccw_ref, ccb_ref, clg_ref, clb_ref, dproj_ref, dscale_ref = ins[:10]
    statec_ref, stated_ref = (ins[10], ins[11]) if has_state else (None, None)
    out_ref, newc_ref, newd_ref = outs
    xn_ref, xpc_ref, xpd_ref, ycat_ref, zb_ref = scr
    finish_cast = _maybe_cast_weights(cast_layers, grid_shape, srcs, dsts, cast_scr)

    rb = SHIFT_ROW_BLOCK if stride == 1 else ROW_BLOCK
    hist_c = (C_CONV - 1) * stride
    hc_rows = _round_up(hist_c, SUBLANES)
    hist_d = POOL_BUF * stride
    hd_rows = _round_up(hist_d, SUBLANES)
    first = pl.program_id(1) == 0
    _init_history(xpc_ref, statec_ref, hist_c, first)
    _init_history(xpd_ref, stated_ref, hist_d, first)
    ccb = ccb_ref[...]
    clg = clg_ref[...]
    clb = clb_ref[...]
    pos0 = start_pos + pl.program_id(1) * tm
    mc = MXU_COLS
    cuts = list(range(0, tm + 1, min(tm, ODD_SUB_TILE)))
    tiles = list(zip(cuts[:-1], cuts[1:]))
    n_sub = len(tiles)


    def in_proj_pieces(s0, s1):
        srows = slice(s0, s1)

        def glu_piece(c0):
            def emit():
                xn = xn_ref[srows, :]
                xpc_ref[hc_rows + s0:hc_rows + s1, c0:c0 + mc] = _dot(xn, win_ref[:, c0:c0 + mc])
                zb = _dot(xn, win_ref[:, C_WIDTH + c0:C_WIDTH + c0 + mc])
                zb_ref[srows, c0:c0 + mc] = zb
                return zb[0:1, 0:LANES]
            return emit

        def pool_in_piece(c0):
            def emit():
                zp = _dot(xn_ref[srows, :], win_ref[:, 2 * C_WIDTH + c0:2 * C_WIDTH + c0 + mc])
                xpd_ref[hd_rows + s0:hd_rows + s1, c0:c0 + mc] = zp
                return zp[0:1, 0:LANES]
            return emit

        return ([glu_piece(c0) for c0 in range(0, C_WIDTH, mc)]
                + [pool_in_piece(c0) for c0 in range(0, D_WIDTH, mc)])

    gated = {}

    def conv_block(s0, s1, r0):
        def emit(after):
            lo = gated.get(s0, s0)
            hi = min(r0 + rb + SUBLANES, s1)
            for g0 in range(lo, hi, rb):
                grows = slice(hc_rows + g0, hc_rows + min(g0 + rb, hi))
                xpc_ref[grows, :] = xpc_ref[grows, :] * jax.nn.sigmoid(zb_ref[g0:min(g0 + rb, hi), :])
            gated[s0] = hi
            parts = []
            for c0 in range(0, C_WIDTH, LANES):
                cols = slice(c0, c0 + LANES)
                if stride % SUBLANES == 0:
                    parts.append(_conv_taps(xpc_ref, ccw_ref, r0, cols, C_CONV, stride, rb))
                else:
                    parts.append(_conv_long_stride1(xpc_ref, ccw_ref, r0, cols, C_CONV, rb))
            cc = jnp.concatenate(parts, axis=-1) + _ordered_after(ccb, after)
            ycat_ref[r0:r0 + rb, 0:C_WIDTH] = jax.nn.silu(_ln_rows(cc, clg, clb)).astype(_BF16)
        return emit

    def pool_block(r0):
        def emit(after):
            for gi, win in enumerate(POOL_WINDOWS):
                cols = slice(gi * D_GROUP_DIM, (gi + 1) * D_GROUP_DIM)
                if stride == 1:
                    static_pos = None if r0 < POOL_BUF else POOL_BUF
                else:
                    static_pos = start_pos + r0 // stride
                pooled = _pool_block(xpd_ref, r0, gi, win, stride, rb, pos0, static_pos)
                diff = (pooled - xpd_ref[hd_rows + r0:hd_rows + r0 + rb, cols]).astype(_BF16)
                scale = dscale_ref[:, cols]
                if gi == 0:
                    scale = _ordered_after(scale, after)
                yd = _dot(diff, dproj_ref[gi]) * scale
                ycat_ref[r0:r0 + rb, C_WIDTH + gi * D_GROUP_DIM:C_WIDTH + (gi + 1) * D_GROUP_DIM] = yd.astype(_BF16)
        return emit

    def out_proj_pieces(s0, s1):
        srows = slice(s0, s1)

        def piece(c0):
            def emit():
                y = _dot(ycat_ref[srows, :], wout_ref[:, c0:c0 + mc])
                out_ref[srows, c0:c0 + mc] = x_ref[srows, c0:c0 + mc] + y
                return y[0:1, 0:LANES]
            return emit

        return [piece(c0) for c0 in range(0, D_MODEL, mc)]

    _norm_to_bf16(x_ref, g_ref, xn_ref, 0, tiles[0][1])
    for emit in in_proj_pieces(*tiles[0]):
        emit()
    for si, (s0, s1) in enumerate(tiles):
        xpc_ref[hc_rows + s1:hc_rows + s1 + SUBLANES, :] = jnp.zeros((SUBLANES, C_WIDTH), _F32)
        side = []
        if si + 1 < n_sub:
            n0, n1 = tiles[si + 1]
            _norm_to_bf16(x_ref, g_ref, xn_ref, n0, n1 - n0)
            side += in_proj_pieces(n0, n1)
        if si >= 1:
            side += out_proj_pieces(*tiles[si - 1])
        blocks = [conv_block(s0, s1, r0) for r0 in range(s0, s1, rb)] + [pool_block(r0) for r0 in range(s0, s1, rb)]
        after = None
        for emit in blocks:
            emit(after)
            after = side.pop(0)() if side else None
        for emit in side:
            emit()
    for emit in out_proj_pieces(*tiles[-1]):
        emit()
    _emit_history(xpc_ref, newc_ref, hist_c, tm, carry=not has_state)
    _emit_history(xpd_ref, newd_ref, hist_d, tm, carry=not has_state)
    finish_cast()


def _ffn_kernel(*refs, stride, has_state, final_norm):
    refs = list(refs)
    x_ref, g_ref, win_ref, wout_ref, cw_ref = refs[:5]
    del refs[:5]
    state_ref = refs.pop(0) if has_state else None
    gfin_ref = refs.pop(0) if final_norm else None
    out_ref, newf_ref, xn_ref, gp_ref, hist_ref, h_ref = refs
    rb = FFN_ROW_BLOCK
    hist = (FFN_CONV - 1) * stride
    h_rows = _round_up(hist, SUBLANES)
    _init_history(hist_ref, state_ref, hist, pl.program_id(1) == 0)
    n_sub, sub, _ = x_ref.shape

    def sub_tile(i, carry):
        xs = x_ref.at[i]
        os = out_ref.at[i]
        _norm_to_bf16(xs, g_ref, xn_ref, 0, sub)
        xn = xn_ref[...]
        for j, c0 in enumerate(range(0, D_FF, FF_CHUNK)):
            cols = slice(c0, c0 + FF_CHUNK)
            slot = slice((j % 2) * FF_CHUNK, (j % 2 + 1) * FF_CHUNK)
            gp_ref[0:h_rows, slot] = hist_ref[:, cols]
            zg = _dot(xn, win_ref[:, c0:c0 + FF_CHUNK])
            zu = _dot(xn, win_ref[:, D_FF + c0:D_FF + c0 + FF_CHUNK])
            for r0 in range(0, sub, rb):
                rows = slice(r0, r0 + rb)
                gp_ref[h_rows + r0:h_rows + r0 + rb, slot] = zg[rows]
                conv = _conv_taps(gp_ref, cw_ref, r0, cols, FFN_CONV, stride, rb, xcols=slot)
                h_ref[rows, cols] = (jax.nn.gelu(conv) * zu[rows]).astype(_BF16)
            hist_ref[:, cols] = gp_ref[sub:sub + h_rows, slot]

        y = xs[...] + _dot(h_ref[...], wout_ref[...])
        if final_norm:
            gfin = gfin_ref[...]
            for r0 in range(0, sub, NORM_BLOCK):
                rows = slice(r0, r0 + NORM_BLOCK)
                os[rows, :] = _rms_rows(y[rows], gfin)
        else:
            os[...] = y
        return carry

    lax.fori_loop(0, n_sub, sub_tile, 0)
    newf_ref[0] = hist_ref[h_rows - hist:h_rows, :]


def _const_spec(shape):
    nd = len(shape)
    return pl.BlockSpec(shape, lambda b, t: (0,) * nd, pipeline_mode=pl.Buffered(1))


def _layer_spec(arr, layer):
    nd = arr.ndim - 1
    return pl.BlockSpec((None,) + arr.shape[1:], lambda b, t: (layer,) + (0,) * nd, pipeline_mode=pl.Buffered(1))


def _smem_spec():
    return pl.BlockSpec(memory_space=pltpu.SMEM)


def _hbm_spec():
    return pl.BlockSpec(memory_space=pl.ANY)


def _tile_spec(tm, nt, width):
    return pl.BlockSpec((tm, width), lambda b, t: (b * nt + t, 0))


def _state_spec(layer, rows, width):
    return pl.BlockSpec((None, 1, rows, width), lambda b, t: (layer, b, 0, 0))


def _new_state_spec(rows, width):
    return pl.BlockSpec((1, rows, width), lambda b, t: (b, 0, 0))


def _params():
    return pltpu.CompilerParams(dimension_semantics=("arbitrary", "arbitrary"), vmem_limit_bytes=VMEM_LIMIT)


def _cast_plumbing(cast_jobs, n_steps):
    in_specs, args, out_specs, out_shape, stage_in, stage_out = [], [], [], [], [], []
    for arr, _ in cast_jobs:
        _, r, c = arr.shape
        rc = r // n_steps
        assert rc * n_steps == r and rc % BF16_ROWS == 0, (arr.shape, n_steps)
        in_specs.append(_hbm_spec())
        args.append(arr)
        out_specs.append(_hbm_spec())
        out_shape.append(jax.ShapeDtypeStruct((r, c), _BF16))
        stage_in.append(pltpu.VMEM((2, rc, c), _F32))
        stage_out.append(pltpu.VMEM((2, rc, c), _BF16))
    scratch = stage_in + stage_out
    if cast_jobs:
        n = len(cast_jobs)
        scratch += [pltpu.SemaphoreType.DMA((n, 2)), pltpu.SemaphoreType.DMA((n, 2))]
    return in_specs, args, out_specs, out_shape, scratch, tuple(layer for _, layer in cast_jobs)


def _even_call(x2, nb, nt, stride, state, layer, win, wout, prm, cast_jobs, name):
    has_state = state is not None
    i = layer // 2
    tm = x2.shape[0] // (nb * nt)
    hist = (B_CONV - 1) * stride
    h_rows = _round_up(hist, SUBLANES)
    c_in, c_args, c_out_specs, c_out_shape, c_scratch, cast_layers = _cast_plumbing(cast_jobs, nb * nt)
    in_specs = [_tile_spec(tm, nt, D_MODEL), _layer_spec(prm['norm_mix_g'], layer), _const_spec(win.shape),
                _const_spec(wout.shape), _layer_spec(prm['a_ln_g'], i), _layer_spec(prm['b_conv_w'], i)]
    args = [x2, prm['norm_mix_g'], win, wout, prm['a_ln_g'], prm['b_conv_w']]
    if has_state:
        in_specs += [_smem_spec(), _smem_spec(), _state_spec(i, hist, B_WIDTH)]
        args += [prm['a_ws_steps'], prm['a_bs_steps'], state]
    else:
        in_specs += [_layer_spec(prm['a_ws'], i), _layer_spec(prm['a_bs_t'], i)]
        args += [prm['a_ws'], prm['a_bs_t']]
    out_shape = [jax.ShapeDtypeStruct(x2.shape, _F32), jax.ShapeDtypeStruct((nb, hist, B_WIDTH), _F32)]
    out_specs = [_tile_spec(tm, nt, D_MODEL), _new_state_spec(hist, B_WIDTH)]
    if has_state:
        out_shape.append(jax.ShapeDtypeStruct((x2.shape[0], A_WIDTH), _F32))
        out_specs.append(_tile_spec(tm, nt, A_WIDTH))
    sub = min(tm, SUB_TILE)
    scratch = [pltpu.VMEM((tm, D_MODEL), _BF16), pltpu.VMEM((h_rows + tm, B_WIDTH), _F32),
               pltpu.VMEM((tm, A_WIDTH + B_WIDTH), _BF16), pltpu.VMEM((tm, A_WIDTH), _BF16),
               pltpu.VMEM((min(tm // sub, 2), sub, win.shape[1]), _F32)]
    return pl.pallas_call(
        functools.partial(_even_kernel, tm=tm, stride=stride, has_state=has_state, gate_layer=i,
                          cast_layers=cast_layers, grid_shape=(nb, nt)),
        grid=(nb, nt), in_specs=in_specs + c_in, out_specs=out_specs + c_out_specs,
        out_shape=out_shape + c_out_shape, scratch_shapes=scratch + c_scratch,
        compiler_params=_params(), name=name)(*args, *c_args)


def _odd_call(x2, nb, nt, stride, state_c, state_d, start_pos, layer, win, wout, prm, cast_jobs, name):
    has_state = state_c is not None
    i = layer // 2
    hist_c = (C_CONV - 1) * stride
    hist_d = POOL_BUF * stride
    hc_rows = _round_up(hist_c, SUBLANES)
    hd_rows = _round_up(hist_d, SUBLANES)
    tm = x2.shape[0] // (nb * nt)
    c_in, c_args, c_out_specs, c_out_shape, c_scratch, cast_layers = _cast_plumbing(cast_jobs, nb * nt)
    names = ['c_conv_w', 'c_conv_b', 'c_ln_g', 'c_ln_b', 'd_proj', 'd_scale']
    in_specs = [_tile_spec(tm, nt, D_MODEL), _layer_spec(prm['norm_mix_g'], layer), _const_spec(win.shape),
                _const_spec(wout.shape)] + [_layer_spec(prm[k], i) for k in names]
    args = [x2, prm['norm_mix_g'], win, wout] + [prm[k] for k in names]
    if has_state:
        in_specs += [_state_spec(i, hist_c, C_WIDTH), _state_spec(i, hist_d, D_WIDTH)]
        args += [state_c, state_d]
    out_shape = [jax.ShapeDtypeStruct(x2.shape, _F32), jax.ShapeDtypeStruct((nb, hist_c, C_WIDTH), _F32),
                 jax.ShapeDtypeStruct((nb, hist_d, D_WIDTH), _F32)]
    out_specs = [_tile_spec(tm, nt, D_MODEL), _new_state_spec(hist_c, C_WIDTH), _new_state_spec(hist_d, D_WIDTH)]
    scratch = [pltpu.VMEM((tm, D_MODEL), _BF16), pltpu.VMEM((hc_rows + tm + SUBLANES, C_WIDTH), _F32),
               pltpu.VMEM((hd_rows + tm, D_WIDTH), _F32), pltpu.VMEM((tm, C_WIDTH + D_WIDTH), _BF16),
               pltpu.VMEM((tm, C_WIDTH), _F32)]
    return pl.pallas_call(
        functools.partial(_odd_kernel, tm=tm, stride=stride, has_state=has_state, start_pos=start_pos,
                          cast_layers=cast_layers, grid_shape=(nb, nt)),
        grid=(nb, nt), in_specs=in_specs + c_in, out_specs=out_specs + c_out_specs,
        out_shape=out_shape + c_out_shape, scratch_shapes=scratch + c_scratch,
        compiler_params=_params(), name=name)(*args, *c_args)


def _ffn_call(x2, nb, nt, stride, state, layer, final_norm, win, wout, prm, name):
    has_state = state is not None
    hist = (FFN_CONV - 1) * stride
    h_rows = _round_up(hist, SUBLANES)
    tm = x2.shape[0] // (nb * nt)
    sub = min(tm, SUB_TILE)
    n_sub = tm // sub
    x3 = x2.reshape(-1, sub, D_MODEL)
    x_spec = pl.BlockSpec((n_sub, sub, D_MODEL), lambda b, t: (b * nt + t, 0, 0))
    in_specs = [x_spec, _layer_spec(prm['norm_ffn_g'], layer), _const_spec(win.shape), _const_spec(wout.shape),
                _layer_spec(prm['ffn_conv_w'], layer)]
    args = [x3, prm['norm_ffn_g'], win, wout, prm['ffn_conv_w']]
    if has_state:
        in_specs.append(_state_spec(layer, hist, D_FF))
        args.append(state)
    if final_norm:
        in_specs.append(_const_spec((1, D_MODEL)))
        args.append(prm['norm_final_g'])
    out_shape = [jax.ShapeDtypeStruct(x3.shape, _F32), jax.ShapeDtypeStruct((nb, hist, D_FF), _F32)]
    out_specs = [x_spec, _new_state_spec(hist, D_FF)]
    scratch = [pltpu.VMEM((sub, D_MODEL), _BF16), pltpu.VMEM((h_rows + sub, 2 * FF_CHUNK), _F32),
               pltpu.VMEM((h_rows, D_FF), _F32), pltpu.VMEM((sub, D_FF), _BF16)]
    y3, newf = pl.pallas_call(
        functools.partial(_ffn_kernel, stride=stride, has_state=has_state, final_norm=final_norm),
        grid=(nb, nt), in_specs=in_specs, out_specs=out_specs, out_shape=out_shape,
        scratch_shapes=scratch, compiler_params=_params(), name=name)(*args)
    return y3.reshape(x2.shape), newf


def _to_tiles(a, ns):
    *lead, n, l, c = a.shape
    k = len(lead)
    a = a.reshape(*lead, n // ns, ns, l, c)
    a = a.transpose(*range(k), k, k + 2, k + 1, k + 3)
    return a.reshape(*lead, n // ns, l * ns, c)


def _from_tiles(a, ns, l):
    *lead, nb, _, c = a.shape
    k = len(lead)
    a = a.reshape(*lead, nb, l, ns, c)
    a = a.transpose(*range(k), k, k + 2, k + 1, k + 3)
    return a.reshape(*lead, nb * ns, l, c)


def _trunk(x2, nb, nt, stride, states, start_pos, prm, wb, raw, tag):
    new_a, new_b, new_c, new_d, new_f = [], [], [], [], []
    depth = prm['ffn_conv_w'].shape[0]
    for layer in range(depth):
        cast_keys, cast_jobs = [], []
        if raw is not None:
            cast_keys.append(('ffn', layer))
            cast_jobs += [(raw['w_ffn_in'], layer), (raw['w_ffn_out'], layer)]
            if layer + 1 < depth:
                nxt = 'even' if (layer + 1) % 2 == 0 else 'odd'
                cast_keys.append(('mix', layer + 1))
                cast_jobs += [(raw['w_in_' + nxt], (layer + 1) // 2), (raw['w_out_' + nxt], (layer + 1) // 2)]
        win, wout = wb[('mix', layer)]
        if layer % 2 == 0:
            st = None if states is None else states['b']
            res = _even_call(x2, nb, nt, stride, st, layer, win, wout, prm, cast_jobs, name=f"{tag}_mix{layer}")
            n_own = 2 if states is None else 3
            x2 = res[0]
            new_b.append(res[1])
            if states is not None:
                new_a.append(res[2])
        else:
            st_c = None if states is None else states['c']
            st_d = None if states is None else states['d']
            res = _odd_call(x2, nb, nt, stride, st_c, st_d, start_pos, layer, win, wout, prm, cast_jobs,
                            name=f"{tag}_mix{layer}")
            n_own = 3
            x2 = res[0]
            new_c.append(res[1])
            new_d.append(res[2])
        for k, key in enumerate(cast_keys):
            wb[key] = (res[n_own + 2 * k], res[n_own + 2 * k + 1])
        st_f = None if states is None else states['f']
        win, wout = wb[('ffn', layer)]
        x2, nf = _ffn_call(x2, nb, nt, stride, st_f, layer, layer == depth - 1, win, wout, prm,
                           name=f"{tag}_ffn{layer}")
        new_f.append(nf)
    return x2, new_a, new_b, new_c, new_d, new_f


def kernel(x_prompt, x_sample, state_b_conv, state_c_conv, state_d_pool, state_ffn_conv, norm_mix_g, norm_ffn_g, norm_final_g, w_in_even, w_out_even, a_ln_g, a_ws, a_bs, b_conv_w, w_in_odd, w_out_odd, c_conv_w, c_conv_b, c_ln_g, c_ln_b, d_proj, d_scale, w_ffn_in, ffn_conv_w, w_ffn_out):
    batch, seq, d = x_prompt.shape
    dec_batch, dec_seq, _ = x_sample.shape
    ns = SAMPLE_TILE // dec_seq

    def rows(p):
        return p.reshape(p.shape[0], 1, p.shape[1])

    prm = dict(norm_mix_g=rows(norm_mix_g), norm_ffn_g=rows(norm_ffn_g), norm_final_g=norm_final_g.reshape(1, -1),
               a_ln_g=rows(a_ln_g), a_ws=a_ws, a_bs_t=jnp.swapaxes(a_bs, 1, 2),
               a_ws_steps=a_ws[:, :, :dec_seq, :dec_seq].reshape(-1), a_bs_steps=a_bs[:, :, :dec_seq].reshape(-1),
               b_conv_w=b_conv_w,
               c_conv_w=jnp.pad(c_conv_w, ((0, 0), (0, _round_up(C_CONV, SUBLANES) - C_CONV), (0, 0))),
               c_conv_b=rows(c_conv_b), c_ln_g=rows(c_ln_g), c_ln_b=rows(c_ln_b), d_proj=d_proj.astype(_BF16),
               d_scale=rows(d_scale), ffn_conv_w=ffn_conv_w)
    raw = dict(w_in_even=w_in_even, w_out_even=w_out_even, w_in_odd=w_in_odd, w_out_odd=w_out_odd,
               w_ffn_in=w_ffn_in, w_ffn_out=w_ffn_out)
    wb = {('mix', 0): (w_in_even[0].astype(_BF16), w_out_even[0].astype(_BF16))}

    yp, _, b_p, c_p, d_p, f_p = _trunk(x_prompt.reshape(batch * seq, d), batch, seq // PROMPT_TILE, 1, None, 0,
                                       prm, wb, raw, "p")

    states = dict(b=_to_tiles(state_b_conv, ns), c=_to_tiles(state_c_conv, ns), d=_to_tiles(state_d_pool, ns),
                  f=_to_tiles(state_ffn_conv, ns))
    xs = _to_tiles(x_sample, ns).reshape(dec_batch * dec_seq, d)
    ys, a_s, b_s, c_s, d_s, f_s = _trunk(xs, dec_batch // ns, 1, ns, states, PAST_LEN, prm, wb, None, "s")

    def untile(parts, l):
        return _from_tiles(jnp.stack(parts), ns, l)

    y_prompt = yp.reshape(batch, seq, d)
    y_sample = _from_tiles(ys.reshape(dec_batch // ns, dec_seq * ns, d), ns, dec_seq)
    new_a = untile([a.reshape(dec_batch // ns, dec_seq * ns, A_WIDTH) for a in a_s], dec_seq)
    return (y_prompt, y_sample, new_a, jnp.stack(b_p), untile(b_s, B_CONV - 1), jnp.stack(c_p),
            untile(c_s, C_CONV - 1), jnp.stack(d_p), untile(d_s, POOL_BUF), jnp.stack(f_p),
            untile(f_s, FFN_CONV - 1))
```
